```python
import math
import jax, jax.numpy as jnp
from jax import lax
import numpy as np

D_MODEL = 2048
BATCH = 4
SEQ = 2048
DEPTH = 1

CHUNK = 64
Q_BLOCK = 128
EPS = 1e-6
MIX_WIDTH = D_MODEL
DA_WIDTH = MIX_WIDTH // 2
DA_V_DIM = 128
DA_QK_DIM = DA_V_DIM // 2
DA_HEADS = DA_WIDTH // DA_V_DIM
GLA_WIDTH = MIX_WIDTH - DA_WIDTH
GLA_HEADS = 4
GLA_DV = GLA_WIDTH // GLA_HEADS
GLA_DK = GLA_DV // 2
GLA_GATE_RANK = 16
GLA_TAU = 16.0
N_EXPERTS = 32
TOP_K = 4
D_FF = D_MODEL
SWIGLU_LIMIT = 7.0
SWIGLU_ALPHA = 1.702
SPLITS = (DA_HEADS * 2 * DA_QK_DIM, DA_HEADS * 2 * DA_QK_DIM, DA_HEADS * DA_V_DIM,
          GLA_HEADS * GLA_DK, GLA_HEADS * GLA_DK, GLA_HEADS * GLA_DV, GLA_HEADS * GLA_DV,
          GLA_GATE_RANK)
IN_COLS = sum(SPLITS)

kernel_name = 'hybrid_diffattn_gla_moe_block'


def rms_norm(x, w):
    xf = x.astype(jnp.float32)
    y = xf * lax.rsqrt(jnp.mean(xf * xf, axis=-1, keepdims=True) + EPS)
    return (y * w).astype(x.dtype)


def split_cols(y):
    idx = np.cumsum(SPLITS)[:-1].tolist()
    return jnp.split(y, idx, axis=-1)


def diff_attention(q, k, v, q_norm_w, k_norm_w, lam, subln_w, lambda_init):
    B, S, H, _, dq = q.shape
    dv = v.shape[-1]
    q = rms_norm(q, q_norm_w) * (dq ** -0.5)
    k = rms_norm(k, k_norm_w)
    nb = S // Q_BLOCK
    qb = q.reshape(B, nb, Q_BLOCK, H, 2, dq).transpose(1, 0, 2, 3, 4, 5)
    key_chunk = jnp.arange(S) // CHUNK

    def block(args):
        qi, bi = args
        scores = jnp.einsum('bqhmd,bkhmd->bhmqk', qi, k).astype(jnp.float32)
        q_chunk = (bi * Q_BLOCK + jnp.arange(Q_BLOCK)) // CHUNK
        mask = key_chunk[None, :] <= q_chunk[:, None]
        scores = jnp.where(mask[None, None, None], scores, -jnp.inf)
        p = jax.nn.softmax(scores, axis=-1)
        attn = p[:, :, 0] - lam * p[:, :, 1]
        return jnp.einsum('bhqk,bkhd->bqhd', attn.astype(v.dtype), v)

    o = lax.map(block, (qb, jnp.arange(nb)))
    o = o.transpose(1, 0, 2, 3, 4).reshape(B, S, H, dv)
    o = rms_norm(o, subln_w) * (1.0 - lambda_init)
    return o.reshape(B, S, H * dv)


def gla(q, k, v, log_g, r, norm_w):
    B, S, H, dk = q.shape
    dv = v.shape[-1]
    nc = S // CHUNK

    def to_chunks(t):
        return t.reshape(B, nc, CHUNK, H, t.shape[-1]).transpose(1, 0, 3, 2, 4)

    causal = jnp.tril(jnp.ones((CHUNK, CHUNK), dtype=bool))

    def step(state, inp):
        qc, kc, vc, gc = inp
        b = jnp.cumsum(gc.astype(jnp.float32), axis=2)
        o_inter = jnp.einsum('bhcd,bhde->bhce', qc * jnp.exp(b), state)
        diff = b[:, :, :, None, :] - b[:, :, None, :, :]
        decay = jnp.exp(jnp.where(causal[None, None, :, :, None], diff, -jnp.inf))
        A = jnp.einsum('bhid,bhjd,bhijd->bhij', qc, kc, decay)
        o_intra = jnp.einsum('bhij,bhje->bhie', A, vc)
        b_last = b[:, :, -1:, :]
        state = jnp.exp(b_last[:, :, 0, :])[..., None] * state + jnp.einsum(
            'bhjd,bhje->bhde', kc * jnp.exp(b_last - b), vc)
        return state, o_inter + o_intra

    state0 = jnp.zeros((B, H, dk, dv), jnp.float32)
    xs = (to_chunks(q * (dk ** -0.5)), to_chunks(k), to_chunks(v), to_chunks(log_g))
    _, o = lax.scan(step, state0, xs)
    o = o.transpose(1, 0, 3, 2, 4).reshape(B, S, H, dv).astype(v.dtype)
    o = rms_norm(o, norm_w) * jax.nn.silu(r)
    return o.reshape(B, S, H * dv)


def moe(h, router_w, router_b, w_gate, b_gate, w_up, b_up, w_down, b_down):
    B, S, D = h.shape
    t = h.reshape(B * S, D)
    logits = (t @ router_w + router_b).astype(jnp.float32)
    top_vals, top_idx = lax.top_k(logits, TOP_K)
    top_w = jax.nn.softmax(top_vals, axis=-1)
    comb = jnp.einsum('tk,tke->te', top_w, jax.nn.one_hot(top_idx, N_EXPERTS, dtype=jnp.float32))
    y = jnp.zeros((B * S, D), jnp.float32)
    for e in range(N_EXPERTS):
        gate = jnp.minimum(t @ w_gate[e] + b_gate[e], SWIGLU_LIMIT)
        up = jnp.clip(t @ w_up[e] + b_up[e], -SWIGLU_LIMIT, SWIGLU_LIMIT)
        act = (up + 1.0) * gate * jax.nn.sigmoid(SWIGLU_ALPHA * gate)
        y = y + comb[:, e:e + 1] * (act @ w_down[e] + b_down[e])
    return y.reshape(B, S, D).astype(h.dtype)


def setup_inputs(seed: int = 0) -> dict:
    key = jax.random.key(seed)
    ks = jax.random.split(key, 24)
    f32 = jnp.float32

    def nrm(k, shape, scale):
        return jax.random.normal(k, shape, f32) * scale

    def gain(k, shape):
        return 1.0 + 0.05 * jax.random.normal(k, shape, f32)

    L = DEPTH
    return {
        'x': nrm(ks[0], (BATCH, SEQ, D_MODEL), 1.0),
        'attn_norm_w': gain(ks[1], (L, D_MODEL)),
        'w_in': nrm(ks[2], (L, D_MODEL, IN_COLS), D_MODEL ** -0.5),
        'da_q_norm_w': gain(ks[3], (L, DA_QK_DIM)),
        'da_k_norm_w': gain(ks[4], (L, DA_QK_DIM)),
        'da_lambda_q1': nrm(ks[5], (L, DA_QK_DIM), 0.1),
        'da_lambda_k1': nrm(ks[6], (L, DA_QK_DIM), 0.1),
        'da_lambda_q2': nrm(ks[7], (L, DA_QK_DIM), 0.1),
        'da_lambda_k2': nrm(ks[8], (L, DA_QK_DIM), 0.1),
        'da_subln_w': gain(ks[9], (L, DA_V_DIM)),
        'gla_gate_up_w': nrm(ks[10], (L, GLA_GATE_RANK, GLA_HEADS * GLA_DK), GLA_GATE_RANK ** -0.5),
        'gla_gate_up_b': nrm(ks[11], (L, GLA_HEADS * GLA_DK), 0.1),
        'gla_norm_w': gain(ks[12], (L, GLA_DV)),
        'w_out': nrm(ks[13], (L, MIX_WIDTH, D_MODEL), MIX_WIDTH ** -0.5),
        'ffn_norm_w': gain(ks[14], (L, D_MODEL)),
        'router_w': nrm(ks[15], (L, D_MODEL, N_EXPERTS), D_MODEL ** -0.5),
        'router_b': nrm(ks[16], (L, N_EXPERTS), 0.01),
        'exp_w_gate': nrm(ks[17], (L, N_EXPERTS, D_MODEL, D_FF), D_MODEL ** -0.5),
        'exp_b_gate': nrm(ks[18], (L, N_EXPERTS, D_FF), 0.02),
        'exp_w_up': nrm(ks[19], (L, N_EXPERTS, D_MODEL, D_FF), D_MODEL ** -0.5),
        'exp_b_up': nrm(ks[20], (L, N_EXPERTS, D_FF), 0.02),
        'exp_w_down': nrm(ks[21], (L, N_EXPERTS, D_FF, D_MODEL), D_FF ** -0.5),
        'exp_b_down': nrm(ks[22], (L, N_EXPERTS, D_MODEL), 0.02),
    }


def reference(x, attn_norm_w, w_in, da_q_norm_w, da_k_norm_w, da_lambda_q1, da_lambda_k1,
              da_lambda_q2, da_lambda_k2, da_subln_w, gla_gate_up_w, gla_gate_up_b, gla_norm_w,
              w_out, ffn_norm_w, router_w, router_b, exp_w_gate, exp_b_gate, exp_w_up, exp_b_up,
              exp_w_down, exp_b_down):
    B, S, _ = x.shape
    h = x
    for l in range(DEPTH):
        lambda_init = 0.8 - 0.6 * math.exp(-0.3 * l)
        hn = rms_norm(h, attn_norm_w[l])
        proj = jnp.einsum('bsd,dp->bsp', hn, w_in[l])
        da_q, da_k, da_v, g_q, g_k, g_v, g_r, g_down = split_cols(proj)
        lam = (jnp.exp(jnp.sum(da_lambda_q1[l].astype(jnp.float32) * da_lambda_k1[l]))
               - jnp.exp(jnp.sum(da_lambda_q2[l].astype(jnp.float32) * da_lambda_k2[l]))
               + lambda_init)
        out_a = diff_attention(da_q.reshape(B, S, DA_HEADS, 2, DA_QK_DIM),
                               da_k.reshape(B, S, DA_HEADS, 2, DA_QK_DIM),
                               da_v.reshape(B, S, DA_HEADS, DA_V_DIM),
                               da_q_norm_w[l], da_k_norm_w[l], lam, da_subln_w[l], lambda_init)
        log_g = jax.nn.log_sigmoid((g_down @ gla_gate_up_w[l] + gla_gate_up_b[l]).astype(jnp.float32)) / GLA_TAU
        out_b = gla(g_q.reshape(B, S, GLA_HEADS, GLA_DK),
                    g_k.reshape(B, S, GLA_HEADS, GLA_DK),
                    g_v.reshape(B, S, GLA_HEADS, GLA_DV),
                    log_g.reshape(B, S, GLA_HEADS, GLA_DK),
                    g_r.reshape(B, S, GLA_HEADS, GLA_DV),
                    gla_norm_w[l])
        mix = jnp.concatenate([out_a, out_b], axis=-1)
        h = h + jnp.einsum('bsm,md->bsd', mix, w_out[l])
        h = h + moe(rms_norm(h, ffn_norm_w[l]), router_w[l], router_b[l], exp_w_gate[l], exp_b_gate[l],
                    exp_w_up[l], exp_b_up[l], exp_w_down[l], exp_b_down[l])
    return h
```

```python
import functools
import math

import jax
import jax.numpy as jnp
from jax import lax
from jax.experimental import pallas as pl
from jax.experimental.pallas import tpu as pltpu

F32 = jnp.float32
BF16 = jnp.bfloat16
I32 = jnp.int32

D_MODEL = 2048
CHUNK = 64
EPS = 1e-6
DA_HEADS = 8
DA_QK = 64
DA_V = 128
GLA_HEADS = 4
GLA_DK = 128
GLA_DV = 256
GLA_RANK = 16
GLA_TAU = 16.0
N_EXPERTS = 32
TOP_K = 4
D_FF = 2048
SWIGLU_LIMIT = 7.0
SWIGLU_ALPHA = 1.702
LAMBDA_INIT = 0.8 - 0.6 * math.exp(-0.3 * 0)

IN_COLS = 6160
IN_COLS_PAD = 6272
LANE = 128
ROW_CHUNK = 16

IN_TM = 1024
IN_TN = 896
ATT_BQ = 256
TOK_TILE = 512
SLOTS = TOP_K * TOK_TILE + N_EXPERTS * ROW_CHUNK
SLOT_BLK = 256
MOE_TM = 256
MOE_FC = 1024
VMEM_LIMIT = 56 * 1024 * 1024


def _cparams(sem):
    return pltpu.CompilerParams(dimension_semantics=sem, vmem_limit_bytes=VMEM_LIMIT)


def _in_proj_kernel(x_ref, nw_ref, w_ref, o_ref, hn_ref):
    @pl.when(pl.program_id(1) == 0)
    def _():
        x = x_ref[...]
        ms = jnp.mean(x * x, axis=-1, keepdims=True)
        hn_ref[...] = (x * lax.rsqrt(ms + EPS) * nw_ref[...]).astype(BF16)

    o_ref[...] = jnp.dot(hn_ref[...], w_ref[...], preferred_element_type=F32).astype(o_ref.dtype)


def _in_proj(x2, norm_w, w_bf):
    t = x2.shape[0]
    tm = min(IN_TM, t)
    return pl.pallas_call(
        _in_proj_kernel,
        grid=(t // tm, IN_COLS_PAD // IN_TN),
        in_specs=[
            pl.BlockSpec((tm, D_MODEL), lambda i, j: (i, 0)),
            pl.BlockSpec((1, D_MODEL), lambda i, j: (0, 0)),
            pl.BlockSpec((D_MODEL, IN_TN), lambda i, j: (0, j)),
        ],
        out_specs=pl.BlockSpec((tm, IN_TN), lambda i, j: (i, j)),
        out_shape=jax.ShapeDtypeStruct((t, IN_COLS_PAD), BF16),
        scratch_shapes=[pltpu.VMEM((tm, D_MODEL), BF16)],
        compiler_params=_cparams(("parallel", "arbitrary")),
        name="in_proj",
    )(x2, norm_w, w_bf)


def _attn_kernel(q_ref, k_ref, v_ref, qw_ref, kw_ref, lq1_ref, lk1_ref, lq2_ref, lk2_ref, sw_ref,
                 o_ref, kn_ref, *, bq):
    qi = pl.program_id(2)
    lane = lax.broadcasted_iota(I32, (1, LANE), 1)
    first_map = lane < DA_QK

    def qk_norm(x, w):
        x2 = x * x
        s1 = jnp.sum(jnp.where(first_map, x2, 0.0), axis=-1, keepdims=True)
        s2 = jnp.sum(jnp.where(first_map, 0.0, x2), axis=-1, keepdims=True)
        r = jnp.where(first_map, lax.rsqrt(s1 / DA_QK + EPS), lax.rsqrt(s2 / DA_QK + EPS))
        return x * r * w

    @pl.when(qi == 0)
    def _():
        kn_ref[...] = qk_norm(k_ref[...].astype(F32), kw_ref[...]).astype(BF16)

    q = qk_norm(q_ref[...].astype(F32), qw_ref[...]) * (DA_QK ** -0.5)
    lhs = jnp.concatenate([jnp.where(first_map, q, 0.0), jnp.where(first_map, 0.0, q)], axis=0).astype(BF16)

    def step(j, carry, masked):
        m, l, acc = carry
        r0 = pl.multiple_of(j * bq, bq)
        kb = kn_ref[pl.ds(r0, bq), :]
        vb = v_ref[pl.ds(r0, bq), :]
        s = lax.dot_general(lhs, kb, (((1,), (1,)), ((), ())), preferred_element_type=F32)
        if masked:
            shift = CHUNK.bit_length() - 1
            row_chunk = (lax.broadcasted_iota(I32, (2 * bq, bq), 0) & (bq - 1)) >> shift
            col_chunk = lax.broadcasted_iota(I32, (2 * bq, bq), 1) >> shift
            s = jnp.where(col_chunk <= row_chunk, s, -jnp.inf)
        m_new = jnp.maximum(m, jnp.max(s, axis=-1, keepdims=True))
        alpha = jnp.exp(m - m_new)
        p = jnp.exp(s - m_new)
        l = alpha * l + jnp.sum(p, axis=-1, keepdims=True)
        acc = alpha * acc + jnp.dot(p.astype(BF16), vb, preferred_element_type=F32)
        return m_new, l, acc

    init = (jnp.full((2 * bq, 1), -jnp.inf, F32), jnp.zeros((2 * bq, 1), F32), jnp.zeros((2 * bq, DA_V), F32))
    carry = lax.fori_loop(0, qi, lambda j, c: step(j, c, False), init)
    _, l, acc = step(qi, carry, True)
    o = acc / l
    lam = (jnp.exp(jnp.sum(lq1_ref[...] * lk1_ref[...], axis=-1, keepdims=True))
           - jnp.exp(jnp.sum(lq2_ref[...] * lk2_ref[...], axis=-1, keepdims=True)) + LAMBDA_INIT)
    out = o[:bq] - lam * o[bq:]
    ms = jnp.mean(out * out, axis=-1, keepdims=True)
    o_ref[...] = (out * lax.rsqrt(ms + EPS) * sw_ref[...] * (1.0 - LAMBDA_INIT)).astype(o_ref.dtype)


def _diff_attn(proj3, qw, kw, lq1, lk1, lq2, lk2, sw):
    b, s, _ = proj3.shape
    bq = min(ATT_BQ, s)
    vec = lambda n: pl.BlockSpec((1, n), lambda bi, h, qi: (0, 0))
    return pl.pallas_call(
        functools.partial(_attn_kernel, bq=bq),
        grid=(b, DA_HEADS, s // bq),
        in_specs=[
            pl.BlockSpec((None, bq, LANE), lambda bi, h, qi: (bi, qi, h)),
            pl.BlockSpec((None, s, LANE), lambda bi, h, qi: (bi, 0, DA_HEADS + h)),
            pl.BlockSpec((None, s, LANE), lambda bi, h, qi: (bi, 0, 2 * DA_HEADS + h)),
            vec(LANE), vec(LANE), vec(DA_QK), vec(DA_QK), vec(DA_QK), vec(DA_QK), vec(DA_V),
        ],
        out_specs=pl.BlockSpec((None, bq, DA_V), lambda bi, h, qi: (bi, qi, h)),
        out_shape=jax.ShapeDtypeStruct((b, s, DA_HEADS * DA_V), BF16),
        scratch_shapes=[pltpu.VMEM((s, LANE), BF16)],
        compiler_params=_cparams(("parallel", "parallel", "arbitrary")),
        name="diff_attn",
    )(proj3, proj3, proj3, qw, kw, lq1, lk1, lq2, lk2, sw)


GLA_SUB = 16


def _gla_kernel(q_ref, k_ref, v_ref, r_ref, gd_ref, wup_ref, bup_ref, nw_ref, o_ref, st_ref, *, n_chunks):
    st_ref[...] = jnp.zeros_like(st_ref)
    rr = lax.broadcasted_iota(I32, (CHUNK, CHUNK), 0)
    cc = lax.broadcasted_iota(I32, (CHUNK, CHUNK), 1)
    tril = jnp.where(cc <= rr, 1.0, 0.0).astype(BF16)
    sub_row = lax.broadcasted_iota(I32, (GLA_SUB, GLA_DK), 0)
    sub_lane = lax.broadcasted_iota(I32, (GLA_SUB, CHUNK), 1)
    chunk_row = lax.broadcasted_iota(I32, (CHUNK, GLA_DK), 0)
    wup = wup_ref[...]
    nt = (((1,), (1,)), ((), ()))
    tn = (((0,), (0,)), ((), ()))

    def chunk(c, _):
        r0 = pl.multiple_of(c * CHUNK, CHUNK)
        q = q_ref[pl.ds(r0, CHUNK), :].astype(F32) * (GLA_DK ** -0.5)
        k = k_ref[pl.ds(r0, CHUNK), :].astype(F32)
        v = v_ref[pl.ds(r0, CHUNK), :]
        z = jnp.dot(gd_ref[pl.ds(r0, CHUNK), :], wup, preferred_element_type=F32) + bup_ref[...]
        lg = (jnp.minimum(z, 0.0) - jnp.log(1.0 + jnp.exp(-jnp.abs(z)))) / GLA_TAU
        lg_hi = lg.astype(BF16)
        lg_lo = (lg - lg_hi.astype(F32)).astype(BF16)
        bcum = (jnp.dot(tril, lg_hi, preferred_element_type=F32)
                + jnp.dot(tril, lg_lo, preferred_element_type=F32))
        bprev = bcum - lg
        st = st_ref[...]
        o_inter = lax.dot_general((q * jnp.exp(bcum)).astype(BF16), st.astype(BF16), nt,
                                  preferred_element_type=F32)
        for blk in range(CHUNK // GLA_SUB):
            lo = blk * GLA_SUB
            hi = lo + GLA_SUB
            b_blk = bcum[lo:hi]
            q_blk = q[lo:hi]
            if blk > 0:
                ref = bprev[lo:lo + 1]
                qt = (q_blk * jnp.exp(b_blk - ref)).astype(BF16)
                kt = (k * jnp.exp(jnp.where(chunk_row < lo, ref - bcum, -jnp.inf))).astype(BF16)
                a_blk = lax.dot_general(qt, kt, nt, preferred_element_type=F32)
            else:
                a_blk = jnp.zeros((GLA_SUB, CHUNK), F32)
            for jj in range(GLA_SUB):
                dlt = jnp.where(sub_row >= jj, b_blk - bcum[lo + jj:lo + jj + 1], -jnp.inf)
                col = jnp.sum(q_blk * k[lo + jj:lo + jj + 1] * jnp.exp(dlt), axis=-1, keepdims=True)
                a_blk = a_blk + jnp.where(sub_lane == lo + jj, col, 0.0)
            o_blk = o_inter[lo:hi] + jnp.dot(a_blk.astype(BF16), v, preferred_element_type=F32)
            ms = jnp.mean(o_blk * o_blk, axis=-1, keepdims=True)
            rg = r_ref[pl.ds(r0 + lo, GLA_SUB), :].astype(F32)
            y = o_blk * lax.rsqrt(ms + EPS) * nw_ref[...] * (rg / (1.0 + jnp.exp(-rg)))
            o_ref[pl.ds(r0 + lo, GLA_SUB), :] = y.astype(o_ref.dtype)
        b_last = bcum[CHUNK - 1:CHUNK]
        k_dec = (k * jnp.exp(b_last - bcum)).astype(BF16)
        st_ref[...] = st * jnp.exp(b_last) + lax.dot_general(v, k_dec, tn, preferred_element_type=F32)
        return 0

    lax.fori_loop(0, n_chunks, chunk, 0)


def _gla(proj3, wup_pad, bup, nw):
    b, s, _ = proj3.shape
    q0 = 3 * DA_HEADS * DA_V // LANE
    k0 = q0 + GLA_HEADS
    v0 = (k0 + GLA_HEADS) * LANE // GLA_DV
    r0 = v0 + GLA_HEADS
    gd = (IN_COLS - GLA_RANK) // LANE
    return pl.pallas_call(
        functools.partial(_gla_kernel, n_chunks=s // CHUNK),
        grid=(b, GLA_HEADS),
        in_specs=[
            pl.BlockSpec((None, s, GLA_DK), lambda bi, h: (bi, 0, q0 + h)),
            pl.BlockSpec((None, s, GLA_DK), lambda bi, h: (bi, 0, k0 + h)),
            pl.BlockSpec((None, s, GLA_DV), lambda bi, h: (bi, 0, v0 + h)),
            pl.BlockSpec((None, s, GLA_DV), lambda bi, h: (bi, 0, r0 + h)),
            pl.BlockSpec((None, s, LANE), lambda bi, h: (bi, 0, gd)),
            pl.BlockSpec((LANE, GLA_DK), lambda bi, h: (0, h)),
            pl.BlockSpec((1, GLA_DK), lambda bi, h: (0, h)),
            pl.BlockSpec((1, GLA_DV), lambda bi, h: (0, 0)),
        ],
        out_specs=pl.BlockSpec((None, s, GLA_DV), lambda bi, h: (bi, 0, h)),
        out_shape=jax.ShapeDtypeStruct((b, s, GLA_HEADS * GLA_DV), BF16),
        scratch_shapes=[pltpu.VMEM((GLA_DV, GLA_DK), F32)],
        compiler_params=_cparams(("parallel", "parallel")),
        name="gla",
    )(proj3, proj3, proj3, proj3, proj3, wup_pad, bup, nw)


def _out_router_kernel(x_ref, a_ref, b_ref, wo_ref, nw_ref, rwt_ref, rb_ref,
                       h_ref, hn_ref, slot_ref, w_ref, nch_ref):
    tm = x_ref.shape[0]
    half = a_ref.shape[1]
    h = (x_ref[...]
         + jnp.dot(a_ref[...], wo_ref[0:half, :], preferred_element_type=F32)
         + jnp.dot(b_ref[...], wo_ref[half:2 * half, :], preferred_element_type=F32))
    h_ref[...] = h
    ms = jnp.mean(h * h, axis=-1, keepdims=True)
    hn = h * lax.rsqrt(ms + EPS) * nw_ref[...]
    hn_hi = hn.astype(BF16)
    hn_ref[...] = hn_hi
    hn_lo = (hn - hn_hi.astype(F32)).astype(BF16)
    rwt = rwt_ref[...]
    rw_hi = rwt.astype(BF16)
    rw_lo = (rwt - rw_hi.astype(F32)).astype(BF16)
    nt = (((1,), (1,)), ((), ()))
    logits = (lax.dot_general(rw_hi, hn_hi, nt, preferred_element_type=F32)
              + lax.dot_general(rw_hi, hn_lo, nt, preferred_element_type=F32)
              + lax.dot_general(rw_lo, hn_hi, nt, preferred_element_type=F32)
              + rb_ref[...])
    e_iota = lax.broadcasted_iota(I32, (N_EXPERTS, tm), 0).astype(F32)
    vals, hots = [], []
    for _ in range(TOP_K):
        m = jnp.max(logits, axis=0, keepdims=True)
        idx = jnp.min(jnp.where(logits == m, e_iota, float(N_EXPERTS)), axis=0, keepdims=True)
        hot = e_iota == idx
        vals.append(m)
        hots.append(hot)
        logits = jnp.where(hot, -jnp.inf, logits)
    exps = [jnp.exp(v - vals[0]) for v in vals]
    denom = exps[0] + exps[1] + exps[2] + exps[3]
    for kk in range(TOP_K):
        w_ref[kk:kk + 1, :] = exps[kk] / denom
    sel = jnp.zeros((N_EXPERTS, tm), F32)
    for hot in hots:
        sel = sel + jnp.where(hot, 1.0, 0.0)
    tr = lax.broadcasted_iota(I32, (tm, tm), 0)
    tc = lax.broadcasted_iota(I32, (tm, tm), 1)
    upper = jnp.where(tr < tc, 1.0, 0.0).astype(BF16)
    rank = jnp.dot(sel.astype(BF16), upper, preferred_element_type=F32)
    cnt = jnp.sum(sel, axis=1, keepdims=True)
    nch = jnp.floor((cnt + (ROW_CHUNK - 1)) * (1.0 / ROW_CHUNK))
    nch_b = jnp.broadcast_to(nch, (N_EXPERTS, LANE))
    er = lax.broadcasted_iota(I32, (N_EXPERTS, N_EXPERTS), 0)
    ec = lax.broadcasted_iota(I32, (N_EXPERTS, N_EXPERTS), 1)
    lower = jnp.where(ec < er, 1.0, 0.0).astype(BF16)
    seg = jnp.dot(lower, nch_b.astype(BF16), preferred_element_type=F32)
    pos = seg[:, 0:1] * ROW_CHUNK + rank
    for kk in range(TOP_K):
        slot_ref[kk:kk + 1, :] = jnp.sum(jnp.where(hots[kk], pos, 0.0), axis=0, keepdims=True).astype(I32)
    nch_ref[...] = nch_b.astype(I32)


def _out_router(x2, out_a, out_b, wo_bf, nw, rwt, rb):
    t = x2.shape[0]
    tm = TOK_TILE
    half = out_a.shape[1]
    nt = t // tm
    return pl.pallas_call(
        _out_router_kernel,
        grid=(nt,),
        in_specs=[
            pl.BlockSpec((tm, D_MODEL), lambda i: (i, 0)),
            pl.BlockSpec((tm, half), lambda i: (i, 0)),
            pl.BlockSpec((tm, half), lambda i: (i, 0)),
            pl.BlockSpec((2 * half, D_MODEL), lambda i: (0, 0)),
            pl.BlockSpec((1, D_MODEL), lambda i: (0, 0)),
            pl.BlockSpec((N_EXPERTS, D_MODEL), lambda i: (0, 0)),
            pl.BlockSpec((N_EXPERTS, 1), lambda i: (0, 0)),
        ],
        out_specs=[
            pl.BlockSpec((tm, D_MODEL), lambda i: (i, 0)),
            pl.BlockSpec((tm, D_MODEL), lambda i: (i, 0)),
            pl.BlockSpec((TOP_K, tm), lambda i: (0, i)),
            pl.BlockSpec((TOP_K, tm), lambda i: (0, i)),
            pl.BlockSpec((N_EXPERTS, LANE), lambda i: (i, 0)),
        ],
        out_shape=[
            jax.ShapeDtypeStruct((t, D_MODEL), F32),
            jax.ShapeDtypeStruct((t, D_MODEL), BF16),
            jax.ShapeDtypeStruct((TOP_K, t), I32),
            jax.ShapeDtypeStruct((TOP_K, t), F32),
            jax.ShapeDtypeStruct((nt * N_EXPERTS, LANE), I32),
        ],
        compiler_params=_cparams(("parallel",)),
        name="out_router",
    )(x2, out_a, out_b, wo_bf, nw, rwt, rb)


def _routing_tables(nch, n_row_tiles):
    cpt = MOE_TM // ROW_CHUNK
    seg_src = jnp.cumsum(nch, axis=1) - nch
    rows_e = jnp.sum(nch, axis=0)
    tiles_e = (rows_e + cpt - 1) // cpt
    cum_tiles = jnp.cumsum(tiles_e)
    base_e = (cum_tiles - tiles_e) * cpt
    seg_dst = base_e[None, :] + jnp.cumsum(nch, axis=0) - nch
    gap_start = base_e + rows_e
    gap_len = tiles_e * cpt - rows_e
    n_used = cum_tiles[-1]
    tile_ids = jnp.arange(n_row_tiles, dtype=I32)
    tile_e = jnp.minimum(jnp.searchsorted(cum_tiles, jnp.minimum(tile_ids, n_used - 1), side="right"),
                         N_EXPERTS - 1).astype(I32)
    first = jnp.concatenate([jnp.ones((1,), I32), (tile_e[1:] != tile_e[:-1]).astype(I32)])
    flat = lambda a: a.reshape(-1).astype(I32)
    return (flat(seg_src), flat(seg_dst), flat(nch), flat(gap_start), flat(gap_len),
            tile_e, first, n_used.reshape(1).astype(I32))


def _chunk_rows(ref, chunk_idx):
    return ref.at[pl.ds(pl.multiple_of(chunk_idx * ROW_CHUNK, ROW_CHUNK), ROW_CHUNK), :]


def _compact_kernel(src_ref, dst_ref, nch_ref, gap_start_ref, gap_len_ref,
                    x_ref, slot_ref, xs_ref, buf_ref, sem):
    tt = pl.program_id(0)
    n_tt = pl.num_programs(0)
    x = x_ref[...]
    s0, s1, s2, s3 = (slot_ref[kk:kk + 1, :] for kk in range(TOP_K))
    for blk in range(SLOTS // SLOT_BLK):
        sid = lax.broadcasted_iota(I32, (SLOT_BLK, TOK_TILE), 0) + blk * SLOT_BLK
        hit = jnp.where(sid == s0, 1.0, jnp.where(sid == s1, 1.0, jnp.where(sid == s2, 1.0,
                        jnp.where(sid == s3, 1.0, 0.0))))
        buf_ref[blk * SLOT_BLK:(blk + 1) * SLOT_BLK, :] = jnp.dot(
            hit.astype(BF16), x, preferred_element_type=F32).astype(BF16)

    def copy(src_chunk, dst_chunk):
        return pltpu.make_async_copy(_chunk_rows(buf_ref, src_chunk), _chunk_rows(xs_ref, dst_chunk), sem)

    def per_expert(e, total):
        n = nch_ref[tt * N_EXPERTS + e]
        src0 = src_ref[tt * N_EXPERTS + e]
        dst0 = dst_ref[tt * N_EXPERTS + e]

        def issue(kk, _):
            copy(src0 + kk, dst0 + kk).start()
            return 0

        lax.fori_loop(0, n, issue, 0)
        return total + n

    total = lax.fori_loop(0, N_EXPERTS, per_expert, 0)

    zero_chunk = SLOTS // ROW_CHUNK - 1

    def per_gap(e, total):
        n = jnp.where(tt == n_tt - 1, gap_len_ref[e], 0)
        g0 = gap_start_ref[e]

        def issue(kk, _):
            copy(zero_chunk, g0 + kk).start()
            return 0

        lax.fori_loop(0, n, issue, 0)
        return total + n

    total = lax.fori_loop(0, N_EXPERTS, per_gap, total)

    def drain(_, c):
        copy(0, 0).wait()
        return c

    lax.fori_loop(0, total, drain, 0)


def _compact(tables, hn2, slots, n_rows):
    seg_src, seg_dst, nch, gap_start, gap_len = tables
    t = hn2.shape[0]
    return pl.pallas_call(
        _compact_kernel,
        grid_spec=pltpu.PrefetchScalarGridSpec(
            num_scalar_prefetch=5,
            grid=(t // TOK_TILE,),
            in_specs=[
                pl.BlockSpec((TOK_TILE, D_MODEL), lambda i, *_: (i, 0)),
                pl.BlockSpec((TOP_K, TOK_TILE), lambda i, *_: (0, i)),
            ],
            out_specs=pl.BlockSpec(memory_space=pl.ANY),
            scratch_shapes=[pltpu.VMEM((SLOTS, D_MODEL), BF16), pltpu.SemaphoreType.DMA(())],
        ),
        out_shape=jax.ShapeDtypeStruct((n_rows, D_MODEL), BF16),
        compiler_params=_cparams(("arbitrary",)),
        name="compact",
    )(seg_src, seg_dst, nch, gap_start, gap_len, hn2, slots)


def _moe_up_kernel(te_ref, first_ref, nu_ref, x_ref, wg_ref, wu_ref, bg_ref, bu_ref, h_ref, wg_bf, wu_bf):
    i = pl.program_id(1)

    @pl.when(i < nu_ref[0])
    def _():
        @pl.when(first_ref[i] == 1)
        def _():
            wg_bf[...] = wg_ref[...].astype(BF16)
            wu_bf[...] = wu_ref[...].astype(BF16)

        x = x_ref[...]
        gate = jnp.minimum(jnp.dot(x, wg_bf[...], preferred_element_type=F32) + bg_ref[...], SWIGLU_LIMIT)
        up = jnp.clip(jnp.dot(x, wu_bf[...], preferred_element_type=F32) + bu_ref[...],
                      -SWIGLU_LIMIT, SWIGLU_LIMIT)
        act = (up + 1.0) * gate * (1.0 / (1.0 + jnp.exp(-SWIGLU_ALPHA * gate)))
        h_ref[...] = act.astype(h_ref.dtype)


def _moe_up(te, first, n_used, xs, wg, wu, bg, bu):
    n_rows = xs.shape[0]
    n_tiles = n_rows // MOE_TM
    row = lambda c, i, te, fi, nu: (jnp.minimum(i, nu[0] - 1), 0)
    wsel = lambda c, i, te, fi, nu: (te[i], 0, c)
    return pl.pallas_call(
        _moe_up_kernel,
        grid_spec=pltpu.PrefetchScalarGridSpec(
            num_scalar_prefetch=3,
            grid=(D_FF // MOE_FC, n_tiles),
            in_specs=[
                pl.BlockSpec((MOE_TM, D_MODEL), row),
                pl.BlockSpec((None, D_MODEL, MOE_FC), wsel),
                pl.BlockSpec((None, D_MODEL, MOE_FC), wsel),
                pl.BlockSpec((None, 1, MOE_FC), wsel),
                pl.BlockSpec((None, 1, MOE_FC), wsel),
            ],
            out_specs=pl.BlockSpec((MOE_TM, MOE_FC), lambda c, i, te, fi, nu: (jnp.minimum(i, nu[0] - 1), c)),
            scratch_shapes=[pltpu.VMEM((D_MODEL, MOE_FC), BF16), pltpu.VMEM((D_MODEL, MOE_FC), BF16)],
        ),
        out_shape=jax.ShapeDtypeStruct((n_rows, D_FF), BF16),
        compiler_params=_cparams(("arbitrary", "arbitrary")),
        name="moe_up",
    )(te, first, n_used, xs, wg, wu, bg, bu)


def _moe_down_kernel(te_ref, first_ref, nu_ref, h_ref, wd_ref, bd_ref, y_ref, wd_bf):
    i = pl.program_id(0)

    @pl.when(i < nu_ref[0])
    def _():
        @pl.when(first_ref[i] == 1)
        def _():
            wd_bf[...] = wd_ref[...].astype(BF16)

        y = jnp.dot(h_ref[...], wd_bf[...], preferred_element_type=F32) + bd_ref[...]
        y_ref[...] = y.astype(y_ref.dtype)


def _moe_down(te, first, n_used, hs, wd, bd):
    n_rows = hs.shape[0]
    n_tiles = n_rows // MOE_TM
    row = lambda i, te, fi, nu: (jnp.minimum(i, nu[0] - 1), 0)
    wsel = lambda i, te, fi, nu: (te[i], 0, 0)
    return pl.pallas_call(
        _moe_down_kernel,
        grid_spec=pltpu.PrefetchScalarGridSpec(
            num_scalar_prefetch=3,
            grid=(n_tiles,),
            in_specs=[
                pl.BlockSpec((MOE_TM, D_FF), row),
                pl.BlockSpec((None, D_FF, D_MODEL), wsel),
                pl.BlockSpec((None, 1, D_MODEL), wsel),
            ],
            out_specs=pl.BlockSpec((MOE_TM, D_MODEL), row),
            scratch_shapes=[pltpu.VMEM((D_FF, D_MODEL), BF16)],
        ),
        out_shape=jax.ShapeDtypeStruct((n_rows, D_MODEL), BF16),
        compiler_params=_cparams(("arbitrary",)),
        name="moe_down",
    )(te, first, n_used, hs, wd, bd)


def _combine_kernel(src_ref, dst_ref, nch_ref, h_ref, slot_ref, w_ref, ys_ref, o_ref, buf_ref, sem):
    tt = pl.program_id(0)

    @pl.when(tt == 0)
    def _():
        buf_ref[...] = jnp.zeros_like(buf_ref)

    def copy(sorted_chunk, local_chunk):
        return pltpu.make_async_copy(_chunk_rows(ys_ref, sorted_chunk), _chunk_rows(buf_ref, local_chunk), sem)

    def per_expert(e, total):
        n = nch_ref[tt * N_EXPERTS + e]
        loc0 = src_ref[tt * N_EXPERTS + e]
        srt0 = dst_ref[tt * N_EXPERTS + e]

        def issue(kk, _):
            copy(srt0 + kk, loc0 + kk).start()
            return 0

        lax.fori_loop(0, n, issue, 0)
        return total + n

    total = lax.fori_loop(0, N_EXPERTS, per_expert, 0)

    def drain(_, c):
        copy(0, 0).wait()
        return c

    lax.fori_loop(0, total, drain, 0)

    o_ref[...] = h_ref[...]
    for blk in range(SLOTS // SLOT_BLK):
        sid = lax.broadcasted_iota(I32, (TOK_TILE, SLOT_BLK), 1) + blk * SLOT_BLK
        wmat = jnp.zeros((TOK_TILE, SLOT_BLK), F32)
        for kk in range(TOP_K):
            wmat = wmat + jnp.where(sid == slot_ref[:, kk:kk + 1], w_ref[:, kk:kk + 1], 0.0)
        w_hi = wmat.astype(BF16)
        w_lo = (wmat - w_hi.astype(F32)).astype(BF16)
        yb = buf_ref[blk * SLOT_BLK:(blk + 1) * SLOT_BLK, :]
        o_ref[...] += (jnp.dot(w_hi, yb, preferred_element_type=F32)
                       + jnp.dot(w_lo, yb, preferred_element_type=F32))


def _combine(tables, h, slots_t, w_t, ys):
    seg_src, seg_dst, nch = tables
    t = h.shape[0]
    return pl.pallas_call(
        _combine_kernel,
        grid_spec=pltpu.PrefetchScalarGridSpec(
            num_scalar_prefetch=3,
            grid=(t // TOK_TILE,),
            in_specs=[
                pl.BlockSpec((TOK_TILE, D_MODEL), lambda i, *_: (i, 0)),
                pl.BlockSpec((TOK_TILE, TOP_K), lambda i, *_: (i, 0)),
                pl.BlockSpec((TOK_TILE, TOP_K), lambda i, *_: (i, 0)),
                pl.BlockSpec(memory_space=pl.ANY),
            ],
            out_specs=pl.BlockSpec((TOK_TILE, D_MODEL), lambda i, *_: (i, 0)),
            scratch_shapes=[pltpu.VMEM((SLOTS, D_MODEL), BF16), pltpu.SemaphoreType.DMA(())],
        ),
        out_shape=jax.ShapeDtypeStruct((t, D_MODEL), F32),
        compiler_params=_cparams(("arbitrary",)),
        name="combine",
    )(seg_src, seg_dst, nch, h, slots_t, w_t, ys)


def kernel(x, attn_norm_w, w_in, da_q_norm_w, da_k_norm_w, da_lambda_q1, da_lambda_k1, da_lambda_q2,
           da_lambda_k2, da_subln_w, gla_gate_up_w, gla_gate_up_b, gla_norm_w, w_out, ffn_norm_w,
           router_w, router_b, exp_w_gate, exp_b_gate, exp_w_up, exp_b_up, exp_w_down, exp_b_down):
    b, s, d = x.shape
    t = b * s
    assert d == D_MODEL and t % TOK_TILE == 0 and s % CHUNK == 0 and attn_norm_w.shape[0] == 1
    x2 = x.reshape(t, d)

    w_in_bf = jnp.pad(w_in[0].astype(BF16), ((0, 0), (0, IN_COLS_PAD - IN_COLS)))
    proj = _in_proj(x2, attn_norm_w, w_in_bf)
    proj3 = proj.reshape(b, s, IN_COLS_PAD)

    out_a = _diff_attn(proj3, jnp.tile(da_q_norm_w, (1, 2)), jnp.tile(da_k_norm_w, (1, 2)),
                       da_lambda_q1, da_lambda_k1, da_lambda_q2, da_lambda_k2, da_subln_w)
    wup_pad = jnp.pad(gla_gate_up_w[0].astype(BF16), ((0, LANE - GLA_RANK), (0, 0)))
    out_b = _gla(proj3, wup_pad, gla_gate_up_b, gla_norm_w)

    h, hn2, slots, top_w, nch_b = _out_router(
        x2, out_a.reshape(t, -1), out_b.reshape(t, -1), w_out[0].astype(BF16), ffn_norm_w,
        router_w[0].T, router_b.reshape(N_EXPERTS, 1))

    n_tok_tiles = t // TOK_TILE
    nch = nch_b[:, 0].reshape(n_tok_tiles, N_EXPERTS)
    n_row_tiles = (TOP_K * t + n_tok_tiles * N_EXPERTS * (ROW_CHUNK - 1)) // MOE_TM + N_EXPERTS
    seg_src, seg_dst, nch_f, gap_start, gap_len, tile_e, first, n_used = _routing_tables(nch, n_row_tiles)

    xs = _compact((seg_src, seg_dst, nch_f, gap_start, gap_len), hn2, slots, n_row_tiles * MOE_TM)
    hs = _moe_up(tile_e, first, n_used, xs,
                 exp_w_gate.reshape(N_EXPERTS, D_MODEL, D_FF), exp_w_up.reshape(N_EXPERTS, D_MODEL, D_FF),
                 exp_b_gate.reshape(N_EXPERTS, 1, D_FF), exp_b_up.reshape(N_EXPERTS, 1, D_FF))
    ys = _moe_down(tile_e, first, n_used, hs, exp_w_down.reshape(N_EXPERTS, D_FF, D_MODEL),
                   exp_b_down.reshape(N_EXPERTS, 1, D_MODEL))
    out = _combine((seg_src, seg_dst, nch_f), h, slots.T, top_w.T, ys)
    return out.reshape(b, s, d)
```

```python
import functools
import math

import jax
import jax.numpy as jnp
from jax import lax
from jax.experimental import pallas as pl
from jax.experimental.pallas import tpu as pltpu

F32 = jnp.float32
BF16 = jnp.bfloat16
I32 = jnp.int32

D_MODEL = 2048
CHUNK = 64
EPS = 1e-6
DA_HEADS = 8
DA_QK = 64
DA_V = 128
GLA_HEADS = 4
GLA_DK = 128
GLA_DV = 256
GLA_RANK = 16
GLA_TAU = 16.0
N_EXPERTS = 32
TOP_K = 4
D_FF = 2048
SWIGLU_LIMIT = 7.0
SWIGLU_ALPHA = 1.702
LAMBDA_INIT = 0.8 - 0.6 * math.exp(-0.3 * 0)

IN_COLS = 6160
IN_COLS_PAD = 6272
LANE = 128
ROW_CHUNK = 16

IN_TM = 1024
IN_TN = 896
ATT_BQ = 256
ATT_HP = 4
TOK_TILE = 512
SLOTS = TOP_K * TOK_TILE + N_EXPERTS * ROW_CHUNK
SLOT_BLK = 256
MOE_TM = 256
MOE_FC = 1024
VMEM_LIMIT = 56 * 1024 * 1024


def _cparams(sem):
    return pltpu.CompilerParams(dimension_semantics=sem, vmem_limit_bytes=VMEM_LIMIT)


def _in_proj_kernel(x_ref, nw_ref, w_ref, o_ref, hn_ref):
    @pl.when(pl.program_id(1) == 0)
    def _():
        x = x_ref[...]
        ms = jnp.mean(x * x, axis=-1, keepdims=True)
        hn_ref[...] = (x * lax.rsqrt(ms + EPS) * nw_ref[...]).astype(BF16)

    o_ref[...] = jnp.dot(hn_ref[...], w_ref[...], preferred_element_type=F32).astype(o_ref.dtype)


def _in_proj(x2, norm_w, w_bf):
    t = x2.shape[0]
    tm = min(IN_TM, t)
    return pl.pallas_call(
        _in_proj_kernel,
        grid=(t // tm, IN_COLS_PAD // IN_TN),
        in_specs=[
            pl.BlockSpec((tm, D_MODEL), lambda i, j: (i, 0)),
            pl.BlockSpec((1, D_MODEL), lambda i, j: (0, 0)),
            pl.BlockSpec((D_MODEL, IN_TN), lambda i, j: (0, j)),
        ],
        out_specs=pl.BlockSpec((tm, IN_TN), lambda i, j: (i, j)),
        out_shape=jax.ShapeDtypeStruct((t, IN_COLS_PAD), BF16),
        scratch_shapes=[pltpu.VMEM((tm, D_MODEL), BF16)],
        compiler_params=_cparams(("parallel", "arbitrary")),
        name="in_proj",
    )(x2, norm_w, w_bf)


def _attn_kernel(q_ref, k_ref, v_ref, qw_ref, kw_ref, lq1_ref, lk1_ref, lq2_ref, lk2_ref, sw_ref,
                 o_ref, kn_ref, vt_ref, *, bq):
    qi = pl.program_id(2)
    first_map = lax.broadcasted_iota(I32, (1, LANE), 1) < DA_QK
    n_grp = 2 * bq // LANE
    shift = CHUNK.bit_length() - 1

    def qk_norm(x, w):
        x2 = x * x
        scale = []
        for hh in range(ATT_HP):
            xh = x2[:, hh * LANE:(hh + 1) * LANE]
            s1 = jnp.sum(jnp.where(first_map, xh, 0.0), axis=-1, keepdims=True)
            s2 = jnp.sum(jnp.where(first_map, 0.0, xh), axis=-1, keepdims=True)
            scale.append(jnp.where(first_map, lax.rsqrt(s1 / DA_QK + EPS), lax.rsqrt(s2 / DA_QK + EPS)))
        return x * jnp.concatenate(scale, axis=1) * w

    @pl.when(qi == 0)
    def _():
        kn_ref[...] = qk_norm(k_ref[...].astype(F32), kw_ref[...]).astype(BF16)
        vt_ref[...] = v_ref[...].astype(F32).T.astype(BF16)

    q = qk_norm(q_ref[...].astype(F32), qw_ref[...]) * (DA_QK ** -0.5 * math.log2(math.e))
    q_t = []
    for hh in range(ATT_HP):
        qh = q[:, hh * LANE:(hh + 1) * LANE]
        q_t.append(jnp.concatenate([jnp.where(first_map, qh, 0.0), jnp.where(first_map, 0.0, qh)],
                                   axis=0).T.astype(BF16))

    def step(j, carry, masked):
        r0 = pl.multiple_of(j * bq, bq)
        scores = [jnp.dot(kn_ref[pl.ds(r0, bq), hh * LANE:(hh + 1) * LANE], q_t[hh], preferred_element_type=F32)
                  for hh in range(ATT_HP)]
        out = []
        for hh in range(ATT_HP):
            ms, ls, alphas, ps = [], [], [], []
            for g in range(n_grp):
                m, l, _ = carry[hh][g]
                s = scores[hh][:, g * LANE:(g + 1) * LANE]
                if masked:
                    key_chunk = lax.broadcasted_iota(I32, (bq, LANE), 0) >> shift
                    qry_chunk = (lax.broadcasted_iota(I32, (bq, LANE), 1) + (g * LANE) % bq) >> shift
                    s = jnp.where(key_chunk <= qry_chunk, s, -jnp.inf)
                m_new = jnp.maximum(m, jnp.max(s, axis=0, keepdims=True))
                alpha = jnp.exp2(m - m_new)
                p = jnp.exp2(s - m_new)
                ms.append(m_new)
                ls.append(alpha * l + jnp.sum(p, axis=0, keepdims=True))
                alphas.append(alpha)
                ps.append(p.astype(BF16))
            pv = jnp.dot(vt_ref[hh * DA_V:(hh + 1) * DA_V, pl.ds(r0, bq)], jnp.concatenate(ps, axis=1),
                         preferred_element_type=F32)
            out.append(tuple((ms[g], ls[g], alphas[g] * carry[hh][g][2] + pv[:, g * LANE:(g + 1) * LANE])
                             for g in range(n_grp)))
        return tuple(out)

    init = tuple(tuple((jnp.full((1, LANE), -jnp.inf, F32), jnp.zeros((1, LANE), F32), jnp.zeros((DA_V, LANE), F32))
                       for _ in range(n_grp)) for _ in range(ATT_HP))
    carry = lax.fori_loop(0, qi, lambda j, c: step(j, c, False), init)
    carry = step(qi, carry, True)
    lam = (jnp.exp(jnp.sum(lq1_ref[...] * lk1_ref[...], axis=-1, keepdims=True))
           - jnp.exp(jnp.sum(lq2_ref[...] * lk2_ref[...], axis=-1, keepdims=True)) + LAMBDA_INIT)
    half = n_grp // 2
    for hh in range(ATT_HP):
        for g in range(half):
            o1 = carry[hh][g][2] / carry[hh][g][1]
            o2 = carry[hh][g + half][2] / carry[hh][g + half][1]
            out = o1 - lam * o2
            ms = jnp.mean(out * out, axis=0, keepdims=True)
            y = out * lax.rsqrt(ms + EPS) * sw_ref[...] * (1.0 - LAMBDA_INIT)
            o_ref[g * LANE:(g + 1) * LANE, hh * DA_V:(hh + 1) * DA_V] = y.T.astype(o_ref.dtype)


def _diff_attn(proj3, qw, kw, lq1, lk1, lq2, lk2, sw):
    b, s, _ = proj3.shape
    bq = min(ATT_BQ, s)
    hw = ATT_HP * LANE
    n_hg = DA_HEADS // ATT_HP
    vec = lambda n: pl.BlockSpec((1, n), lambda bi, h, qi: (0, 0))
    return pl.pallas_call(
        functools.partial(_attn_kernel, bq=bq),
        grid=(b, n_hg, s // bq),
        in_specs=[
            pl.BlockSpec((None, bq, hw), lambda bi, h, qi: (bi, qi, h)),
            pl.BlockSpec((None, s, hw), lambda bi, h, qi: (bi, 0, n_hg + h)),
            pl.BlockSpec((None, s, hw), lambda bi, h, qi: (bi, 0, 2 * n_hg + h)),
            vec(hw), vec(hw), vec(DA_QK), vec(DA_QK), vec(DA_QK), vec(DA_QK),
            pl.BlockSpec((DA_V, 1), lambda bi, h, qi: (0, 0)),
        ],
        out_specs=pl.BlockSpec((None, bq, hw), lambda bi, h, qi: (bi, qi, h)),
        out_shape=jax.ShapeDtypeStruct((b, s, DA_HEADS * DA_V), BF16),
        scratch_shapes=[pltpu.VMEM((s, hw), BF16), pltpu.VMEM((hw, s), BF16)],
        compiler_params=_cparams(("parallel", "parallel", "arbitrary")),
        name="diff_attn",
    )(proj3, proj3, proj3, qw, kw, lq1, lk1, lq2, lk2, sw.reshape(DA_V, 1))


GLA_SUB = 16


def _gla_kernel(q_ref, k_ref, v_ref, r_ref, gd_ref, wup_ref, bup_ref, nw_ref, o_ref, st_ref, *, n_chunks):
    st_ref[...] = jnp.zeros_like(st_ref)
    rr = lax.broadcasted_iota(I32, (CHUNK, CHUNK), 0)
    cc = lax.broadcasted_iota(I32, (CHUNK, CHUNK), 1)
    tril = jnp.where(cc <= rr, 1.0, 0.0).astype(BF16)
    sub_row = lax.broadcasted_iota(I32, (GLA_SUB, GLA_DK), 0)
    sub_lane = lax.broadcasted_iota(I32, (GLA_SUB, CHUNK), 1)
    chunk_row = lax.broadcasted_iota(I32, (CHUNK, GLA_DK), 0)
    nt = (((1,), (1,)), ((), ()))
    tn = (((0,), (0,)), ((), ()))

    def chunk(c, _):
        r0 = pl.multiple_of(c * CHUNK, CHUNK)
        z_all = jnp.dot(gd_ref[pl.ds(r0, CHUNK), :], wup_ref[...], preferred_element_type=F32) + bup_ref[...]
        for hh in range(GLA_HEADS):
            head_chunk(r0, hh, z_all[:, hh * GLA_DK:(hh + 1) * GLA_DK])
        return 0

    def head_chunk(r0, hh, z):
        kcol = slice(hh * GLA_DK, (hh + 1) * GLA_DK)
        vcol = slice(hh * GLA_DV, (hh + 1) * GLA_DV)
        q = q_ref[pl.ds(r0, CHUNK), kcol].astype(F32) * (GLA_DK ** -0.5)
        k = k_ref[pl.ds(r0, CHUNK), kcol].astype(F32)
        v = v_ref[pl.ds(r0, CHUNK), vcol]
        lg = (jnp.minimum(z, 0.0) - jnp.log(1.0 + jnp.exp(-jnp.abs(z)))) / GLA_TAU
        lg_hi = lg.astype(BF16)
        lg_lo = (lg - lg_hi.astype(F32)).astype(BF16)
        bcum = (jnp.dot(tril, lg_hi, preferred_element_type=F32)
                + jnp.dot(tril, lg_lo, preferred_element_type=F32))
        bprev = bcum - lg
        st = st_ref[hh]
        o_inter = lax.dot_general((q * jnp.exp(bcum)).astype(BF16), st.astype(BF16), nt,
                                  preferred_element_type=F32)
        for blk in range(CHUNK // GLA_SUB):
            lo = blk * GLA_SUB
            hi = lo + GLA_SUB
            b_blk = bcum[lo:hi]
            q_blk = q[lo:hi]
            if blk > 0:
                ref = bprev[lo:lo + 1]
                qt = (q_blk * jnp.exp(b_blk - ref)).astype(BF16)
                kt = (k * jnp.exp(jnp.where(chunk_row < lo, ref - bcum, -jnp.inf))).astype(BF16)
                a_blk = lax.dot_general(qt, kt, nt, preferred_element_type=F32)
            else:
                a_blk = jnp.zeros((GLA_SUB, CHUNK), F32)
            for jj in range(GLA_SUB):
                dlt = jnp.where(sub_row >= jj, b_blk - bcum[lo + jj:lo + jj + 1], -jnp.inf)
                col = jnp.sum(q_blk * k[lo + jj:lo + jj + 1] * jnp.exp(dlt), axis=-1, keepdims=True)
                a_blk = a_blk + jnp.where(sub_lane == lo + jj, col, 0.0)
            o_blk = o_inter[lo:hi] + jnp.dot(a_blk.astype(BF16), v, preferred_element_type=F32)
            ms = jnp.mean(o_blk * o_blk, axis=-1, keepdims=True)
            rg = r_ref[pl.ds(r0 + lo, GLA_SUB), vcol].astype(F32)
            y = o_blk * lax.rsqrt(ms + EPS) * nw_ref[...] * (rg / (1.0 + jnp.exp(-rg)))
            o_ref[pl.ds(r0 + lo, GLA_SUB), vcol] = y.astype(o_ref.dtype)
        b_last = bcum[CHUNK - 1:CHUNK]
        k_dec = (k * jnp.exp(b_last - bcum)).astype(BF16)
        st_ref[hh] = st * jnp.exp(b_last) + lax.dot_general(v, k_dec, tn, preferred_element_type=F32)

    lax.fori_loop(0, n_chunks, chunk, 0)


def _gla(proj3, wup_pad, bup, nw):
    b, s, _ = proj3.shape
    kw = GLA_HEADS * GLA_DK
    vw = GLA_HEADS * GLA_DV
    q0 = 3 * DA_HEADS * DA_V // kw
    k0 = q0 + 1
    v0 = (k0 + 1) * kw // vw
    r0 = v0 + 1
    gd = (IN_COLS - GLA_RANK) // LANE
    return pl.pallas_call(
        functools.partial(_gla_kernel, n_chunks=s // CHUNK),
        grid=(b,),
        in_specs=[
            pl.BlockSpec((None, s, kw), lambda bi: (bi, 0, q0)),
            pl.BlockSpec((None, s, kw), lambda bi: (bi, 0, k0)),
            pl.BlockSpec((None, s, vw), lambda bi: (bi, 0, v0)),
            pl.BlockSpec((None, s, vw), lambda bi: (bi, 0, r0)),
            pl.BlockSpec((None, s, LANE), lambda bi: (bi, 0, gd)),
            pl.BlockSpec((LANE, kw), lambda bi: (0, 0)),
            pl.BlockSpec((1, kw), lambda bi: (0, 0)),
            pl.BlockSpec((1, GLA_DV), lambda bi: (0, 0)),
        ],
        out_specs=pl.BlockSpec((None, s, vw), lambda bi: (bi, 0, 0)),
        out_shape=jax.ShapeDtypeStruct((b, s, vw), BF16),
        scratch_shapes=[pltpu.VMEM((GLA_HEADS, GLA_DV, GLA_DK), F32)],
        compiler_params=_cparams(("parallel",)),
        name="gla",
    )(proj3, proj3, proj3, proj3, proj3, wup_pad, bup, nw)


def _out_router_kernel(x_ref, a_ref, b_ref, wo_ref, nw_ref, rwt_ref, rb_ref,
                       h_ref, hn_ref, slot_ref, w_ref, nch_ref):
    tm = x_ref.shape[0]
    half = a_ref.shape[1]
    h = (x_ref[...]
         + jnp.dot(a_ref[...], wo_ref[0:half, :], preferred_element_type=F32)
         + jnp.dot(b_ref[...], wo_ref[half:2 * half, :], preferred_element_type=F32))
    h_ref[...] = h
    ms = jnp.mean(h * h, axis=-1, keepdims=True)
    hn = h * lax.rsqrt(ms + EPS) * nw_ref[...]
    hn_hi = hn.astype(BF16)
    hn_ref[...] = hn_hi
    hn_lo = (hn - hn_hi.astype(F32)).astype(BF16)
    rwt = rwt_ref[...]
    rw_hi = rwt.astype(BF16)
    rw_lo = (rwt - rw_hi.astype(F32)).astype(BF16)
    nt = (((1,), (1,)), ((), ()))
    logits = (lax.dot_general(rw_hi, hn_hi, nt, preferred_element_type=F32)
              + lax.dot_general(rw_hi, hn_lo, nt, preferred_element_type=F32)
              + lax.dot_general(rw_lo, hn_hi, nt, preferred_element_type=F32)
              + rb_ref[...])
    e_iota = lax.broadcasted_iota(I32, (N_EXPERTS, tm), 0).astype(F32)
    vals, hots = [], []
    for _ in range(TOP_K):
        m = jnp.max(logits, axis=0, keepdims=True)
        idx = jnp.min(jnp.where(logits == m, e_iota, float(N_EXPERTS)), axis=0, keepdims=True)
        hot = e_iota == idx
        vals.append(m)
        hots.append(hot)
        logits = jnp.where(hot, -jnp.inf, logits)
    exps = [jnp.exp(v - vals[0]) for v in vals]
    denom = exps[0] + exps[1] + exps[2] + exps[3]
    for kk in range(TOP_K):
        w_ref[kk:kk + 1, :] = exps[kk] / denom
    sel = jnp.zeros((N_EXPERTS, tm), F32)
    for hot in hots:
        sel = sel + jnp.where(hot, 1.0, 0.0)
    tr = lax.broadcasted_iota(I32, (tm, tm), 0)
    tc = lax.broadcasted_iota(I32, (tm, tm), 1)
    upper = jnp.where(tr < tc, 1.0, 0.0).astype(BF16)
    rank = jnp.dot(sel.astype(BF16), upper, preferred_element_type=F32)
    cnt = jnp.sum(sel, axis=1, keepdims=True)
    nch = jnp.floor((cnt + (ROW_CHUNK - 1)) * (1.0 / ROW_CHUNK))
    nch_b = jnp.broadcast_to(nch, (N_EXPERTS, LANE))
    er = lax.broadcasted_iota(I32, (N_EXPERTS, N_EXPERTS), 0)
    ec = lax.broadcasted_iota(I32, (N_EXPERTS, N_EXPERTS), 1)
    lower = jnp.where(ec < er, 1.0, 0.0).astype(BF16)
    seg = jnp.dot(lower, nch_b.astype(BF16), preferred_element_type=F32)
    pos = seg[:, 0:1] * ROW_CHUNK + rank
    for kk in range(TOP_K):
        slot_ref[kk:kk + 1, :] = jnp.sum(jnp.where(hots[kk], pos, 0.0), axis=0, keepdims=True).astype(I32)
    nch_ref[...] = nch_b.astype(I32)


def _out_router(x2, out_a, out_b, wo_bf, nw, rwt, rb):
    t = x2.shape[0]
    tm = TOK_TILE
    half = out_a.shape[1]
    nt = t // tm
    return pl.pallas_call(
        _out_router_kernel,
        grid=(nt,),
        in_specs=[
            pl.BlockSpec((tm, D_MODEL), lambda i: (i, 0)),
            pl.BlockSpec((tm, half), lambda i: (i, 0)),
            pl.BlockSpec((tm, half), lambda i: (i, 0)),
            pl.BlockSpec((2 * half, D_MODEL), lambda i: (0, 0)),
            pl.BlockSpec((1, D_MODEL), lambda i: (0, 0)),
            pl.BlockSpec((N_EXPERTS, D_MODEL), lambda i: (0, 0)),
            pl.BlockSpec((N_EXPERTS, 1), lambda i: (0, 0)),
        ],
        out_specs=[
            pl.BlockSpec((tm, D_MODEL), lambda i: (i, 0)),
            pl.BlockSpec((tm, D_MODEL), lambda i: (i, 0)),
            pl.BlockSpec((TOP_K, tm), lambda i: (0, i)),
            pl.BlockSpec((TOP_K, tm), lambda i: (0, i)),
            pl.BlockSpec((N_EXPERTS, LANE), lambda i: (i, 0)),
        ],
        out_shape=[
            jax.ShapeDtypeStruct((t, D_MODEL), F32),
            jax.ShapeDtypeStruct((t, D_MODEL), BF16),
            jax.ShapeDtypeStruct((TOP_K, t), I32),
            jax.ShapeDtypeStruct((TOP_K, t), F32),
            jax.ShapeDtypeStruct((nt * N_EXPERTS, LANE), I32),
        ],
        compiler_params=_cparams(("parallel",)),
        name="out_router",
    )(x2, out_a, out_b, wo_bf, nw, rwt, rb)


def _routing_tables(nch, n_row_tiles):
    cpt = MOE_TM // ROW_CHUNK
    seg_src = jnp.cumsum(nch, axis=1) - nch
    rows_e = jnp.sum(nch, axis=0)
    tiles_e = (rows_e + cpt - 1) // cpt
    cum_tiles = jnp.cumsum(tiles_e)
    base_e = (cum_tiles - tiles_e) * cpt
    seg_dst = base_e[None, :] + jnp.cumsum(nch, axis=0) - nch
    gap_start = base_e + rows_e
    gap_len = tiles_e * cpt - rows_e
    n_used = cum_tiles[-1]
    tile_ids = jnp.arange(n_row_tiles, dtype=I32)
    tile_e = jnp.sum((cum_tiles[None, :] <= jnp.minimum(tile_ids, n_used - 1)[:, None]).astype(I32), axis=1)
    tile_e = jnp.minimum(tile_e, N_EXPERTS - 1)
    first = jnp.concatenate([jnp.ones((1,), I32), (tile_e[1:] != tile_e[:-1]).astype(I32)])
    flat = lambda a: a.reshape(-1).astype(I32)
    return (flat(seg_src), flat(seg_dst), flat(nch), flat(gap_start), flat(gap_len),
            tile_e, first, n_used.reshape(1).astype(I32))


def _chunk_rows(ref, chunk_idx):
    return ref.at[pl.ds(pl.multiple_of(chunk_idx * ROW_CHUNK, ROW_CHUNK), ROW_CHUNK), :]


def _compact_kernel(src_ref, dst_ref, nch_ref, gap_start_ref, gap_len_ref,
                    x_ref, slot_ref, xs_ref, buf_ref, sem):
    tt = pl.program_id(0)
    n_tt = pl.num_programs(0)
    x = x_ref[...]
    s0, s1, s2, s3 = (slot_ref[kk:kk + 1, :] for kk in range(TOP_K))
    for blk in range(SLOTS // SLOT_BLK):
        sid = lax.broadcasted_iota(I32, (SLOT_BLK, TOK_TILE), 0) + blk * SLOT_BLK
        hit = jnp.where(sid == s0, 1.0, jnp.where(sid == s1, 1.0, jnp.where(sid == s2, 1.0,
                        jnp.where(sid == s3, 1.0, 0.0))))
        buf_ref[blk * SLOT_BLK:(blk + 1) * SLOT_BLK, :] = jnp.dot(
            hit.astype(BF16), x, preferred_element_type=F32).astype(BF16)

    def copy(src_chunk, dst_chunk):
        return pltpu.make_async_copy(_chunk_rows(buf_ref, src_chunk), _chunk_rows(xs_ref, dst_chunk), sem)

    def per_expert(e, total):
        n = nch_ref[tt * N_EXPERTS + e]
        src0 = src_ref[tt * N_EXPERTS + e]
        dst0 = dst_ref[tt * N_EXPERTS + e]

        def issue(kk, _):
            copy(src0 + kk, dst0 + kk).start()
            return 0

        lax.fori_loop(0, n, issue, 0)
        return total + n

    total = lax.fori_loop(0, N_EXPERTS, per_expert, 0)

    zero_chunk = SLOTS // ROW_CHUNK - 1

    def per_gap(e, total):
        n = jnp.where(tt == n_tt - 1, gap_len_ref[e], 0)
        g0 = gap_start_ref[e]

        def issue(kk, _):
            copy(zero_chunk, g0 + kk).start()
            return 0

        lax.fori_loop(0, n, issue, 0)
        return total + n

    total = lax.fori_loop(0, N_EXPERTS, per_gap, total)

    def drain(_, c):
        copy(0, 0).wait()
        return c

    lax.fori_loop(0, total, drain, 0)


def _compact(tables, hn2, slots, n_rows):
    seg_src, seg_dst, nch, gap_start, gap_len = tables
    t = hn2.shape[0]
    return pl.pallas_call(
        _compact_kernel,
        grid_spec=pltpu.PrefetchScalarGridSpec(
            num_scalar_prefetch=5,
            grid=(t // TOK_TILE,),
            in_specs=[
                pl.BlockSpec((TOK_TILE, D_MODEL), lambda i, *_: (i, 0)),
                pl.BlockSpec((TOP_K, TOK_TILE), lambda i, *_: (0, i)),
            ],
            out_specs=pl.BlockSpec(memory_space=pl.ANY),
            scratch_shapes=[pltpu.VMEM((SLOTS, D_MODEL), BF16), pltpu.SemaphoreType.DMA(())],
        ),
        out_shape=jax.ShapeDtypeStruct((n_rows, D_MODEL), BF16),
        compiler_params=_cparams(("arbitrary",)),
        name="compact",
    )(seg_src, seg_dst, nch, gap_start, gap_len, hn2, slots)


def _moe_up_kernel(te_ref, first_ref, nu_ref, x_ref, wg_ref, wu_ref, bg_ref, bu_ref, h_ref, wg_bf, wu_bf):
    i = pl.program_id(1)

    @pl.when(i < nu_ref[0])
    def _():
        @pl.when(first_ref[i] == 1)
        def _():
            wg_bf[...] = wg_ref[...].astype(BF16)
            wu_bf[...] = wu_ref[...].astype(BF16)

        x = x_ref[...]
        gate = jnp.minimum(jnp.dot(x, wg_bf[...], preferred_element_type=F32) + bg_ref[...], SWIGLU_LIMIT)
        up = jnp.clip(jnp.dot(x, wu_bf[...], preferred_element_type=F32) + bu_ref[...],
                      -SWIGLU_LIMIT, SWIGLU_LIMIT)
        act = (up + 1.0) * gate * (1.0 / (1.0 + jnp.exp(-SWIGLU_ALPHA * gate)))
        h_ref[...] = act.astype(h_ref.dtype)


def _moe_up(te, first, n_used, xs, wg, wu, bg, bu):
    n_rows = xs.shape[0]
    n_tiles = n_rows // MOE_TM
    row = lambda c, i, te, fi, nu: (jnp.minimum(i, nu[0] - 1), 0)
    wsel = lambda c, i, te, fi, nu: (te[i], 0, c)
    return pl.pallas_call(
        _moe_up_kernel,
        grid_spec=pltpu.PrefetchScalarGridSpec(
            num_scalar_prefetch=3,
            grid=(D_FF // MOE_FC, n_tiles),
            in_specs=[
                pl.BlockSpec((MOE_TM, D_MODEL), row),
                pl.BlockSpec((None, D_MODEL, MOE_FC), wsel),
                pl.BlockSpec((None, D_MODEL, MOE_FC), wsel),
                pl.BlockSpec((None, 1, MOE_FC), wsel),
                pl.BlockSpec((None, 1, MOE_FC), wsel),
            ],
            out_specs=pl.BlockSpec((MOE_TM, MOE_FC), lambda c, i, te, fi, nu: (jnp.minimum(i, nu[0] - 1), c)),
            scratch_shapes=[pltpu.VMEM((D_MODEL, MOE_FC), BF16), pltpu.VMEM((D_MODEL, MOE_FC), BF16)],
        ),
        out_shape=jax.ShapeDtypeStruct((n_rows, D_FF), BF16),
        compiler_params=_cparams(("arbitrary", "arbitrary")),
        name="moe_up",
    )(te, first, n_used, xs, wg, wu, bg, bu)


def _moe_down_kernel(te_ref, first_ref, nu_ref, h_ref, wd_ref, bd_ref, y_ref, wd_bf):
    i = pl.program_id(0)

    @pl.when(i < nu_ref[0])
    def _():
        @pl.when(first_ref[i] == 1)
        def _():
            wd_bf[...] = wd_ref[...].astype(BF16)

        y = jnp.dot(h_ref[...], wd_bf[...], preferred_element_type=F32) + bd_ref[...]
        y_ref[...] = y.astype(y_ref.dtype)


def _moe_down(te, first, n_used, hs, wd, bd):
    n_rows = hs.shape[0]
    n_tiles = n_rows // MOE_TM
    row = lambda i, te, fi, nu: (jnp.minimum(i, nu[0] - 1), 0)
    wsel = lambda i, te, fi, nu: (te[i], 0, 0)
    return pl.pallas_call(
        _moe_down_kernel,
        grid_spec=pltpu.PrefetchScalarGridSpec(
            num_scalar_prefetch=3,
            grid=(n_tiles,),
            in_specs=[
                pl.BlockSpec((MOE_TM, D_FF), row),
                pl.BlockSpec((None, D_FF, D_MODEL), wsel),
                pl.BlockSpec((None, 1, D_MODEL), wsel),
            ],
            out_specs=pl.BlockSpec((MOE_TM, D_MODEL), row),
            scratch_shapes=[pltpu.VMEM((D_FF, D_MODEL), BF16)],
        ),
        out_shape=jax.ShapeDtypeStruct((n_rows, D_MODEL), BF16),
        compiler_params=_cparams(("arbitrary",)),
        name="moe_down",
    )(te, first, n_used, hs, wd, bd)


def _combine_kernel(src_ref, dst_ref, nch_ref, h_ref, slot_ref, w_ref, ys_ref, o_ref, buf_ref, sem):
    tt = pl.program_id(0)

    @pl.when(tt == 0)
    def _():
        buf_ref[...] = jnp.zeros_like(buf_ref)

    def copy(sorted_chunk, local_chunk):
        return pltpu.make_async_copy(_chunk_rows(ys_ref, sorted_chunk), _chunk_rows(buf_ref, local_chunk), sem)

    def per_expert(e, total):
        n = nch_ref[tt * N_EXPERTS + e]
        loc0 = src_ref[tt * N_EXPERTS + e]
        srt0 = dst_ref[tt * N_EXPERTS + e]

        def issue(kk, _):
            copy(srt0 + kk, loc0 + kk).start()
            return 0

        lax.fori_loop(0, n, issue, 0)
        return total + n

    total = lax.fori_loop(0, N_EXPERTS, per_expert, 0)

    def drain(_, c):
        copy(0, 0).wait()
        return c

    lax.fori_loop(0, total, drain, 0)

    o_ref[...] = h_ref[...]
    for blk in range(SLOTS // SLOT_BLK):
        sid = lax.broadcasted_iota(I32, (TOK_TILE, SLOT_BLK), 1) + blk * SLOT_BLK
        wmat = jnp.zeros((TOK_TILE, SLOT_BLK), F32)
        for kk in range(TOP_K):
            wmat = wmat + jnp.where(sid == slot_ref[:, kk:kk + 1], w_ref[:, kk:kk + 1], 0.0)
        w_hi = wmat.astype(BF16)
        w_lo = (wmat - w_hi.astype(F32)).astype(BF16)
        yb = buf_ref[blk * SLOT_BLK:(blk + 1) * SLOT_BLK, :]
        o_ref[...] += (jnp.dot(w_hi, yb, preferred_element_type=F32)
                       + jnp.dot(w_lo, yb, preferred_element_type=F32))


def _combine(tables, h, slots_t, w_t, ys):
    seg_src, seg_dst, nch = tables
    t = h.shape[0]
    return pl.pallas_call(
        _combine_kernel,
        grid_spec=pltpu.PrefetchScalarGridSpec(
            num_scalar_prefetch=3,
            grid=(t // TOK_TILE,),
            in_specs=[
                pl.BlockSpec((TOK_TILE, D_MODEL), lambda i, *_: (i, 0)),
                pl.BlockSpec((TOK_TILE, TOP_K), lambda i, *_: (i, 0)),
                pl.BlockSpec((TOK_TILE, TOP_K), lambda i, *_: (i, 0)),
                pl.BlockSpec(memory_space=pl.ANY),
            ],
            out_specs=pl.BlockSpec((TOK_TILE, D_MODEL), lambda i, *_: (i, 0)),
            scratch_shapes=[pltpu.VMEM((SLOTS, D_MODEL), BF16), pltpu.SemaphoreType.DMA(())],
        ),
        out_shape=jax.ShapeDtypeStruct((t, D_MODEL), F32),
        compiler_params=_cparams(("arbitrary",)),
        name="combine",
    )(seg_src, seg_dst, nch, h, slots_t, w_t, ys)


def kernel(x, attn_norm_w, w_in, da_q_norm_w, da_k_norm_w, da_lambda_q1, da_lambda_k1, da_lambda_q2,
           da_lambda_k2, da_subln_w, gla_gate_up_w, gla_gate_up_b, gla_norm_w, w_out, ffn_norm_w,
           router_w, router_b, exp_w_gate, exp_b_gate, exp_w_up, exp_b_up, exp_w_down, exp_b_down):
    b, s, d = x.shape
    t = b * s
    assert d == D_MODEL and t % TOK_TILE == 0 and s % CHUNK == 0 and attn_norm_w.shape[0] == 1
    x2 = x.reshape(t, d)

    w_in_bf = jnp.pad(w_in[0].astype(BF16), ((0, 0), (0, IN_COLS_PAD - IN_COLS)))
    proj = _in_proj(x2, attn_norm_w, w_in_bf)
    proj3 = proj.reshape(b, s, IN_COLS_PAD)

    out_a = _diff_attn(proj3, jnp.tile(da_q_norm_w, (1, 2 * ATT_HP)), jnp.tile(da_k_norm_w, (1, 2 * ATT_HP)),
                       da_lambda_q1, da_lambda_k1, da_lambda_q2, da_lambda_k2, da_subln_w)
    wup_pad = jnp.pad(gla_gate_up_w[0].astype(BF16), ((0, LANE - GLA_RANK), (0, 0)))
    out_b = _gla(proj3, wup_pad, gla_gate_up_b, gla_norm_w)

    h, hn2, slots, top_w, nch_b = _out_router(
        x2, out_a.reshape(t, -1), out_b.reshape(t, -1), w_out[0].astype(BF16), ffn_norm_w,
        router_w[0].T, router_b.reshape(N_EXPERTS, 1))

    n_tok_tiles = t // TOK_TILE
    nch = nch_b[:, 0].reshape(n_tok_tiles, N_EXPERTS)
    n_row_tiles = (TOP_K * t + n_tok_tiles * N_EXPERTS * (ROW_CHUNK - 1)) // MOE_TM + N_EXPERTS
    seg_src, seg_dst, nch_f, gap_start, gap_len, tile_e, first, n_used = _routing_tables(nch, n_row_tiles)

    xs = _compact((seg_src, seg_dst, nch_f, gap_start, gap_len), hn2, slots, n_row_tiles * MOE_TM)
    hs = _moe_up(tile_e, first, n_used, xs,
                 exp_w_gate.reshape(N_EXPERTS, D_MODEL, D_FF), exp_w_up.reshape(N_EXPERTS, D_MODEL, D_FF),
                 exp_b_gate.reshape(N_EXPERTS, 1, D_FF), exp_b_up.reshape(N_EXPERTS, 1, D_FF))
    ys = _moe_down(tile_e, first, n_used, hs, exp_w_down.reshape(N_EXPERTS, D_FF, D_MODEL),
                   exp_b_down.reshape(N_EXPERTS, 1, D_MODEL))
    out = _combine((seg_src, seg_dst, nch_f), h, slots.T, top_w.T, ys)
    return out.reshape(b, s, d)
```

```python
import functools
import math

import jax
import jax.numpy as jnp
from jax import lax
from jax.experimental import pallas as pl
from jax.experimental.pallas import tpu as pltpu

F32 = jnp.float32
BF16 = jnp.bfloat16
I32 = jnp.int32

D_MODEL = 2048
CHUNK = 64
EPS = 1e-6
DA_HEADS = 8
DA_QK = 64
DA_V = 128
GLA_HEADS = 4
GLA_DK = 128
GLA_DV = 256
GLA_RANK = 16
GLA_TAU = 16.0
N_EXPERTS = 32
TOP_K = 4
D_FF = 2048
SWIGLU_LIMIT = 7.0
SWIGLU_ALPHA = 1.702
LAMBDA_INIT = 0.8 - 0.6 * math.exp(-0.3 * 0)

IN_COLS = 6160
IN_COLS_PAD = 6272
LANE = 128
ROW_CHUNK = 16

IN_TM = 1024
IN_TN = 896
ATT_BQ = 256
ATT_HP = 4
TOK_TILE = 512
SLOTS = TOP_K * TOK_TILE + N_EXPERTS * ROW_CHUNK
SLOT_BLK = 256
MOE_TM = 256
MOE_FC = 1024
VMEM_LIMIT = 56 * 1024 * 1024


def _cparams(sem):
    return pltpu.CompilerParams(dimension_semantics=sem, vmem_limit_bytes=VMEM_LIMIT)


def _in_proj_kernel(x_ref, nw_ref, w_ref, o_ref, hn_ref):
    @pl.when(pl.program_id(1) == 0)
    def _():
        x = x_ref[...]
        ms = jnp.mean(x * x, axis=-1, keepdims=True)
        hn_ref[...] = (x * lax.rsqrt(ms + EPS) * nw_ref[...]).astype(BF16)

    o_ref[...] = jnp.dot(hn_ref[...], w_ref[...], preferred_element_type=F32).astype(o_ref.dtype)


def _in_proj(x2, norm_w, w_bf):
    t = x2.shape[0]
    tm = min(IN_TM, t)
    return pl.pallas_call(
        _in_proj_kernel,
        grid=(t // tm, IN_COLS_PAD // IN_TN),
        in_specs=[
            pl.BlockSpec((tm, D_MODEL), lambda i, j: (i, 0)),
            pl.BlockSpec((1, D_MODEL), lambda i, j: (0, 0)),
            pl.BlockSpec((D_MODEL, IN_TN), lambda i, j: (0, j)),
        ],
        out_specs=pl.BlockSpec((tm, IN_TN), lambda i, j: (i, j)),
        out_shape=jax.ShapeDtypeStruct((t, IN_COLS_PAD), BF16),
        scratch_shapes=[pltpu.VMEM((tm, D_MODEL), BF16)],
        compiler_params=_cparams(("parallel", "arbitrary")),
        name="in_proj",
    )(x2, norm_w, w_bf)


def _attn_kernel(q_ref, k_ref, v_ref, qw_ref, kw_ref, lq1_ref, lk1_ref, lq2_ref, lk2_ref, sw_ref,
                 o_ref, kn_ref, vt_ref, *, bq):
    qi = pl.program_id(2)
    first_map = lax.broadcasted_iota(I32, (1, LANE), 1) < DA_QK
    n_grp = 2 * bq // LANE
    shift = CHUNK.bit_length() - 1

    def qk_norm(x, w):
        x2 = x * x
        scale = []
        for hh in range(ATT_HP):
            xh = x2[:, hh * LANE:(hh + 1) * LANE]
            s1 = jnp.sum(jnp.where(first_map, xh, 0.0), axis=-1, keepdims=True)
            s2 = jnp.sum(jnp.where(first_map, 0.0, xh), axis=-1, keepdims=True)
            scale.append(jnp.where(first_map, lax.rsqrt(s1 / DA_QK + EPS), lax.rsqrt(s2 / DA_QK + EPS)))
        return x * jnp.concatenate(scale, axis=1) * w

    @pl.when(qi == 0)
    def _():
        kn_ref[...] = qk_norm(k_ref[...].astype(F32), kw_ref[...]).astype(BF16)
        vt_ref[...] = v_ref[...].astype(F32).T.astype(BF16)

    q = qk_norm(q_ref[...].astype(F32), qw_ref[...]) * (DA_QK ** -0.5 * math.log2(math.e))
    q_t = []
    for hh in range(ATT_HP):
        qh = q[:, hh * LANE:(hh + 1) * LANE]
        q_t.append(jnp.concatenate([jnp.where(first_map, qh, 0.0), jnp.where(first_map, 0.0, qh)],
                                   axis=0).T.astype(BF16))

    def step(j, carry, masked):
        r0 = pl.multiple_of(j * bq, bq)
        scores = [jnp.dot(kn_ref[pl.ds(r0, bq), hh * LANE:(hh + 1) * LANE], q_t[hh], preferred_element_type=F32)
                  for hh in range(ATT_HP)]
        out = []
        for hh in range(ATT_HP):
            ms, ls, alphas, ps = [], [], [], []
            for g in range(n_grp):
                m, l, _ = carry[hh][g]
                s = scores[hh][:, g * LANE:(g + 1) * LANE]
                if masked:
                    key_chunk = lax.broadcasted_iota(I32, (bq, LANE), 0) >> shift
                    qry_chunk = (lax.broadcasted_iota(I32, (bq, LANE), 1) + (g * LANE) % bq) >> shift
                    s = jnp.where(key_chunk <= qry_chunk, s, -jnp.inf)
                m_new = jnp.maximum(m, jnp.max(s, axis=0, keepdims=True))
                alpha = jnp.exp2(m - m_new)
                p = jnp.exp2(s - m_new)
                ms.append(m_new)
                ls.append(alpha * l + jnp.sum(p, axis=0, keepdims=True))
                alphas.append(alpha)
                ps.append(p.astype(BF16))
            pv = jnp.dot(vt_ref[hh * DA_V:(hh + 1) * DA_V, pl.ds(r0, bq)], jnp.concatenate(ps, axis=1),
                         preferred_element_type=F32)
            out.append(tuple((ms[g], ls[g], alphas[g] * carry[hh][g][2] + pv[:, g * LANE:(g + 1) * LANE])
                             for g in range(n_grp)))
        return tuple(out)

    init = tuple(tuple((jnp.full((1, LANE), -jnp.inf, F32), jnp.zeros((1, LANE), F32), jnp.zeros((DA_V, LANE), F32))
                       for _ in range(n_grp)) for _ in range(ATT_HP))
    carry = lax.fori_loop(0, qi, lambda j, c: step(j, c, False), init)
    carry = step(qi, carry, True)
    lam = (jnp.exp(jnp.sum(lq1_ref[...] * lk1_ref[...], axis=-1, keepdims=True))
           - jnp.exp(jnp.sum(lq2_ref[...] * lk2_ref[...], axis=-1, keepdims=True)) + LAMBDA_INIT)
    half = n_grp // 2
    for hh in range(ATT_HP):
        for g in range(half):
            o1 = carry[hh][g][2] / carry[hh][g][1]
            o2 = carry[hh][g + half][2] / carry[hh][g + half][1]
            out = o1 - lam * o2
            ms = jnp.mean(out * out, axis=0, keepdims=True)
            y = out * lax.rsqrt(ms + EPS) * sw_ref[...] * (1.0 - LAMBDA_INIT)
            o_ref[g * LANE:(g + 1) * LANE, hh * DA_V:(hh + 1) * DA_V] = y.T.astype(o_ref.dtype)


def _diff_attn(proj3, qw, kw, lq1, lk1, lq2, lk2, sw):
    b, s, _ = proj3.shape
    bq = min(ATT_BQ, s)
    hw = ATT_HP * LANE
    n_hg = DA_HEADS // ATT_HP
    vec = lambda n: pl.BlockSpec((1, n), lambda bi, h, qi: (0, 0))
    return pl.pallas_call(
        functools.partial(_attn_kernel, bq=bq),
        grid=(b, n_hg, s // bq),
        in_specs=[
            pl.BlockSpec((None, bq, hw), lambda bi, h, qi: (bi, qi, h)),
            pl.BlockSpec((None, s, hw), lambda bi, h, qi: (bi, 0, n_hg + h)),
            pl.BlockSpec((None, s, hw), lambda bi, h, qi: (bi, 0, 2 * n_hg + h)),
            vec(hw), vec(hw), vec(DA_QK), vec(DA_QK), vec(DA_QK), vec(DA_QK),
            pl.BlockSpec((DA_V, 1), lambda bi, h, qi: (0, 0)),
        ],
        out_specs=pl.BlockSpec((None, bq, hw), lambda bi, h, qi: (bi, qi, h)),
        out_shape=jax.ShapeDtypeStruct((b, s, DA_HEADS * DA_V), BF16),
        scratch_shapes=[pltpu.VMEM((s, hw), BF16), pltpu.VMEM((hw, s), BF16)],
        compiler_params=_cparams(("parallel", "parallel", "arbitrary")),
        name="diff_attn",
    )(proj3, proj3, proj3, qw, kw, lq1, lk1, lq2, lk2, sw.reshape(DA_V, 1))


GLA_SUB = 16


def _gla_kernel(q_ref, k_ref, v_ref, r_ref, gd_ref, wup_ref, bup_ref, nw_ref, o_ref, st_ref, *, n_chunks):
    st_ref[...] = jnp.zeros_like(st_ref)
    rr = lax.broadcasted_iota(I32, (CHUNK, CHUNK), 0)
    cc = lax.broadcasted_iota(I32, (CHUNK, CHUNK), 1)
    tril = jnp.where(cc <= rr, 1.0, 0.0).astype(BF16)
    sub_row = lax.broadcasted_iota(I32, (GLA_SUB, GLA_DK), 0)
    sub_lane = lax.broadcasted_iota(I32, (GLA_SUB, CHUNK), 1)
    chunk_row = lax.broadcasted_iota(I32, (CHUNK, GLA_DK), 0)
    nt = (((1,), (1,)), ((), ()))
    tn = (((0,), (0,)), ((), ()))

    def chunk(c, _):
        r0 = pl.multiple_of(c * CHUNK, CHUNK)
        z_all = jnp.dot(gd_ref[pl.ds(r0, CHUNK), :], wup_ref[...], preferred_element_type=F32) + bup_ref[...]
        for hh in range(GLA_HEADS):
            head_chunk(r0, hh, z_all[:, hh * GLA_DK:(hh + 1) * GLA_DK])
        return 0

    def head_chunk(r0, hh, z):
        kcol = slice(hh * GLA_DK, (hh + 1) * GLA_DK)
        vcol = slice(hh * GLA_DV, (hh + 1) * GLA_DV)
        q = q_ref[pl.ds(r0, CHUNK), kcol].astype(F32) * (GLA_DK ** -0.5)
        k = k_ref[pl.ds(r0, CHUNK), kcol].astype(F32)
        v = v_ref[pl.ds(r0, CHUNK), vcol]
        lg = (jnp.minimum(z, 0.0) - jnp.log(1.0 + jnp.exp(-jnp.abs(z)))) / GLA_TAU
        lg_hi = lg.astype(BF16)
        lg_lo = (lg - lg_hi.astype(F32)).astype(BF16)
        bcum = (jnp.dot(tril, lg_hi, preferred_element_type=F32)
                + jnp.dot(tril, lg_lo, preferred_element_type=F32))
        bprev = bcum - lg
        st = st_ref[hh]
        o_inter = lax.dot_general((q * jnp.exp(bcum)).astype(BF16), st.astype(BF16), nt,
                                  preferred_element_type=F32)
        for blk in range(CHUNK // GLA_SUB):
            lo = blk * GLA_SUB
            hi = lo + GLA_SUB
            b_blk = bcum[lo:hi]
            q_blk = q[lo:hi]
            if blk > 0:
                ref = bprev[lo:lo + 1]
                qt = (q_blk * jnp.exp(b_blk - ref)).astype(BF16)
                kt = (k * jnp.exp(jnp.where(chunk_row < lo, ref - bcum, -jnp.inf))).astype(BF16)
                a_blk = lax.dot_general(qt, kt, nt, preferred_element_type=F32)
            else:
                a_blk = jnp.zeros((GLA_SUB, CHUNK), F32)
            for jj in range(GLA_SUB):
                dlt = jnp.where(sub_row >= jj, b_blk - bcum[lo + jj:lo + jj + 1], -jnp.inf)
                col = jnp.sum(q_blk * k[lo + jj:lo + jj + 1] * jnp.exp(dlt), axis=-1, keepdims=True)
                a_blk = a_blk + jnp.where(sub_lane == lo + jj, col, 0.0)
            o_blk = o_inter[lo:hi] + jnp.dot(a_blk.astype(BF16), v, preferred_element_type=F32)
            ms = jnp.mean(o_blk * o_blk, axis=-1, keepdims=True)
            rg = r_ref[pl.ds(r0 + lo, GLA_SUB), vcol].astype(F32)
            y = o_blk * lax.rsqrt(ms + EPS) * nw_ref[...] * (rg / (1.0 + jnp.exp(-rg)))
            o_ref[pl.ds(r0 + lo, GLA_SUB), vcol] = y.astype(o_ref.dtype)
        b_last = bcum[CHUNK - 1:CHUNK]
        k_dec = (k * jnp.exp(b_last - bcum)).astype(BF16)
        st_ref[hh] = st * jnp.exp(b_last) + lax.dot_general(v, k_dec, tn, preferred_element_type=F32)

    lax.fori_loop(0, n_chunks, chunk, 0)


def _gla(proj3, wup_pad, bup, nw):
    b, s, _ = proj3.shape
    kw = GLA_HEADS * GLA_DK
    vw = GLA_HEADS * GLA_DV
    q0 = 3 * DA_HEADS * DA_V // kw
    k0 = q0 + 1
    v0 = (k0 + 1) * kw // vw
    r0 = v0 + 1
    gd = (IN_COLS - GLA_RANK) // LANE
    return pl.pallas_call(
        functools.partial(_gla_kernel, n_chunks=s // CHUNK),
        grid=(b,),
        in_specs=[
            pl.BlockSpec((None, s, kw), lambda bi: (bi, 0, q0)),
            pl.BlockSpec((None, s, kw), lambda bi: (bi, 0, k0)),
            pl.BlockSpec((None, s, vw), lambda bi: (bi, 0, v0)),
            pl.BlockSpec((None, s, vw), lambda bi: (bi, 0, r0)),
            pl.BlockSpec((None, s, LANE), lambda bi: (bi, 0, gd)),
            pl.BlockSpec((LANE, kw), lambda bi: (0, 0)),
            pl.BlockSpec((1, kw), lambda bi: (0, 0)),
            pl.BlockSpec((1, GLA_DV), lambda bi: (0, 0)),
        ],
        out_specs=pl.BlockSpec((None, s, vw), lambda bi: (bi, 0, 0)),
        out_shape=jax.ShapeDtypeStruct((b, s, vw), BF16),
        scratch_shapes=[pltpu.VMEM((GLA_HEADS, GLA_DV, GLA_DK), F32)],
        compiler_params=_cparams(("parallel",)),
        name="gla",
    )(proj3, proj3, proj3, proj3, proj3, wup_pad, bup, nw)


def _out_router_kernel(x_ref, a_ref, b_ref, wo_ref, nw_ref, rwt_ref, rb_ref,
                       h_ref, hn_ref, slot_ref, w_ref, nch_ref):
    tm = x_ref.shape[0]
    half = a_ref.shape[1]
    h = (x_ref[...]
         + jnp.dot(a_ref[...], wo_ref[0:half, :], preferred_element_type=F32)
         + jnp.dot(b_ref[...], wo_ref[half:2 * half, :], preferred_element_type=F32))
    h_ref[...] = h
    ms = jnp.mean(h * h, axis=-1, keepdims=True)
    hn = h * lax.rsqrt(ms + EPS) * nw_ref[...]
    hn_hi = hn.astype(BF16)
    hn_ref[...] = hn_hi
    hn_lo = (hn - hn_hi.astype(F32)).astype(BF16)
    rwt = rwt_ref[...]
    rw_hi = rwt.astype(BF16)
    rw_lo = (rwt - rw_hi.astype(F32)).astype(BF16)
    nt = (((1,), (1,)), ((), ()))
    logits = (lax.dot_general(rw_hi, hn_hi, nt, preferred_element_type=F32)
              + lax.dot_general(rw_hi, hn_lo, nt, preferred_element_type=F32)
              + lax.dot_general(rw_lo, hn_hi, nt, preferred_element_type=F32)
              + rb_ref[...])
    e_iota = lax.broadcasted_iota(I32, (N_EXPERTS, tm), 0).astype(F32)
    vals, hots = [], []
    for _ in range(TOP_K):
        m = jnp.max(logits, axis=0, keepdims=True)
        idx = jnp.min(jnp.where(logits == m, e_iota, float(N_EXPERTS)), axis=0, keepdims=True)
        hot = e_iota == idx
        vals.append(m)
        hots.append(hot)
        logits = jnp.where(hot, -jnp.inf, logits)
    exps = [jnp.exp(v - vals[0]) for v in vals]
    denom = exps[0] + exps[1] + exps[2] + exps[3]
    for kk in range(TOP_K):
        w_ref[kk:kk + 1, :] = exps[kk] / denom
    sel = jnp.zeros((N_EXPERTS, tm), F32)
    for hot in hots:
        sel = sel + jnp.where(hot, 1.0, 0.0)
    tr = lax.broadcasted_iota(I32, (tm, tm), 0)
    tc = lax.broadcasted_iota(I32, (tm, tm), 1)
    upper = jnp.where(tr < tc, 1.0, 0.0).astype(BF16)
    rank = jnp.dot(sel.astype(BF16), upper, preferred_element_type=F32)
    cnt = jnp.sum(sel, axis=1, keepdims=True)
    nch = jnp.floor((cnt + (ROW_CHUNK - 1)) * (1.0 / ROW_CHUNK))
    nch_b = jnp.broadcast_to(nch, (N_EXPERTS, LANE))
    er = lax.broadcasted_iota(I32, (N_EXPERTS, N_EXPERTS), 0)
    ec = lax.broadcasted_iota(I32, (N_EXPERTS, N_EXPERTS), 1)
    lower = jnp.where(ec < er, 1.0, 0.0).astype(BF16)
    seg = jnp.dot(lower, nch_b.astype(BF16), preferred_element_type=F32)
    pos = seg[:, 0:1] * ROW_CHUNK + rank
    for kk in range(TOP_K):
        slot_ref[kk:kk + 1, :] = jnp.sum(jnp.where(hots[kk], pos, 0.0), axis=0, keepdims=True).astype(I32)
    nch_ref[...] = nch_b.astype(I32)


def _out_router(x2, out_a, out_b, wo_bf, nw, rwt, rb):
    t = x2.shape[0]
    tm = TOK_TILE
    half = out_a.shape[1]
    nt = t // tm
    return pl.pallas_call(
        _out_router_kernel,
        grid=(nt,),
        in_specs=[
            pl.BlockSpec((tm, D_MODEL), lambda i: (i, 0)),
            pl.BlockSpec((tm, half), lambda i: (i, 0)),
            pl.BlockSpec((tm, half), lambda i: (i, 0)),
            pl.BlockSpec((2 * half, D_MODEL), lambda i: (0, 0)),
            pl.BlockSpec((1, D_MODEL), lambda i: (0, 0)),
            pl.BlockSpec((N_EXPERTS, D_MODEL), lambda i: (0, 0)),
            pl.BlockSpec((N_EXPERTS, 1), lambda i: (0, 0)),
        ],
        out_specs=[
            pl.BlockSpec((tm, D_MODEL), lambda i: (i, 0)),
            pl.BlockSpec((tm, D_MODEL), lambda i: (i, 0)),
            pl.BlockSpec((TOP_K, tm), lambda i: (0, i)),
            pl.BlockSpec((TOP_K, tm), lambda i: (0, i)),
            pl.BlockSpec((N_EXPERTS, LANE), lambda i: (i, 0)),
        ],
        out_shape=[
            jax.ShapeDtypeStruct((t, D_MODEL), F32),
            jax.ShapeDtypeStruct((t, D_MODEL), BF16),
            jax.ShapeDtypeStruct((TOP_K, t), I32),
            jax.ShapeDtypeStruct((TOP_K, t), F32),
            jax.ShapeDtypeStruct((nt * N_EXPERTS, LANE), I32),
        ],
        compiler_params=_cparams(("parallel",)),
        name="out_router",
    )(x2, out_a, out_b, wo_bf, nw, rwt, rb)


def _routing_tables(nch, n_row_tiles):
    cpt = MOE_TM // ROW_CHUNK
    seg_src = jnp.cumsum(nch, axis=1) - nch
    rows_e = jnp.sum(nch, axis=0)
    tiles_e = (rows_e + cpt - 1) // cpt
    cum_tiles = jnp.cumsum(tiles_e)
    base_e = (cum_tiles - tiles_e) * cpt
    seg_dst = base_e[None, :] + jnp.cumsum(nch, axis=0) - nch
    gap_start = base_e + rows_e
    gap_len = tiles_e * cpt - rows_e
    n_used = cum_tiles[-1]
    tile_ids = jnp.arange(n_row_tiles, dtype=I32)
    tile_e = jnp.sum((cum_tiles[None, :] <= jnp.minimum(tile_ids, n_used - 1)[:, None]).astype(I32), axis=1)
    tile_e = jnp.minimum(tile_e, N_EXPERTS - 1)
    first = jnp.concatenate([jnp.ones((1,), I32), (tile_e[1:] != tile_e[:-1]).astype(I32)])
    e_ids = jnp.arange(N_EXPERTS, dtype=I32)
    later = (e_ids[None, :] > e_ids[:, None]) & (tiles_e[None, :] > 0)
    next_e = jnp.min(jnp.where(later, e_ids[None, :], N_EXPERTS), axis=1)
    next_e = jnp.where(next_e == N_EXPERTS, -1, next_e)
    tile_next = next_e[tile_e]
    tile_k = tile_ids - (cum_tiles - tiles_e)[tile_e]
    tile_rows = jnp.clip(rows_e[tile_e] * ROW_CHUNK - tile_k * MOE_TM, 0, MOE_TM)
    tot_gather = jnp.sum(nch, axis=1)
    tot_scatter = tot_gather.at[-1].add(jnp.sum(gap_len))
    flat = lambda a: a.reshape(-1).astype(I32)
    return (flat(seg_src), flat(seg_dst), flat(nch), flat(gap_start), flat(gap_len),
            flat(tot_scatter), flat(tot_gather),
            (tile_e.astype(I32), first, tile_next.astype(I32), tile_rows.astype(I32),
             n_used.reshape(1).astype(I32)))


def _chunk_rows(ref, chunk_idx):
    return ref.at[pl.ds(pl.multiple_of(chunk_idx * ROW_CHUNK, ROW_CHUNK), ROW_CHUNK), :]


def _compact_kernel(src_ref, dst_ref, nch_ref, gap_start_ref, gap_len_ref, tot_ref,
                    x_ref, slot_ref, xs_ref, buf_ref, sems):
    tt = pl.program_id(0)
    n_tt = pl.num_programs(0)
    cur = tt & 1
    x = x_ref[...]
    s0, s1, s2, s3 = (slot_ref[kk:kk + 1, :] for kk in range(TOP_K))
    for blk in range(SLOTS // SLOT_BLK):
        sid = lax.broadcasted_iota(I32, (SLOT_BLK, TOK_TILE), 0) + blk * SLOT_BLK
        hit = jnp.where(sid == s0, 1.0, jnp.where(sid == s1, 1.0, jnp.where(sid == s2, 1.0,
                        jnp.where(sid == s3, 1.0, 0.0))))
        buf_ref[cur, blk * SLOT_BLK:(blk + 1) * SLOT_BLK, :] = jnp.dot(
            hit.astype(BF16), x, preferred_element_type=F32).astype(BF16)

    def copy(slot, src_chunk, dst_chunk):
        return pltpu.make_async_copy(_chunk_rows(buf_ref.at[slot], src_chunk), _chunk_rows(xs_ref, dst_chunk),
                                     sems.at[slot])

    def per_expert(e, _):
        n = nch_ref[tt * N_EXPERTS + e]
        src0 = src_ref[tt * N_EXPERTS + e]
        dst0 = dst_ref[tt * N_EXPERTS + e]

        def issue(kk, _):
            copy(cur, src0 + kk, dst0 + kk).start()
            return 0

        lax.fori_loop(0, n, issue, 0)
        return 0

    lax.fori_loop(0, N_EXPERTS, per_expert, 0)

    zero_chunk = SLOTS // ROW_CHUNK - 1

    def per_gap(e, _):
        n = jnp.where(tt == n_tt - 1, gap_len_ref[e], 0)
        g0 = gap_start_ref[e]

        def issue(kk, _):
            copy(cur, zero_chunk, g0 + kk).start()
            return 0

        lax.fori_loop(0, n, issue, 0)
        return 0

    lax.fori_loop(0, N_EXPERTS, per_gap, 0)

    def drain(slot, count):
        def one(_, c):
            copy(slot, 0, 0).wait()
            return c

        lax.fori_loop(0, count, one, 0)

    @pl.when(tt > 0)
    def _():
        drain(1 - cur, tot_ref[tt - 1])

    @pl.when(tt == n_tt - 1)
    def _():
        drain(cur, tot_ref[tt])


def _compact(tables, hn2, slots, n_rows):
    seg_src, seg_dst, nch, gap_start, gap_len, tot = tables
    t = hn2.shape[0]
    return pl.pallas_call(
        _compact_kernel,
        grid_spec=pltpu.PrefetchScalarGridSpec(
            num_scalar_prefetch=6,
            grid=(t // TOK_TILE,),
            in_specs=[
                pl.BlockSpec((TOK_TILE, D_MODEL), lambda i, *_: (i, 0)),
                pl.BlockSpec((TOP_K, TOK_TILE), lambda i, *_: (0, i)),
            ],
            out_specs=pl.BlockSpec(memory_space=pl.ANY),
            scratch_shapes=[pltpu.VMEM((2, SLOTS, D_MODEL), BF16), pltpu.SemaphoreType.DMA((2,))],
        ),
        out_shape=jax.ShapeDtypeStruct((n_rows, D_MODEL), BF16),
        compiler_params=_cparams(("arbitrary",)),
        name="compact",
    )(seg_src, seg_dst, nch, gap_start, gap_len, tot, hn2, slots)


def _per_fill(rows, compute):
    @pl.when(rows > MOE_TM // 2)
    def _():
        compute(MOE_TM)

    @pl.when(rows <= MOE_TM // 2)
    def _():
        compute(MOE_TM // 2)


def _moe_up_kernel(te_ref, first_ref, next_ref, rows_ref, nu_ref, x_ref, wg_hbm, wu_hbm, bg_ref, bu_ref, h_ref,
                   wg_st, wu_st, wg_bf, wu_bf, sems):
    c = pl.program_id(0)
    i = pl.program_id(1)

    def fetch(e, cc):
        col = pl.ds(pl.multiple_of(cc * MOE_FC, MOE_FC), MOE_FC)
        return (pltpu.make_async_copy(wg_hbm.at[e, :, col], wg_st, sems.at[0]),
                pltpu.make_async_copy(wu_hbm.at[e, :, col], wu_st, sems.at[1]))

    @pl.when((c == 0) & (i == 0))
    def _():
        for cp in fetch(te_ref[0], 0):
            cp.start()

    @pl.when(i < nu_ref[0])
    def _():
        @pl.when(first_ref[i] == 1)
        def _():
            for cp in fetch(te_ref[i], c):
                cp.wait()
            wg_bf[...] = wg_st[...].astype(BF16)
            wu_bf[...] = wu_st[...].astype(BF16)
            nxt = next_ref[i]

            @pl.when(nxt >= 0)
            def _():
                for cp in fetch(nxt, c):
                    cp.start()

            @pl.when((nxt < 0) & (c + 1 < pl.num_programs(0)))
            def _():
                for cp in fetch(te_ref[0], c + 1):
                    cp.start()

        def compute(n):
            x = x_ref[0:n, :]
            gate = jnp.minimum(jnp.dot(x, wg_bf[...], preferred_element_type=F32) + bg_ref[...], SWIGLU_LIMIT)
            up = jnp.clip(jnp.dot(x, wu_bf[...], preferred_element_type=F32) + bu_ref[...],
                          -SWIGLU_LIMIT, SWIGLU_LIMIT)
            act = (up + 1.0) * gate * (1.0 / (1.0 + jnp.exp(-SWIGLU_ALPHA * gate)))
            h_ref[0:n, :] = act.astype(h_ref.dtype)

        _per_fill(rows_ref[i], compute)


def _moe_up(tiles, xs, wg, wu, bg, bu):
    n_rows = xs.shape[0]
    n_tiles = n_rows // MOE_TM
    row = lambda c, i, te, fi, nx, rw, nu: (jnp.minimum(i, nu[0] - 1), 0)
    bsel = lambda c, i, te, fi, nx, rw, nu: (te[i], 0, c)
    return pl.pallas_call(
        _moe_up_kernel,
        grid_spec=pltpu.PrefetchScalarGridSpec(
            num_scalar_prefetch=5,
            grid=(D_FF // MOE_FC, n_tiles),
            in_specs=[
                pl.BlockSpec((MOE_TM, D_MODEL), row),
                pl.BlockSpec(memory_space=pl.ANY),
                pl.BlockSpec(memory_space=pl.ANY),
                pl.BlockSpec((None, 1, MOE_FC), bsel),
                pl.BlockSpec((None, 1, MOE_FC), bsel),
            ],
            out_specs=pl.BlockSpec((MOE_TM, MOE_FC),
                                   lambda c, i, te, fi, nx, rw, nu: (jnp.minimum(i, nu[0] - 1), c)),
            scratch_shapes=[pltpu.VMEM((D_MODEL, MOE_FC), F32), pltpu.VMEM((D_MODEL, MOE_FC), F32),
                            pltpu.VMEM((D_MODEL, MOE_FC), BF16), pltpu.VMEM((D_MODEL, MOE_FC), BF16),
                            pltpu.SemaphoreType.DMA((2,))],
        ),
        out_shape=jax.ShapeDtypeStruct((n_rows, D_FF), BF16),
        compiler_params=_cparams(("arbitrary", "arbitrary")),
        name="moe_up",
    )(*tiles, xs, wg, wu, bg, bu)


def _moe_down_kernel(te_ref, first_ref, next_ref, rows_ref, nu_ref, h_ref, wd_hbm, bd_ref, y_ref,
                     wd_st, wd_bf, sem):
    i = pl.program_id(0)

    def fetch(e):
        return pltpu.make_async_copy(wd_hbm.at[e], wd_st, sem)

    @pl.when(i == 0)
    def _():
        fetch(te_ref[0]).start()

    @pl.when(i < nu_ref[0])
    def _():
        @pl.when(first_ref[i] == 1)
        def _():
            fetch(te_ref[i]).wait()
            wd_bf[...] = wd_st[...].astype(BF16)
            nxt = next_ref[i]

            @pl.when(nxt >= 0)
            def _():
                fetch(nxt).start()

        def compute(n):
            y = jnp.dot(h_ref[0:n, :], wd_bf[...], preferred_element_type=F32) + bd_ref[...]
            y_ref[0:n, :] = y.astype(y_ref.dtype)

        _per_fill(rows_ref[i], compute)


def _moe_down(tiles, hs, wd, bd):
    n_rows = hs.shape[0]
    n_tiles = n_rows // MOE_TM
    row = lambda i, te, fi, nx, rw, nu: (jnp.minimum(i, nu[0] - 1), 0)
    return pl.pallas_call(
        _moe_down_kernel,
        grid_spec=pltpu.PrefetchScalarGridSpec(
            num_scalar_prefetch=5,
            grid=(n_tiles,),
            in_specs=[
                pl.BlockSpec((MOE_TM, D_FF), row),
                pl.BlockSpec(memory_space=pl.ANY),
                pl.BlockSpec((None, 1, D_MODEL), lambda i, te, fi, nx, rw, nu: (te[i], 0, 0)),
            ],
            out_specs=pl.BlockSpec((MOE_TM, D_MODEL), row),
            scratch_shapes=[pltpu.VMEM((D_FF, D_MODEL), F32), pltpu.VMEM((D_FF, D_MODEL), BF16),
                            pltpu.SemaphoreType.DMA(())],
        ),
        out_shape=jax.ShapeDtypeStruct((n_rows, D_MODEL), BF16),
        compiler_params=_cparams(("arbitrary",)),
        name="moe_down",
    )(*tiles, hs, wd, bd)


def _combine_kernel(src_ref, dst_ref, nch_ref, tot_ref, h_ref, slot_ref, w_ref, ys_ref, o_ref, buf_ref, sems):
    tt = pl.program_id(0)
    n_tt = pl.num_programs(0)
    cur = tt & 1

    def copy(slot, sorted_chunk, local_chunk):
        return pltpu.make_async_copy(_chunk_rows(ys_ref, sorted_chunk), _chunk_rows(buf_ref.at[slot], local_chunk),
                                     sems.at[slot])

    def gather(tile, slot):
        def per_expert(e, _):
            n = nch_ref[tile * N_EXPERTS + e]
            loc0 = src_ref[tile * N_EXPERTS + e]
            srt0 = dst_ref[tile * N_EXPERTS + e]

            def issue(kk, _):
                copy(slot, srt0 + kk, loc0 + kk).start()
                return 0

            lax.fori_loop(0, n, issue, 0)
            return 0

        lax.fori_loop(0, N_EXPERTS, per_expert, 0)

    @pl.when(tt == 0)
    def _():
        buf_ref[...] = jnp.zeros_like(buf_ref)
        gather(0, 0)

    @pl.when(tt + 1 < n_tt)
    def _():
        gather(tt + 1, 1 - cur)

    def one(_, c):
        copy(cur, 0, 0).wait()
        return c

    lax.fori_loop(0, tot_ref[tt], one, 0)

    o_ref[...] = h_ref[...]
    for blk in range(SLOTS // SLOT_BLK):
        sid = lax.broadcasted_iota(I32, (TOK_TILE, SLOT_BLK), 1) + blk * SLOT_BLK
        wmat = jnp.zeros((TOK_TILE, SLOT_BLK), F32)
        for kk in range(TOP_K):
            wmat = wmat + jnp.where(sid == slot_ref[:, kk:kk + 1], w_ref[:, kk:kk + 1], 0.0)
        w_hi = wmat.astype(BF16)
        w_lo = (wmat - w_hi.astype(F32)).astype(BF16)
        yb = buf_ref[cur, blk * SLOT_BLK:(blk + 1) * SLOT_BLK, :]
        o_ref[...] += (jnp.dot(w_hi, yb, preferred_element_type=F32)
                       + jnp.dot(w_lo, yb, preferred_element_type=F32))


def _combine(tables, h, slots_t, w_t, ys):
    seg_src, seg_dst, nch, tot = tables
    t = h.shape[0]
    return pl.pallas_call(
        _combine_kernel,
        grid_spec=pltpu.PrefetchScalarGridSpec(
            num_scalar_prefetch=4,
            grid=(t // TOK_TILE,),
            in_specs=[
                pl.BlockSpec((TOK_TILE, D_MODEL), lambda i, *_: (i, 0)),
                pl.BlockSpec((TOK_TILE, TOP_K), lambda i, *_: (i, 0)),
                pl.BlockSpec((TOK_TILE, TOP_K), lambda i, *_: (i, 0)),
                pl.BlockSpec(memory_space=pl.ANY),
            ],
            out_specs=pl.BlockSpec((TOK_TILE, D_MODEL), lambda i, *_: (i, 0)),
            scratch_shapes=[pltpu.VMEM((2, SLOTS, D_MODEL), BF16), pltpu.SemaphoreType.DMA((2,))],
        ),
        out_shape=jax.ShapeDtypeStruct((t, D_MODEL), F32),
        compiler_params=_cparams(("arbitrary",)),
        name="combine",
    )(seg_src, seg_dst, nch, tot, h, slots_t, w_t, ys)


def kernel(x, attn_norm_w, w_in, da_q_norm_w, da_k_norm_w, da_lambda_q1, da_lambda_k1, da_lambda_q2,
           da_lambda_k2, da_subln_w, gla_gate_up_w, gla_gate_up_b, gla_norm_w, w_out, ffn_norm_w,
           router_w, router_b, exp_w_gate, exp_b_gate, exp_w_up, exp_b_up, exp_w_down, exp_b_down):
    b, s, d = x.shape
    t = b * s
    assert d == D_MODEL and t % TOK_TILE == 0 and s % CHUNK == 0 and attn_norm_w.shape[0] == 1
    x2 = x.reshape(t, d)

    w_in_bf = jnp.pad(w_in[0].astype(BF16), ((0, 0), (0, IN_COLS_PAD - IN_COLS)))
    proj = _in_proj(x2, attn_norm_w, w_in_bf)
    proj3 = proj.reshape(b, s, IN_COLS_PAD)

    out_a = _diff_attn(proj3, jnp.tile(da_q_norm_w, (1, 2 * ATT_HP)), jnp.tile(da_k_norm_w, (1, 2 * ATT_HP)),
                       da_lambda_q1, da_lambda_k1, da_lambda_q2, da_lambda_k2, da_subln_w)
    wup_pad = jnp.pad(gla_gate_up_w[0].astype(BF16), ((0, LANE - GLA_RANK), (0, 0)))
    out_b = _gla(proj3, wup_pad, gla_gate_up_b, gla_norm_w)

    h, hn2, slots, top_w, nch_b = _out_router(
        x2, out_a.reshape(t, -1), out_b.reshape(t, -1), w_out[0].astype(BF16), ffn_norm_w,
        router_w[0].T, router_b.reshape(N_EXPERTS, 1))

    n_tok_tiles = t // TOK_TILE
    nch = nch_b[:, 0].reshape(n_tok_tiles, N_EXPERTS)
    n_row_tiles = (TOP_K * t + n_tok_tiles * N_EXPERTS * (ROW_CHUNK - 1)) // MOE_TM + N_EXPERTS
    seg_src, seg_dst, nch_f, gap_start, gap_len, tot_scatter, tot_gather, tiles = _routing_tables(nch, n_row_tiles)

    xs = _compact((seg_src, seg_dst, nch_f, gap_start, gap_len, tot_scatter), hn2, slots, n_row_tiles * MOE_TM)
    hs = _moe_up(tiles, xs,
                 exp_w_gate.reshape(N_EXPERTS, D_MODEL, D_FF), exp_w_up.reshape(N_EXPERTS, D_MODEL, D_FF),
                 exp_b_gate.reshape(N_EXPERTS, 1, D_FF), exp_b_up.reshape(N_EXPERTS, 1, D_FF))
    ys = _moe_down(tiles, hs, exp_w_down.reshape(N_EXPERTS, D_FF, D_MODEL),
                   exp_b_down.reshape(N_EXPERTS, 1, D_MODEL))
    out = _combine((seg_src, seg_dst, nch_f, tot_gather), h, slots.T, top_w.T, ys)
    return out.reshape(b, s, d)
```

```python
import functools
import math

import jax
import jax.numpy as jnp
from jax import lax
from jax.experimental import pallas as pl
from jax.experimental.pallas import tpu as pltpu

F32 = jnp.float32
BF16 = jnp.bfloat16
I32 = jnp.int32

D_MODEL = 2048
CHUNK = 64
EPS = 1e-6
DA_HEADS = 8
DA_QK = 64
DA_V = 128
GLA_HEADS = 4
GLA_DK = 128
GLA_DV = 256
GLA_RANK = 16
GLA_TAU = 16.0
N_EXPERTS = 32
TOP_K = 4
D_FF = 2048
SWIGLU_LIMIT = 7.0
SWIGLU_ALPHA = 1.702
LAMBDA_INIT = 0.8 - 0.6 * math.exp(-0.3 * 0)

IN_COLS = 6160
IN_COLS_PAD = 6272
LANE = 128
ROW_CHUNK = 16

IN_TM = 1024
IN_TN = 896
ATT_BQ = 256
ATT_HP = 4
TOK_TILE = 512
SLOTS = TOP_K * TOK_TILE + N_EXPERTS * ROW_CHUNK
SLOT_BLK = 256
XS_COLS = D_MODEL + LANE
MOE_TM = 512
MOE_SUB = 128
MOE_FC = 1024
VMEM_LIMIT = 56 * 1024 * 1024


def _cparams(sem):
    return pltpu.CompilerParams(dimension_semantics=sem, vmem_limit_bytes=VMEM_LIMIT)


def _in_proj_kernel(x_ref, nw_ref, w_ref, o_ref, hn_ref):
    @pl.when(pl.program_id(1) == 0)
    def _():
        x = x_ref[...]
        ms = jnp.mean(x * x, axis=-1, keepdims=True)
        hn_ref[...] = (x * lax.rsqrt(ms + EPS) * nw_ref[...]).astype(BF16)

    o_ref[...] = jnp.dot(hn_ref[...], w_ref[...], preferred_element_type=F32).astype(o_ref.dtype)


def _in_proj(x2, norm_w, w_bf):
    t = x2.shape[0]
    tm = min(IN_TM, t)
    return pl.pallas_call(
        _in_proj_kernel,
        grid=(t // tm, IN_COLS_PAD // IN_TN),
        in_specs=[
            pl.BlockSpec((tm, D_MODEL), lambda i, j: (i, 0)),
            pl.BlockSpec((1, D_MODEL), lambda i, j: (0, 0)),
            pl.BlockSpec((D_MODEL, IN_TN), lambda i, j: (0, j)),
        ],
        out_specs=pl.BlockSpec((tm, IN_TN), lambda i, j: (i, j)),
        out_shape=jax.ShapeDtypeStruct((t, IN_COLS_PAD), BF16),
        scratch_shapes=[pltpu.VMEM((tm, D_MODEL), BF16)],
        compiler_params=_cparams(("parallel", "arbitrary")),
        name="in_proj",
    )(x2, norm_w, w_bf)


def _attn_kernel(q_ref, k_ref, v_ref, qw_ref, kw_ref, lq1_ref, lk1_ref, lq2_ref, lk2_ref, sw_ref,
                 o_ref, kn_ref, vt_ref, *, bq):
    qi = pl.program_id(2)
    first_map = lax.broadcasted_iota(I32, (1, LANE), 1) < DA_QK
    n_grp = 2 * bq // LANE
    shift = CHUNK.bit_length() - 1

    def qk_norm(x, w):
        x2 = x * x
        scale = []
        for hh in range(ATT_HP):
            xh = x2[:, hh * LANE:(hh + 1) * LANE]
            s1 = jnp.sum(jnp.where(first_map, xh, 0.0), axis=-1, keepdims=True)
            s2 = jnp.sum(jnp.where(first_map, 0.0, xh), axis=-1, keepdims=True)
            scale.append(jnp.where(first_map, lax.rsqrt(s1 / DA_QK + EPS), lax.rsqrt(s2 / DA_QK + EPS)))
        return x * jnp.concatenate(scale, axis=1) * w

    @pl.when(qi == 0)
    def _():
        kn_ref[...] = qk_norm(k_ref[...].astype(F32), kw_ref[...]).astype(BF16)
        vt_ref[...] = v_ref[...].astype(F32).T.astype(BF16)

    q = qk_norm(q_ref[...].astype(F32), qw_ref[...]) * (DA_QK ** -0.5 * math.log2(math.e))
    q_t = []
    for hh in range(ATT_HP):
        qh = q[:, hh * LANE:(hh + 1) * LANE]
        q_t.append(jnp.concatenate([jnp.where(first_map, qh, 0.0), jnp.where(first_map, 0.0, qh)],
                                   axis=0).T.astype(BF16))

    def step(j, carry, masked):
        r0 = pl.multiple_of(j * bq, bq)
        scores = [jnp.dot(kn_ref[pl.ds(r0, bq), hh * LANE:(hh + 1) * LANE], q_t[hh], preferred_element_type=F32)
                  for hh in range(ATT_HP)]
        out = []
        for hh in range(ATT_HP):
            ms, ls, alphas, ps = [], [], [], []
            for g in range(n_grp):
                m, l, _ = carry[hh][g]
                s = scores[hh][:, g * LANE:(g + 1) * LANE]
                if masked:
                    key_chunk = lax.broadcasted_iota(I32, (bq, LANE), 0) >> shift
                    qry_chunk = (lax.broadcasted_iota(I32, (bq, LANE), 1) + (g * LANE) % bq) >> shift
                    s = jnp.where(key_chunk <= qry_chunk, s, -jnp.inf)
                m_new = jnp.maximum(m, jnp.max(s, axis=0, keepdims=True))
                alpha = jnp.exp2(m - m_new)
                p = jnp.exp2(s - m_new)
                ms.append(m_new)
                ls.append(alpha * l + jnp.sum(p, axis=0, keepdims=True))
                alphas.append(alpha)
                ps.append(p.astype(BF16))
            pv = jnp.dot(vt_ref[hh * DA_V:(hh + 1) * DA_V, pl.ds(r0, bq)], jnp.concatenate(ps, axis=1),
                         preferred_element_type=F32)
            out.append(tuple((ms[g], ls[g], alphas[g] * carry[hh][g][2] + pv[:, g * LANE:(g + 1) * LANE])
                             for g in range(n_grp)))
        return tuple(out)

    init = tuple(tuple((jnp.full((1, LANE), -jnp.inf, F32), jnp.zeros((1, LANE), F32), jnp.zeros((DA_V, LANE), F32))
                       for _ in range(n_grp)) for _ in range(ATT_HP))
    carry = lax.fori_loop(0, qi, lambda j, c: step(j, c, False), init)
    carry = step(qi, carry, True)
    lam = (jnp.exp(jnp.sum(lq1_ref[...] * lk1_ref[...], axis=-1, keepdims=True))
           - jnp.exp(jnp.sum(lq2_ref[...] * lk2_ref[...], axis=-1, keepdims=True)) + LAMBDA_INIT)
    half = n_grp // 2
    for hh in range(ATT_HP):
        for g in range(half):
            o1 = carry[hh][g][2] / carry[hh][g][1]
            o2 = carry[hh][g + half][2] / carry[hh][g + half][1]
            out = o1 - lam * o2
            ms = jnp.mean(out * out, axis=0, keepdims=True)
            y = out * lax.rsqrt(ms + EPS) * sw_ref[...] * (1.0 - LAMBDA_INIT)
            o_ref[g * LANE:(g + 1) * LANE, hh * DA_V:(hh + 1) * DA_V] = y.T.astype(o_ref.dtype)


def _diff_attn(proj3, qw, kw, lq1, lk1, lq2, lk2, sw):
    b, s, _ = proj3.shape
    bq = min(ATT_BQ, s)
    hw = ATT_HP * LANE
    n_hg = DA_HEADS // ATT_HP
    vec = lambda n: pl.BlockSpec((1, n), lambda bi, h, qi: (0, 0))
    return pl.pallas_call(
        functools.partial(_attn_kernel, bq=bq),
        grid=(b, n_hg, s // bq),
        in_specs=[
            pl.BlockSpec((None, bq, hw), lambda bi, h, qi: (bi, qi, h)),
            pl.BlockSpec((None, s, hw), lambda bi, h, qi: (bi, 0, n_hg + h)),
            pl.BlockSpec((None, s, hw), lambda bi, h, qi: (bi, 0, 2 * n_hg + h)),
            vec(hw), vec(hw), vec(DA_QK), vec(DA_QK), vec(DA_QK), vec(DA_QK),
            pl.BlockSpec((DA_V, 1), lambda bi, h, qi: (0, 0)),
        ],
        out_specs=pl.BlockSpec((None, bq, hw), lambda bi, h, qi: (bi, qi, h)),
        out_shape=jax.ShapeDtypeStruct((b, s, DA_HEADS * DA_V), BF16),
        scratch_shapes=[pltpu.VMEM((s, hw), BF16), pltpu.VMEM((hw, s), BF16)],
        compiler_params=_cparams(("parallel", "parallel", "arbitrary")),
        name="diff_attn",
    )(proj3, proj3, proj3, qw, kw, lq1, lk1, lq2, lk2, sw.reshape(DA_V, 1))


GLA_SUB = 16


def _gla_kernel(q_ref, k_ref, v_ref, r_ref, gd_ref, wup_ref, bup_ref, nw_ref, o_ref, st_ref, *, n_chunks):
    st_ref[...] = jnp.zeros_like(st_ref)
    rr = lax.broadcasted_iota(I32, (CHUNK, CHUNK), 0)
    cc = lax.broadcasted_iota(I32, (CHUNK, CHUNK), 1)
    tril = jnp.where(cc <= rr, 1.0, 0.0).astype(BF16)
    sub_row = lax.broadcasted_iota(I32, (GLA_SUB, GLA_DK), 0)
    sub_lane = lax.broadcasted_iota(I32, (GLA_SUB, CHUNK), 1)
    chunk_row = lax.broadcasted_iota(I32, (CHUNK, GLA_DK), 0)
    nt = (((1,), (1,)), ((), ()))
    tn = (((0,), (0,)), ((), ()))

    def chunk(c, _):
        r0 = pl.multiple_of(c * CHUNK, CHUNK)
        z_all = jnp.dot(gd_ref[pl.ds(r0, CHUNK), :], wup_ref[...], preferred_element_type=F32) + bup_ref[...]
        for hh in range(GLA_HEADS):
            head_chunk(r0, hh, z_all[:, hh * GLA_DK:(hh + 1) * GLA_DK])
        return 0

    def head_chunk(r0, hh, z):
        kcol = slice(hh * GLA_DK, (hh + 1) * GLA_DK)
        vcol = slice(hh * GLA_DV, (hh + 1) * GLA_DV)
        q = q_ref[pl.ds(r0, CHUNK), kcol].astype(F32) * (GLA_DK ** -0.5)
        k = k_ref[pl.ds(r0, CHUNK), kcol].astype(F32)
        v = v_ref[pl.ds(r0, CHUNK), vcol]
        lg = (jnp.minimum(z, 0.0) - jnp.log(1.0 + jnp.exp(-jnp.abs(z)))) / GLA_TAU
        lg_hi = lg.astype(BF16)
        lg_lo = (lg - lg_hi.astype(F32)).astype(BF16)
        bcum = (jnp.dot(tril, lg_hi, preferred_element_type=F32)
                + jnp.dot(tril, lg_lo, preferred_element_type=F32))
        bprev = bcum - lg
        st = st_ref[hh]
        o_inter = lax.dot_general((q * jnp.exp(bcum)).astype(BF16), st.astype(BF16), nt,
                                  preferred_element_type=F32)
        for blk in range(CHUNK // GLA_SUB):
            lo = blk * GLA_SUB
            hi = lo + GLA_SUB
            b_blk = bcum[lo:hi]
            q_blk = q[lo:hi]
            if blk > 0:
                ref = bprev[lo:lo + 1]
                qt = (q_blk * jnp.exp(b_blk - ref)).astype(BF16)
                kt = (k * jnp.exp(jnp.where(chunk_row < lo, ref - bcum, -jnp.inf))).astype(BF16)
                a_blk = lax.dot_general(qt, kt, nt, preferred_element_type=F32)
            else:
                a_blk = jnp.zeros((GLA_SUB, CHUNK), F32)
            for jj in range(GLA_SUB):
                dlt = jnp.where(sub_row >= jj, b_blk - bcum[lo + jj:lo + jj + 1], -jnp.inf)
                col = jnp.sum(q_blk * k[lo + jj:lo + jj + 1] * jnp.exp(dlt), axis=-1, keepdims=True)
                a_blk = a_blk + jnp.where(sub_lane == lo + jj, col, 0.0)
            o_blk = o_inter[lo:hi] + jnp.dot(a_blk.astype(BF16), v, preferred_element_type=F32)
            ms = jnp.mean(o_blk * o_blk, axis=-1, keepdims=True)
            rg = r_ref[pl.ds(r0 + lo, GLA_SUB), vcol].astype(F32)
            y = o_blk * lax.rsqrt(ms + EPS) * nw_ref[...] * (rg / (1.0 + jnp.exp(-rg)))
            o_ref[pl.ds(r0 + lo, GLA_SUB), vcol] = y.astype(o_ref.dtype)
        b_last = bcum[CHUNK - 1:CHUNK]
        k_dec = (k * jnp.exp(b_last - bcum)).astype(BF16)
        st_ref[hh] = st * jnp.exp(b_last) + lax.dot_general(v, k_dec, tn, preferred_element_type=F32)

    lax.fori_loop(0, n_chunks, chunk, 0)


def _gla(proj3, wup_pad, bup, nw):
    b, s, _ = proj3.shape
    kw = GLA_HEADS * GLA_DK
    vw = GLA_HEADS * GLA_DV
    q0 = 3 * DA_HEADS * DA_V // kw
    k0 = q0 + 1
    v0 = (k0 + 1) * kw // vw
    r0 = v0 + 1
    gd = (IN_COLS - GLA_RANK) // LANE
    return pl.pallas_call(
        functools.partial(_gla_kernel, n_chunks=s // CHUNK),
        grid=(b,),
        in_specs=[
            pl.BlockSpec((None, s, kw), lambda bi: (bi, 0, q0)),
            pl.BlockSpec((None, s, kw), lambda bi: (bi, 0, k0)),
            pl.BlockSpec((None, s, vw), lambda bi: (bi, 0, v0)),
            pl.BlockSpec((None, s, vw), lambda bi: (bi, 0, r0)),
            pl.BlockSpec((None, s, LANE), lambda bi: (bi, 0, gd)),
            pl.BlockSpec((LANE, kw), lambda bi: (0, 0)),
            pl.BlockSpec((1, kw), lambda bi: (0, 0)),
            pl.BlockSpec((1, GLA_DV), lambda bi: (0, 0)),
        ],
        out_specs=pl.BlockSpec((None, s, vw), lambda bi: (bi, 0, 0)),
        out_shape=jax.ShapeDtypeStruct((b, s, vw), BF16),
        scratch_shapes=[pltpu.VMEM((GLA_HEADS, GLA_DV, GLA_DK), F32)],
        compiler_params=_cparams(("parallel",)),
        name="gla",
    )(proj3, proj3, proj3, proj3, proj3, wup_pad, bup, nw)


def _out_router_kernel(x_ref, a_ref, b_ref, wo_ref, nw_ref, rwt_ref, rb_ref,
                       h_ref, hn_ref, slot_ref, w_ref, nch_ref):
    tm = x_ref.shape[0]
    half = a_ref.shape[1]
    h = (x_ref[...]
         + jnp.dot(a_ref[...], wo_ref[0:half, :], preferred_element_type=F32)
         + jnp.dot(b_ref[...], wo_ref[half:2 * half, :], preferred_element_type=F32))
    h_ref[...] = h
    ms = jnp.mean(h * h, axis=-1, keepdims=True)
    hn = h * lax.rsqrt(ms + EPS) * nw_ref[...]
    hn_hi = hn.astype(BF16)
    hn_ref[...] = hn_hi
    hn_lo = (hn - hn_hi.astype(F32)).astype(BF16)
    rwt = rwt_ref[...]
    rw_hi = rwt.astype(BF16)
    rw_lo = (rwt - rw_hi.astype(F32)).astype(BF16)
    nt = (((1,), (1,)), ((), ()))
    logits = (lax.dot_general(rw_hi, hn_hi, nt, preferred_element_type=F32)
              + lax.dot_general(rw_hi, hn_lo, nt, preferred_element_type=F32)
              + lax.dot_general(rw_lo, hn_hi, nt, preferred_element_type=F32)
              + rb_ref[...])
    e_iota = lax.broadcasted_iota(I32, (N_EXPERTS, tm), 0).astype(F32)
    vals, hots = [], []
    for _ in range(TOP_K):
        m = jnp.max(logits, axis=0, keepdims=True)
        idx = jnp.min(jnp.where(logits == m, e_iota, float(N_EXPERTS)), axis=0, keepdims=True)
        hot = e_iota == idx
        vals.append(m)
        hots.append(hot)
        logits = jnp.where(hot, -jnp.inf, logits)
    exps = [jnp.exp(v - vals[0]) for v in vals]
    denom = exps[0] + exps[1] + exps[2] + exps[3]
    for kk in range(TOP_K):
        w_ref[kk:kk + 1, :] = exps[kk] / denom
    sel = jnp.zeros((N_EXPERTS, tm), F32)
    for hot in hots:
        sel = sel + jnp.where(hot, 1.0, 0.0)
    tr = lax.broadcasted_iota(I32, (tm, tm), 0)
    tc = lax.broadcasted_iota(I32, (tm, tm), 1)
    upper = jnp.where(tr < tc, 1.0, 0.0).astype(BF16)
    rank = jnp.dot(sel.astype(BF16), upper, preferred_element_type=F32)
    cnt = jnp.sum(sel, axis=1, keepdims=True)
    nch = jnp.floor((cnt + (ROW_CHUNK - 1)) * (1.0 / ROW_CHUNK))
    nch_b = jnp.broadcast_to(nch, (N_EXPERTS, LANE))
    er = lax.broadcasted_iota(I32, (N_EXPERTS, N_EXPERTS), 0)
    ec = lax.broadcasted_iota(I32, (N_EXPERTS, N_EXPERTS), 1)
    lower = jnp.where(ec < er, 1.0, 0.0).astype(BF16)
    seg = jnp.dot(lower, nch_b.astype(BF16), preferred_element_type=F32)
    pos = seg[:, 0:1] * ROW_CHUNK + rank
    for kk in range(TOP_K):
        slot_ref[kk:kk + 1, :] = jnp.sum(jnp.where(hots[kk], pos, 0.0), axis=0, keepdims=True).astype(I32)
    nch_ref[...] = nch_b.astype(I32)


def _out_router(x2, out_a, out_b, wo_bf, nw, rwt, rb):
    t = x2.shape[0]
    tm = TOK_TILE
    half = out_a.shape[1]
    nt = t // tm
    return pl.pallas_call(
        _out_router_kernel,
        grid=(nt,),
        in_specs=[
            pl.BlockSpec((tm, D_MODEL), lambda i: (i, 0)),
            pl.BlockSpec((tm, half), lambda i: (i, 0)),
            pl.BlockSpec((tm, half), lambda i: (i, 0)),
            pl.BlockSpec((2 * half, D_MODEL), lambda i: (0, 0)),
            pl.BlockSpec((1, D_MODEL), lambda i: (0, 0)),
            pl.BlockSpec((N_EXPERTS, D_MODEL), lambda i: (0, 0)),
            pl.BlockSpec((N_EXPERTS, 1), lambda i: (0, 0)),
        ],
        out_specs=[
            pl.BlockSpec((tm, D_MODEL), lambda i: (i, 0)),
            pl.BlockSpec((tm, D_MODEL), lambda i: (i, 0)),
            pl.BlockSpec((TOP_K, tm), lambda i: (0, i)),
            pl.BlockSpec((TOP_K, tm), lambda i: (0, i)),
            pl.BlockSpec((N_EXPERTS, LANE), lambda i: (i, 0)),
        ],
        out_shape=[
            jax.ShapeDtypeStruct((t, D_MODEL), F32),
            jax.ShapeDtypeStruct((t, D_MODEL), BF16),
            jax.ShapeDtypeStruct((TOP_K, t), I32),
            jax.ShapeDtypeStruct((TOP_K, t), F32),
            jax.ShapeDtypeStruct((nt * N_EXPERTS, LANE), I32),
        ],
        compiler_params=_cparams(("parallel",)),
        name="out_router",
    )(x2, out_a, out_b, wo_bf, nw, rwt, rb)


def _routing_tables(nch, n_row_tiles):
    cpt = MOE_TM // ROW_CHUNK
    seg_src = jnp.cumsum(nch, axis=1) - nch
    rows_e = jnp.sum(nch, axis=0)
    tiles_e = (rows_e + cpt - 1) // cpt
    cum_tiles = jnp.cumsum(tiles_e)
    base_e = (cum_tiles - tiles_e) * cpt
    seg_dst = base_e[None, :] + jnp.cumsum(nch, axis=0) - nch
    gap_start = base_e + rows_e
    cps = MOE_SUB // ROW_CHUNK
    gap_len = (rows_e + cps - 1) // cps * cps - rows_e
    n_used = cum_tiles[-1]
    tile_ids = jnp.arange(n_row_tiles, dtype=I32)
    tile_e = jnp.sum((cum_tiles[None, :] <= jnp.minimum(tile_ids, n_used - 1)[:, None]).astype(I32), axis=1)
    tile_e = jnp.minimum(tile_e, N_EXPERTS - 1)
    first = jnp.concatenate([jnp.ones((1,), I32), (tile_e[1:] != tile_e[:-1]).astype(I32)])
    e_ids = jnp.arange(N_EXPERTS, dtype=I32)
    later = (e_ids[None, :] > e_ids[:, None]) & (tiles_e[None, :] > 0)
    next_e = jnp.min(jnp.where(later, e_ids[None, :], N_EXPERTS), axis=1)
    next_e = jnp.where(next_e == N_EXPERTS, -1, next_e)
    own = (tile_e[:, None] == e_ids[None, :]).astype(I32)
    pick = lambda per_expert: jnp.sum(own * per_expert[None, :], axis=1)
    tile_next = pick(next_e)
    tile_k = tile_ids - pick(cum_tiles - tiles_e)
    tile_rows = jnp.clip(pick(rows_e) * ROW_CHUNK - tile_k * MOE_TM, 0, MOE_TM)
    tot_gather = jnp.sum(nch, axis=1)
    is_last = jnp.arange(nch.shape[0], dtype=I32) == nch.shape[0] - 1
    tot_scatter = tot_gather + jnp.where(is_last, jnp.sum(gap_len), 0)
    flat = lambda a: a.reshape(-1).astype(I32)
    return (flat(seg_src), flat(seg_dst), flat(nch), flat(gap_start), flat(gap_len),
            flat(tot_scatter), flat(tot_gather),
            (tile_e.astype(I32), first, tile_next.astype(I32), tile_rows.astype(I32),
             n_used.reshape(1).astype(I32)))


def _chunk_rows(ref, chunk_idx):
    return ref.at[pl.ds(pl.multiple_of(chunk_idx * ROW_CHUNK, ROW_CHUNK), ROW_CHUNK), :]


def _compact_kernel(src_ref, dst_ref, nch_ref, gap_start_ref, gap_len_ref, tot_ref,
                    x_ref, slot_ref, w_ref, xs_ref, buf_ref, sems):
    tt = pl.program_id(0)
    n_tt = pl.num_programs(0)
    cur = tt & 1
    x = x_ref[...]
    s0, s1, s2, s3 = (slot_ref[kk:kk + 1, :] for kk in range(TOP_K))
    w0, w1, w2, w3 = (w_ref[kk:kk + 1, :] for kk in range(TOP_K))
    lane = lax.broadcasted_iota(I32, (SLOT_BLK, LANE), 1)
    for blk in range(SLOTS // SLOT_BLK):
        rows = slice(blk * SLOT_BLK, (blk + 1) * SLOT_BLK)
        sid = lax.broadcasted_iota(I32, (SLOT_BLK, TOK_TILE), 0) + blk * SLOT_BLK
        hit = jnp.where(sid == s0, 1.0, jnp.where(sid == s1, 1.0, jnp.where(sid == s2, 1.0,
                        jnp.where(sid == s3, 1.0, 0.0))))
        buf_ref[cur, rows, 0:D_MODEL] = jnp.dot(hit.astype(BF16), x, preferred_element_type=F32).astype(BF16)
        wsel = jnp.where(sid == s0, w0, jnp.where(sid == s1, w1, jnp.where(sid == s2, w2,
                         jnp.where(sid == s3, w3, 0.0))))
        wrow = jnp.sum(wsel, axis=1, keepdims=True)
        t0 = wrow.astype(BF16).astype(F32)
        t1 = (wrow - t0).astype(BF16).astype(F32)
        t2 = wrow - t0 - t1
        buf_ref[cur, rows, D_MODEL:XS_COLS] = jnp.where(
            lane == 0, t0, jnp.where(lane == 1, t1, jnp.where(lane == 2, t2, 0.0))).astype(BF16)

    def copy(slot, src_chunk, dst_chunk):
        return pltpu.make_async_copy(_chunk_rows(buf_ref.at[slot], src_chunk), _chunk_rows(xs_ref, dst_chunk),
                                     sems.at[slot])

    def per_expert(e, _):
        n = nch_ref[tt * N_EXPERTS + e]
        src0 = src_ref[tt * N_EXPERTS + e]
        dst0 = dst_ref[tt * N_EXPERTS + e]

        def issue(kk, _):
            copy(cur, src0 + kk, dst0 + kk).start()
            return 0

        lax.fori_loop(0, n, issue, 0)
        return 0

    lax.fori_loop(0, N_EXPERTS, per_expert, 0)

    zero_chunk = SLOTS // ROW_CHUNK - 1

    def per_gap(e, _):
        n = jnp.where(tt == n_tt - 1, gap_len_ref[e], 0)
        g0 = gap_start_ref[e]

        def issue(kk, _):
            copy(cur, zero_chunk, g0 + kk).start()
            return 0

        lax.fori_loop(0, n, issue, 0)
        return 0

    lax.fori_loop(0, N_EXPERTS, per_gap, 0)

    def drain(slot, count):
        def one(_, c):
            copy(slot, 0, 0).wait()
            return c

        lax.fori_loop(0, count, one, 0)

    @pl.when(tt > 0)
    def _():
        drain(1 - cur, tot_ref[tt - 1])

    @pl.when(tt == n_tt - 1)
    def _():
        drain(cur, tot_ref[tt])


def _compact(tables, hn2, slots, top_w, n_rows):
    seg_src, seg_dst, nch, gap_start, gap_len, tot = tables
    t = hn2.shape[0]
    return pl.pallas_call(
        _compact_kernel,
        grid_spec=pltpu.PrefetchScalarGridSpec(
            num_scalar_prefetch=6,
            grid=(t // TOK_TILE,),
            in_specs=[
                pl.BlockSpec((TOK_TILE, D_MODEL), lambda i, *_: (i, 0)),
                pl.BlockSpec((TOP_K, TOK_TILE), lambda i, *_: (0, i)),
                pl.BlockSpec((TOP_K, TOK_TILE), lambda i, *_: (0, i)),
            ],
            out_specs=pl.BlockSpec(memory_space=pl.ANY),
            scratch_shapes=[pltpu.VMEM((2, SLOTS, XS_COLS), BF16), pltpu.SemaphoreType.DMA((2,))],
        ),
        out_shape=jax.ShapeDtypeStruct((n_rows, XS_COLS), BF16),
        compiler_params=_cparams(("arbitrary",)),
        name="compact",
    )(seg_src, seg_dst, nch, gap_start, gap_len, tot, hn2, slots, top_w)


def _per_fill(rows, compute):
    for n in range(MOE_SUB, MOE_TM + 1, MOE_SUB):
        @pl.when((rows > n - MOE_SUB) & (rows <= n))
        def _():
            compute(n)


def _moe_up_kernel(te_ref, first_ref, next_ref, rows_ref, nu_ref, x_ref, wg_hbm, wu_hbm, bg_ref, bu_ref, h_ref,
                   wg_st, wu_st, wg_bf, wu_bf, sems):
    c = pl.program_id(0)
    i = pl.program_id(1)

    def fetch(e, cc):
        col = pl.ds(pl.multiple_of(cc * MOE_FC, MOE_FC), MOE_FC)
        return (pltpu.make_async_copy(wg_hbm.at[e, :, col], wg_st, sems.at[0]),
                pltpu.make_async_copy(wu_hbm.at[e, :, col], wu_st, sems.at[1]))

    @pl.when((c == 0) & (i == 0))
    def _():
        for cp in fetch(te_ref[0], 0):
            cp.start()

    @pl.when(i < nu_ref[0])
    def _():
        @pl.when(first_ref[i] == 1)
        def _():
            for cp in fetch(te_ref[i], c):
                cp.wait()
            wg_bf[...] = wg_st[...].astype(BF16)
            wu_bf[...] = wu_st[...].astype(BF16)
            nxt = next_ref[i]

            @pl.when(nxt >= 0)
            def _():
                for cp in fetch(nxt, c):
                    cp.start()

            @pl.when((nxt < 0) & (c + 1 < pl.num_programs(0)))
            def _():
                for cp in fetch(te_ref[0], c + 1):
                    cp.start()

        def compute(n):
            x = x_ref[0:n, :]
            gate = jnp.minimum(jnp.dot(x, wg_bf[...], preferred_element_type=F32) + bg_ref[...], SWIGLU_LIMIT)
            up = jnp.clip(jnp.dot(x, wu_bf[...], preferred_element_type=F32) + bu_ref[...],
                          -SWIGLU_LIMIT, SWIGLU_LIMIT)
            act = (up + 1.0) * gate * (1.0 / (1.0 + jnp.exp(-SWIGLU_ALPHA * gate)))
            h_ref[0:n, :] = act.astype(h_ref.dtype)

        _per_fill(rows_ref[i], compute)


def _moe_up(tiles, xs, wg, wu, bg, bu):
    n_rows = xs.shape[0]
    n_tiles = n_rows // MOE_TM
    row = lambda c, i, te, fi, nx, rw, nu: (jnp.minimum(i, nu[0] - 1), 0)
    bsel = lambda c, i, te, fi, nx, rw, nu: (te[i], 0, c)
    return pl.pallas_call(
        _moe_up_kernel,
        grid_spec=pltpu.PrefetchScalarGridSpec(
            num_scalar_prefetch=5,
            grid=(D_FF // MOE_FC, n_tiles),
            in_specs=[
                pl.BlockSpec((MOE_TM, D_MODEL), row),
                pl.BlockSpec(memory_space=pl.ANY),
                pl.BlockSpec(memory_space=pl.ANY),
                pl.BlockSpec((None, 1, MOE_FC), bsel),
                pl.BlockSpec((None, 1, MOE_FC), bsel),
            ],
            out_specs=pl.BlockSpec((MOE_TM, MOE_FC),
                                   lambda c, i, te, fi, nx, rw, nu: (jnp.minimum(i, nu[0] - 1), c)),
            scratch_shapes=[pltpu.VMEM((D_MODEL, MOE_FC), F32), pltpu.VMEM((D_MODEL, MOE_FC), F32),
                            pltpu.VMEM((D_MODEL, MOE_FC), BF16), pltpu.VMEM((D_MODEL, MOE_FC), BF16),
                            pltpu.SemaphoreType.DMA((2,))],
        ),
        out_shape=jax.ShapeDtypeStruct((n_rows, D_FF), BF16),
        compiler_params=_cparams(("arbitrary", "arbitrary")),
        name="moe_up",
    )(*tiles, xs, wg, wu, bg, bu)


def _moe_down_kernel(te_ref, first_ref, next_ref, rows_ref, nu_ref, h_ref, cw_ref, wd_hbm, bd_ref, y_ref,
                     wd_st, wd_bf, sem):
    i = pl.program_id(0)

    def fetch(e):
        return pltpu.make_async_copy(wd_hbm.at[e], wd_st, sem)

    @pl.when(i == 0)
    def _():
        fetch(te_ref[0]).start()

    @pl.when(i < nu_ref[0])
    def _():
        @pl.when(first_ref[i] == 1)
        def _():
            fetch(te_ref[i]).wait()
            wd_bf[...] = wd_st[...].astype(BF16)
            nxt = next_ref[i]

            @pl.when(nxt >= 0)
            def _():
                fetch(nxt).start()

        def compute(n):
            cw = cw_ref[0:n, :].astype(F32)
            cw = cw[:, 0:1] + cw[:, 1:2] + cw[:, 2:3]
            y = (jnp.dot(h_ref[0:n, :], wd_bf[...], preferred_element_type=F32) + bd_ref[...]) * cw
            y_ref[0:n, :] = y.astype(y_ref.dtype)

        _per_fill(rows_ref[i], compute)


def _moe_down(tiles, hs, xs, wd, bd):
    n_rows = hs.shape[0]
    n_tiles = n_rows // MOE_TM
    row = lambda i, te, fi, nx, rw, nu: (jnp.minimum(i, nu[0] - 1), 0)
    return pl.pallas_call(
        _moe_down_kernel,
        grid_spec=pltpu.PrefetchScalarGridSpec(
            num_scalar_prefetch=5,
            grid=(n_tiles,),
            in_specs=[
                pl.BlockSpec((MOE_TM, D_FF), row),
                pl.BlockSpec((MOE_TM, LANE),
                             lambda i, te, fi, nx, rw, nu: (jnp.minimum(i, nu[0] - 1), D_MODEL // LANE)),
                pl.BlockSpec(memory_space=pl.ANY),
                pl.BlockSpec((None, 1, D_MODEL), lambda i, te, fi, nx, rw, nu: (te[i], 0, 0)),
            ],
            out_specs=pl.BlockSpec((MOE_TM, D_MODEL), row),
            scratch_shapes=[pltpu.VMEM((D_FF, D_MODEL), F32), pltpu.VMEM((D_FF, D_MODEL), BF16),
                            pltpu.SemaphoreType.DMA(())],
        ),
        out_shape=jax.ShapeDtypeStruct((n_rows, D_MODEL), BF16),
        compiler_params=_cparams(("arbitrary",)),
        name="moe_down",
    )(*tiles, hs, xs, wd, bd)


def _combine_kernel(src_ref, dst_ref, nch_ref, tot_ref, h_ref, slot_ref, ys_ref, o_ref, buf_ref, sems):
    tt = pl.program_id(0)
    n_tt = pl.num_programs(0)
    cur = tt & 1

    def copy(slot, sorted_chunk, local_chunk):
        return pltpu.make_async_copy(_chunk_rows(ys_ref, sorted_chunk), _chunk_rows(buf_ref.at[slot], local_chunk),
                                     sems.at[slot])

    def gather(tile, slot):
        def per_expert(e, _):
            n = nch_ref[tile * N_EXPERTS + e]
            loc0 = src_ref[tile * N_EXPERTS + e]
            srt0 = dst_ref[tile * N_EXPERTS + e]

            def issue(kk, _):
                copy(slot, srt0 + kk, loc0 + kk).start()
                return 0

            lax.fori_loop(0, n, issue, 0)
            return 0

        lax.fori_loop(0, N_EXPERTS, per_expert, 0)

    @pl.when(tt == 0)
    def _():
        buf_ref[...] = jnp.zeros_like(buf_ref)
        gather(0, 0)

    @pl.when(tt + 1 < n_tt)
    def _():
        gather(tt + 1, 1 - cur)

    def one(_, c):
        copy(cur, 0, 0).wait()
        return c

    lax.fori_loop(0, tot_ref[tt], one, 0)

    o_ref[...] = h_ref[...]
    for blk in range(SLOTS // SLOT_BLK):
        sid = lax.broadcasted_iota(I32, (TOK_TILE, SLOT_BLK), 1) + blk * SLOT_BLK
        hit = jnp.zeros((TOK_TILE, SLOT_BLK), F32)
        for kk in range(TOP_K):
            hit = jnp.where(sid == slot_ref[:, kk:kk + 1], 1.0, hit)
        yb = buf_ref[cur, blk * SLOT_BLK:(blk + 1) * SLOT_BLK, :]
        o_ref[...] += jnp.dot(hit.astype(BF16), yb, preferred_element_type=F32)


def _combine(tables, h, slots_t, ys):
    seg_src, seg_dst, nch, tot = tables
    t = h.shape[0]
    return pl.pallas_call(
        _combine_kernel,
        grid_spec=pltpu.PrefetchScalarGridSpec(
            num_scalar_prefetch=4,
            grid=(t // TOK_TILE,),
            in_specs=[
                pl.BlockSpec((TOK_TILE, D_MODEL), lambda i, *_: (i, 0)),
                pl.BlockSpec((TOK_TILE, TOP_K), lambda i, *_: (i, 0)),
                pl.BlockSpec(memory_space=pl.ANY),
            ],
            out_specs=pl.BlockSpec((TOK_TILE, D_MODEL), lambda i, *_: (i, 0)),
            scratch_shapes=[pltpu.VMEM((2, SLOTS, D_MODEL), BF16), pltpu.SemaphoreType.DMA((2,))],
        ),
        out_shape=jax.ShapeDtypeStruct((t, D_MODEL), F32),
        compiler_params=_cparams(("arbitrary",)),
        name="combine",
    )(seg_src, seg_dst, nch, tot, h, slots_t, ys)


def kernel(x, attn_norm_w, w_in, da_q_norm_w, da_k_norm_w, da_lambda_q1, da_lambda_k1, da_lambda_q2,
           da_lambda_k2, da_subln_w, gla_gate_up_w, gla_gate_up_b, gla_norm_w, w_out, ffn_norm_w,
           router_w, router_b, exp_w_gate, exp_b_gate, exp_w_up, exp_b_up, exp_w_down, exp_b_down):
    b, s, d = x.shape
    t = b * s
    assert d == D_MODEL and t % TOK_TILE == 0 and s % CHUNK == 0 and attn_norm_w.shape[0] == 1
    x2 = x.reshape(t, d)

    w_in_bf = jnp.concatenate([w_in[0].astype(BF16), jnp.zeros((D_MODEL, IN_COLS_PAD - IN_COLS), BF16)], axis=1)
    proj = _in_proj(x2, attn_norm_w, w_in_bf)
    proj3 = proj.reshape(b, s, IN_COLS_PAD)

    out_a = _diff_attn(proj3, jnp.tile(da_q_norm_w, (1, 2 * ATT_HP)), jnp.tile(da_k_norm_w, (1, 2 * ATT_HP)),
                       da_lambda_q1, da_lambda_k1, da_lambda_q2, da_lambda_k2, da_subln_w)
    wup_pad = jnp.pad(gla_gate_up_w[0].astype(BF16), ((0, LANE - GLA_RANK), (0, 0)))
    out_b = _gla(proj3, wup_pad, gla_gate_up_b, gla_norm_w)

    h, hn2, slots, top_w, nch_b = _out_router(
        x2, out_a.reshape(t, -1), out_b.reshape(t, -1), w_out[0].astype(BF16), ffn_norm_w,
        router_w[0].T, router_b.reshape(N_EXPERTS, 1))

    n_tok_tiles = t // TOK_TILE
    nch = nch_b[:, 0].reshape(n_tok_tiles, N_EXPERTS)
    n_row_tiles = (TOP_K * t + n_tok_tiles * N_EXPERTS * (ROW_CHUNK - 1)) // MOE_TM + N_EXPERTS
    seg_src, seg_dst, nch_f, gap_start, gap_len, tot_scatter, tot_gather, tiles = _routing_tables(nch, n_row_tiles)

    xs = _compact((seg_src, seg_dst, nch_f, gap_start, gap_len, tot_scatter), hn2, slots, top_w,
                  n_row_tiles * MOE_TM)
    hs = _moe_up(tiles, xs,
                 exp_w_gate.reshape(N_EXPERTS, D_MODEL, D_FF), exp_w_up.reshape(N_EXPERTS, D_MODEL, D_FF),
                 exp_b_gate.reshape(N_EXPERTS, 1, D_FF), exp_b_up.reshape(N_EXPERTS, 1, D_FF))
    ys = _moe_down(tiles, hs, xs, exp_w_down.reshape(N_EXPERTS, D_FF, D_MODEL),
                   exp_b_down.reshape(N_EXPERTS, 1, D_MODEL))
    out = _combine((seg_src, seg_dst, nch_f, tot_gather), h, slots.T, ys)
    return out.reshape(b, s, d)
```

```python
import functools
import math

import jax
import jax.numpy as jnp
from jax import lax
from jax.experimental import pallas as pl
from jax.experimental.pallas import tpu as pltpu

F32 = jnp.float32
BF16 = jnp.bfloat16
I32 = jnp.int32

D_MODEL = 2048
CHUNK = 64
EPS = 1e-6
DA_HEADS = 8
DA_QK = 64
DA_V = 128
GLA_HEADS = 4
GLA_DK = 128
GLA_DV = 256
GLA_RANK = 16
GLA_TAU = 16.0
N_EXPERTS = 32
TOP_K = 4
D_FF = 2048
SWIGLU_LIMIT = 7.0
SWIGLU_ALPHA = 1.702
LAMBDA_INIT = 0.8 - 0.6 * math.exp(-0.3 * 0)

IN_COLS = 6160
IN_COLS_MAIN = IN_COLS - GLA_RANK
LANE = 128
ROW_CHUNK = 8

IN_TM = 1024
IN_TN = 1024
ATT_BQ = 256
ATT_HP = 4
TOK_TILE = 256
SLOTS = TOP_K * TOK_TILE + N_EXPERTS * ROW_CHUNK
SLOT_BLK = 256
XS_COLS = D_MODEL + LANE
MOE_TM = 512
MOE_SUB = 128
MOE_FC = 1024
VMEM_LIMIT = 56 * 1024 * 1024


def _cparams(sem):
    return pltpu.CompilerParams(dimension_semantics=sem, vmem_limit_bytes=VMEM_LIMIT)


def _rms_gd_kernel(x_ref, nw_ref, wt_ref, hn_ref, gd_ref):
    x = x_ref[...]
    ms = jnp.mean(x * x, axis=-1, keepdims=True)
    hn = (x * lax.rsqrt(ms + EPS) * nw_ref[...]).astype(BF16)
    hn_ref[...] = hn
    gd_ref[...] = jnp.dot(hn, wt_ref[...], preferred_element_type=F32).astype(gd_ref.dtype)


def _rms_gd(x2, norm_w, w_tail):
    t = x2.shape[0]
    tm = min(IN_TM, t)
    return pl.pallas_call(
        _rms_gd_kernel,
        grid=(t // tm,),
        in_specs=[
            pl.BlockSpec((tm, D_MODEL), lambda i: (i, 0)),
            pl.BlockSpec((1, D_MODEL), lambda i: (0, 0)),
            pl.BlockSpec((D_MODEL, LANE), lambda i: (0, 0)),
        ],
        out_specs=[pl.BlockSpec((tm, D_MODEL), lambda i: (i, 0)), pl.BlockSpec((tm, LANE), lambda i: (i, 0))],
        out_shape=[jax.ShapeDtypeStruct((t, D_MODEL), BF16), jax.ShapeDtypeStruct((t, LANE), BF16)],
        compiler_params=_cparams(("parallel",)),
        name="rms_gd",
    )(x2, norm_w, w_tail)


def _in_proj_kernel(hn_ref, w_ref, o_ref, w_bf):
    @pl.when(pl.program_id(1) == 0)
    def _():
        w_bf[...] = w_ref[...].astype(BF16)

    o_ref[...] = jnp.dot(hn_ref[...], w_bf[...], preferred_element_type=F32).astype(o_ref.dtype)


def _in_proj(hn, w_in):
    t = hn.shape[0]
    tm = min(IN_TM, t)
    return pl.pallas_call(
        _in_proj_kernel,
        grid=(IN_COLS_MAIN // IN_TN, t // tm),
        in_specs=[
            pl.BlockSpec((tm, D_MODEL), lambda j, i: (i, 0)),
            pl.BlockSpec((None, D_MODEL, IN_TN), lambda j, i: (0, 0, j)),
        ],
        out_specs=pl.BlockSpec((tm, IN_TN), lambda j, i: (i, j)),
        out_shape=jax.ShapeDtypeStruct((t, IN_COLS_MAIN), BF16),
        scratch_shapes=[pltpu.VMEM((D_MODEL, IN_TN), BF16)],
        compiler_params=_cparams(("arbitrary", "arbitrary")),
        name="in_proj",
    )(hn, w_in)


def _attn_kernel(q_ref, k_ref, v_ref, qw_ref, kw_ref, lq1_ref, lk1_ref, lq2_ref, lk2_ref, sw_ref,
                 o_ref, kn_ref, vt_ref, *, bq):
    qi = pl.program_id(2)
    first_map = lax.broadcasted_iota(I32, (1, LANE), 1) < DA_QK
    n_grp = 2 * bq // LANE
    shift = CHUNK.bit_length() - 1

    def qk_norm(x, w):
        x2 = x * x
        scale = []
        for hh in range(ATT_HP):
            xh = x2[:, hh * LANE:(hh + 1) * LANE]
            s1 = jnp.sum(jnp.where(first_map, xh, 0.0), axis=-1, keepdims=True)
            s2 = jnp.sum(jnp.where(first_map, 0.0, xh), axis=-1, keepdims=True)
            scale.append(jnp.where(first_map, lax.rsqrt(s1 / DA_QK + EPS), lax.rsqrt(s2 / DA_QK + EPS)))
        return x * jnp.concatenate(scale, axis=1) * w

    @pl.when(qi == 0)
    def _():
        kn_ref[...] = qk_norm(k_ref[...].astype(F32), kw_ref[...]).astype(BF16)
        vt_ref[...] = v_ref[...].astype(F32).T.astype(BF16)

    q = qk_norm(q_ref[...].astype(F32), qw_ref[...]) * (DA_QK ** -0.5 * math.log2(math.e))
    q_t = []
    for hh in range(ATT_HP):
        qh = q[:, hh * LANE:(hh + 1) * LANE]
        q_t.append(jnp.concatenate([jnp.where(first_map, qh, 0.0), jnp.where(first_map, 0.0, qh)],
                                   axis=0).T.astype(BF16))

    def step(j, carry, masked):
        r0 = pl.multiple_of(j * bq, bq)
        scores = [jnp.dot(kn_ref[pl.ds(r0, bq), hh * LANE:(hh + 1) * LANE], q_t[hh], preferred_element_type=F32)
                  for hh in range(ATT_HP)]
        out = []
        for hh in range(ATT_HP):
            ms, ls, alphas, ps = [], [], [], []
            for g in range(n_grp):
                m, l, _ = carry[hh][g]
                s = scores[hh][:, g * LANE:(g + 1) * LANE]
                if masked:
                    key_chunk = lax.broadcasted_iota(I32, (bq, LANE), 0) >> shift
                    qry_chunk = (lax.broadcasted_iota(I32, (bq, LANE), 1) + (g * LANE) % bq) >> shift
                    s = jnp.where(key_chunk <= qry_chunk, s, -jnp.inf)
                m_new = jnp.maximum(m, jnp.max(s, axis=0, keepdims=True))
                alpha = jnp.exp2(m - m_new)
                p = jnp.exp2(s - m_new)
                ms.append(m_new)
                ls.append(alpha * l + jnp.sum(p, axis=0, keepdims=True))
                alphas.append(alpha)
                ps.append(p.astype(BF16))
            pv = jnp.dot(vt_ref[hh * DA_V:(hh + 1) * DA_V, pl.ds(r0, bq)], jnp.concatenate(ps, axis=1),
                         preferred_element_type=F32)
            out.append(tuple((ms[g], ls[g], alphas[g] * carry[hh][g][2] + pv[:, g * LANE:(g + 1) * LANE])
                             for g in range(n_grp)))
        return tuple(out)

    init = tuple(tuple((jnp.full((1, LANE), -jnp.inf, F32), jnp.zeros((1, LANE), F32), jnp.zeros((DA_V, LANE), F32))
                       for _ in range(n_grp)) for _ in range(ATT_HP))
    carry = lax.fori_loop(0, qi, lambda j, c: step(j, c, False), init)
    carry = step(qi, carry, True)
    lam = (jnp.exp(jnp.sum(lq1_ref[...] * lk1_ref[...], axis=-1, keepdims=True))
           - jnp.exp(jnp.sum(lq2_ref[...] * lk2_ref[...], axis=-1, keepdims=True)) + LAMBDA_INIT)
    half = n_grp // 2
    for hh in range(ATT_HP):
        for g in range(half):
            o1 = carry[hh][g][2] / carry[hh][g][1]
            o2 = carry[hh][g + half][2] / carry[hh][g + half][1]
            out = o1 - lam * o2
            ms = jnp.mean(out * out, axis=0, keepdims=True)
            y = out * lax.rsqrt(ms + EPS) * sw_ref[...] * (1.0 - LAMBDA_INIT)
            o_ref[g * LANE:(g + 1) * LANE, hh * DA_V:(hh + 1) * DA_V] = y.T.astype(o_ref.dtype)


def _diff_attn(proj3, qw, kw, lq1, lk1, lq2, lk2, sw):
    b, s, _ = proj3.shape
    bq = min(ATT_BQ, s)
    hw = ATT_HP * LANE
    n_hg = DA_HEADS // ATT_HP
    vec = lambda n: pl.BlockSpec((1, n), lambda bi, h, qi: (0, 0))
    return pl.pallas_call(
        functools.partial(_attn_kernel, bq=bq),
        grid=(b, n_hg, s // bq),
        in_specs=[
            pl.BlockSpec((None, bq, hw), lambda bi, h, qi: (bi, qi, h)),
            pl.BlockSpec((None, s, hw), lambda bi, h, qi: (bi, 0, n_hg + h)),
            pl.BlockSpec((None, s, hw), lambda bi, h, qi: (bi, 0, 2 * n_hg + h)),
            vec(hw), vec(hw), vec(DA_QK), vec(DA_QK), vec(DA_QK), vec(DA_QK),
            pl.BlockSpec((DA_V, 1), lambda bi, h, qi: (0, 0)),
        ],
        out_specs=pl.BlockSpec((None, bq, hw), lambda bi, h, qi: (bi, qi, h)),
        out_shape=jax.ShapeDtypeStruct((b, s, DA_HEADS * DA_V), BF16),
        scratch_shapes=[pltpu.VMEM((s, hw), BF16), pltpu.VMEM((hw, s), BF16)],
        compiler_params=_cparams(("parallel", "parallel", "arbitrary")),
        name="diff_attn",
    )(proj3, proj3, proj3, qw, kw, lq1, lk1, lq2, lk2, sw.reshape(DA_V, 1))


GLA_SUB = 16


def _gla_kernel(q_ref, k_ref, v_ref, r_ref, gd_ref, wup_ref, bup_ref, nw_ref, o_ref, st_ref, *, n_chunks):
    st_ref[...] = jnp.zeros_like(st_ref)
    rr = lax.broadcasted_iota(I32, (CHUNK, CHUNK), 0)
    cc = lax.broadcasted_iota(I32, (CHUNK, CHUNK), 1)
    tril = jnp.where(cc <= rr, 1.0, 0.0).astype(BF16)
    sub_row = lax.broadcasted_iota(I32, (GLA_SUB, GLA_DK), 0)
    sub_lane = lax.broadcasted_iota(I32, (GLA_SUB, CHUNK), 1)
    chunk_row = lax.broadcasted_iota(I32, (CHUNK, GLA_DK), 0)
    nt = (((1,), (1,)), ((), ()))
    tn = (((0,), (0,)), ((), ()))

    def chunk(c, _):
        r0 = pl.multiple_of(c * CHUNK, CHUNK)
        z_all = jnp.dot(gd_ref[pl.ds(r0, CHUNK), :], wup_ref[...], preferred_element_type=F32) + bup_ref[...]
        for hh in range(GLA_HEADS):
            head_chunk(r0, hh, z_all[:, hh * GLA_DK:(hh + 1) * GLA_DK])
        return 0

    def head_chunk(r0, hh, z):
        kcol = slice(hh * GLA_DK, (hh + 1) * GLA_DK)
        vcol = slice(hh * GLA_DV, (hh + 1) * GLA_DV)
        q = q_ref[pl.ds(r0, CHUNK), kcol].astype(F32) * (GLA_DK ** -0.5)
        k = k_ref[pl.ds(r0, CHUNK), kcol].astype(F32)
        v = v_ref[pl.ds(r0, CHUNK), vcol]
        lg = (jnp.minimum(z, 0.0) - jnp.log(1.0 + jnp.exp(-jnp.abs(z)))) / GLA_TAU
        lg_hi = lg.astype(BF16)
        lg_lo = (lg - lg_hi.astype(F32)).astype(BF16)
        bcum = (jnp.dot(tril, lg_hi, preferred_element_type=F32)
                + jnp.dot(tril, lg_lo, preferred_element_type=F32))
        bprev = bcum - lg
        st = st_ref[hh]
        o_inter = lax.dot_general((q * jnp.exp(bcum)).astype(BF16), st.astype(BF16), nt,
                                  preferred_element_type=F32)
        for blk in range(CHUNK // GLA_SUB):
            lo = blk * GLA_SUB
            hi = lo + GLA_SUB
            b_blk = bcum[lo:hi]
            q_blk = q[lo:hi]
            if blk > 0:
                ref = bprev[lo:lo + 1]
                qt = (q_blk * jnp.exp(b_blk - ref)).astype(BF16)
                kt = (k * jnp.exp(jnp.where(chunk_row < lo, ref - bcum, -jnp.inf))).astype(BF16)
                a_blk = lax.dot_general(qt, kt, nt, preferred_element_type=F32)
            else:
                a_blk = jnp.zeros((GLA_SUB, CHUNK), F32)
            for jj in range(GLA_SUB):
                dlt = jnp.where(sub_row >= jj, b_blk - bcum[lo + jj:lo + jj + 1], -jnp.inf)
                col = jnp.sum(q_blk * k[lo + jj:lo + jj + 1] * jnp.exp(dlt), axis=-1, keepdims=True)
                a_blk = a_blk + jnp.where(sub_lane == lo + jj, col, 0.0)
            o_blk = o_inter[lo:hi] + jnp.dot(a_blk.astype(BF16), v, preferred_element_type=F32)
            ms = jnp.mean(o_blk * o_blk, axis=-1, keepdims=True)
            rg = r_ref[pl.ds(r0 + lo, GLA_SUB), vcol].astype(F32)
            y = o_blk * lax.rsqrt(ms + EPS) * nw_ref[...] * (rg / (1.0 + jnp.exp(-rg)))
            o_ref[pl.ds(r0 + lo, GLA_SUB), vcol] = y.astype(o_ref.dtype)
        b_last = bcum[CHUNK - 1:CHUNK]
        k_dec = (k * jnp.exp(b_last - bcum)).astype(BF16)
        st_ref[hh] = st * jnp.exp(b_last) + lax.dot_general(v, k_dec, tn, preferred_element_type=F32)

    lax.fori_loop(0, n_chunks, chunk, 0)


def _gla(proj3, gd3, wup_pad, bup, nw):
    b, s, _ = proj3.shape
    kw = GLA_HEADS * GLA_DK
    vw = GLA_HEADS * GLA_DV
    q0 = 3 * DA_HEADS * DA_V // kw
    k0 = q0 + 1
    v0 = (k0 + 1) * kw // vw
    r0 = v0 + 1
    return pl.pallas_call(
        functools.partial(_gla_kernel, n_chunks=s // CHUNK),
        grid=(b,),
        in_specs=[
            pl.BlockSpec((None, s, kw), lambda bi: (bi, 0, q0)),
            pl.BlockSpec((None, s, kw), lambda bi: (bi, 0, k0)),
            pl.BlockSpec((None, s, vw), lambda bi: (bi, 0, v0)),
            pl.BlockSpec((None, s, vw), lambda bi: (bi, 0, r0)),
            pl.BlockSpec((None, s, LANE), lambda bi: (bi, 0, 0)),
            pl.BlockSpec((LANE, kw), lambda bi: (0, 0)),
            pl.BlockSpec((1, kw), lambda bi: (0, 0)),
            pl.BlockSpec((1, GLA_DV), lambda bi: (0, 0)),
        ],
        out_specs=pl.BlockSpec((None, s, vw), lambda bi: (bi, 0, 0)),
        out_shape=jax.ShapeDtypeStruct((b, s, vw), BF16),
        scratch_shapes=[pltpu.VMEM((GLA_HEADS, GLA_DV, GLA_DK), F32)],
        compiler_params=_cparams(("parallel",)),
        name="gla",
    )(proj3, proj3, proj3, proj3, gd3, wup_pad, bup, nw)


def _out_router_kernel(x_ref, a_ref, b_ref, wo_ref, nw_ref, rwt_ref, rb_ref,
                       h_ref, hn_ref, slot_ref, w_ref, nch_ref):
    tm = x_ref.shape[0]
    half = a_ref.shape[1]
    h = (x_ref[...]
         + jnp.dot(a_ref[...], wo_ref[0:half, :], preferred_element_type=F32)
         + jnp.dot(b_ref[...], wo_ref[half:2 * half, :], preferred_element_type=F32))
    h_ref[...] = h
    ms = jnp.mean(h * h, axis=-1, keepdims=True)
    hn = h * lax.rsqrt(ms + EPS) * nw_ref[...]
    hn_hi = hn.astype(BF16)
    hn_ref[...] = hn_hi
    hn_lo = (hn - hn_hi.astype(F32)).astype(BF16)
    rwt = rwt_ref[...]
    rw_hi = rwt.astype(BF16)
    rw_lo = (rwt - rw_hi.astype(F32)).astype(BF16)
    nt = (((1,), (1,)), ((), ()))
    logits = (lax.dot_general(rw_hi, hn_hi, nt, preferred_element_type=F32)
              + lax.dot_general(rw_hi, hn_lo, nt, preferred_element_type=F32)
              + lax.dot_general(rw_lo, hn_hi, nt, preferred_element_type=F32)
              + rb_ref[...])
    e_iota = lax.broadcasted_iota(I32, (N_EXPERTS, tm), 0).astype(F32)
    vals, hots = [], []
    for _ in range(TOP_K):
        m = jnp.max(logits, axis=0, keepdims=True)
        idx = jnp.min(jnp.where(logits == m, e_iota, float(N_EXPERTS)), axis=0, keepdims=True)
        hot = e_iota == idx
        vals.append(m)
        hots.append(hot)
        logits = jnp.where(hot, -jnp.inf, logits)
    exps = [jnp.exp(v - vals[0]) for v in vals]
    denom = exps[0] + exps[1] + exps[2] + exps[3]
    for kk in range(TOP_K):
        w_ref[kk:kk + 1, :] = exps[kk] / denom
    sel = jnp.zeros((N_EXPERTS, tm), F32)
    for hot in hots:
        sel = sel + jnp.where(hot, 1.0, 0.0)
    tr = lax.broadcasted_iota(I32, (tm, tm), 0)
    tc = lax.broadcasted_iota(I32, (tm, tm), 1)
    upper = jnp.where(tr < tc, 1.0, 0.0).astype(BF16)
    rank = jnp.dot(sel.astype(BF16), upper, preferred_element_type=F32)
    cnt = jnp.sum(sel, axis=1, keepdims=True)
    nch = jnp.floor((cnt + (ROW_CHUNK - 1)) * (1.0 / ROW_CHUNK))
    nch_b = jnp.broadcast_to(nch, (N_EXPERTS, LANE))
    er = lax.broadcasted_iota(I32, (N_EXPERTS, N_EXPERTS), 0)
    ec = lax.broadcasted_iota(I32, (N_EXPERTS, N_EXPERTS), 1)
    lower = jnp.where(ec < er, 1.0, 0.0).astype(BF16)
    seg = jnp.dot(lower, nch_b.astype(BF16), preferred_element_type=F32)
    pos = seg[:, 0:1] * ROW_CHUNK + rank
    for kk in range(TOP_K):
        slot_ref[kk:kk + 1, :] = jnp.sum(jnp.where(hots[kk], pos, 0.0), axis=0, keepdims=True).astype(I32)
    nch_ref[...] = nch_b.astype(I32)


def _out_router(x2, out_a, out_b, wo_bf, nw, rwt, rb):
    t = x2.shape[0]
    tm = TOK_TILE
    half = out_a.shape[1]
    nt = t // tm
    return pl.pallas_call(
        _out_router_kernel,
        grid=(nt,),
        in_specs=[
            pl.BlockSpec((tm, D_MODEL), lambda i: (i, 0)),
            pl.BlockSpec((tm, half), lambda i: (i, 0)),
            pl.BlockSpec((tm, half), lambda i: (i, 0)),
            pl.BlockSpec((2 * half, D_MODEL), lambda i: (0, 0)),
            pl.BlockSpec((1, D_MODEL), lambda i: (0, 0)),
            pl.BlockSpec((N_EXPERTS, D_MODEL), lambda i: (0, 0)),
            pl.BlockSpec((N_EXPERTS, 1), lambda i: (0, 0)),
        ],
        out_specs=[
            pl.BlockSpec((tm, D_MODEL), lambda i: (i, 0)),
            pl.BlockSpec((tm, D_MODEL), lambda i: (i, 0)),
            pl.BlockSpec((TOP_K, tm), lambda i: (0, i)),
            pl.BlockSpec((TOP_K, tm), lambda i: (0, i)),
            pl.BlockSpec((N_EXPERTS, LANE), lambda i: (i, 0)),
        ],
        out_shape=[
            jax.ShapeDtypeStruct((t, D_MODEL), F32),
            jax.ShapeDtypeStruct((t, D_MODEL), BF16),
            jax.ShapeDtypeStruct((TOP_K, t), I32),
            jax.ShapeDtypeStruct((TOP_K, t), F32),
            jax.ShapeDtypeStruct((nt * N_EXPERTS, LANE), I32),
        ],
        compiler_params=_cparams(("parallel",)),
        name="out_router",
    )(x2, out_a, out_b, wo_bf, nw, rwt, rb)


def _routing_tables(nch, n_row_tiles):
    cpt = MOE_TM // ROW_CHUNK
    seg_src = jnp.cumsum(nch, axis=1) - nch
    rows_e = jnp.sum(nch, axis=0)
    tiles_e = (rows_e + cpt - 1) // cpt
    cum_tiles = jnp.cumsum(tiles_e)
    base_e = (cum_tiles - tiles_e) * cpt
    seg_dst = base_e[None, :] + jnp.cumsum(nch, axis=0) - nch
    gap_start = base_e + rows_e
    cps = MOE_SUB // ROW_CHUNK
    gap_len = (rows_e + cps - 1) // cps * cps - rows_e
    n_used = cum_tiles[-1]
    tile_ids = jnp.arange(n_row_tiles, dtype=I32)
    tile_e = jnp.sum((cum_tiles[None, :] <= jnp.minimum(tile_ids, n_used - 1)[:, None]).astype(I32), axis=1)
    tile_e = jnp.minimum(tile_e, N_EXPERTS - 1)
    first = jnp.concatenate([jnp.ones((1,), I32), (tile_e[1:] != tile_e[:-1]).astype(I32)])
    e_ids = jnp.arange(N_EXPERTS, dtype=I32)
    later = (e_ids[None, :] > e_ids[:, None]) & (tiles_e[None, :] > 0)
    next_e = jnp.min(jnp.where(later, e_ids[None, :], N_EXPERTS), axis=1)
    next_e = jnp.where(next_e == N_EXPERTS, -1, next_e)
    own = (tile_e[:, None] == e_ids[None, :]).astype(I32)
    pick = lambda per_expert: jnp.sum(own * per_expert[None, :], axis=1)
    tile_next = pick(next_e)
    tile_k = tile_ids - pick(cum_tiles - tiles_e)
    tile_rows = jnp.clip(pick(rows_e) * ROW_CHUNK - tile_k * MOE_TM, 0, MOE_TM)
    tot_gather = jnp.sum(nch, axis=1)
    is_last = jnp.arange(nch.shape[0], dtype=I32) == nch.shape[0] - 1
    tot_scatter = tot_gather + jnp.where(is_last, jnp.sum(gap_len), 0)
    flat = lambda a: a.reshape(-1).astype(I32)
    return (flat(seg_src), flat(seg_dst), flat(nch), flat(gap_start), flat(gap_len),
            flat(tot_scatter), flat(tot_gather),
            (tile_e.astype(I32), first, tile_next.astype(I32), tile_rows.astype(I32),
             n_used.reshape(1).astype(I32)))


def _chunk_rows(ref, chunk_idx):
    return ref.at[pl.ds(pl.multiple_of(chunk_idx * ROW_CHUNK, ROW_CHUNK), ROW_CHUNK), :]


def _compact_kernel(src_ref, dst_ref, nch_ref, gap_start_ref, gap_len_ref, tot_ref,
                    x_ref, slot_ref, w_ref, xs_ref, buf_ref, sems):
    tt = pl.program_id(0)
    n_tt = pl.num_programs(0)
    cur = tt & 1
    x = x_ref[...]
    s0, s1, s2, s3 = (slot_ref[kk:kk + 1, :] for kk in range(TOP_K))
    w0, w1, w2, w3 = (w_ref[kk:kk + 1, :] for kk in range(TOP_K))
    lane = lax.broadcasted_iota(I32, (SLOT_BLK, LANE), 1)
    for blk in range(SLOTS // SLOT_BLK):
        rows = slice(blk * SLOT_BLK, (blk + 1) * SLOT_BLK)
        sid = lax.broadcasted_iota(I32, (SLOT_BLK, TOK_TILE), 0) + blk * SLOT_BLK
        hit = jnp.where(sid == s0, 1.0, jnp.where(sid == s1, 1.0, jnp.where(sid == s2, 1.0,
                        jnp.where(sid == s3, 1.0, 0.0))))
        buf_ref[cur, rows, 0:D_MODEL] = jnp.dot(hit.astype(BF16), x, preferred_element_type=F32)
        wsel = jnp.where(sid == s0, w0, jnp.where(sid == s1, w1, jnp.where(sid == s2, w2,
                         jnp.where(sid == s3, w3, 0.0))))
        wrow = jnp.sum(wsel, axis=1, keepdims=True)
        buf_ref[cur, rows, D_MODEL:XS_COLS] = jnp.where(lane == 0, wrow, 0.0)

    def copy(slot, src_chunk, dst_chunk):
        return pltpu.make_async_copy(_chunk_rows(buf_ref.at[slot], src_chunk), _chunk_rows(xs_ref, dst_chunk),
                                     sems.at[slot])

    def per_expert(e, _):
        n = nch_ref[tt * N_EXPERTS + e]
        src0 = src_ref[tt * N_EXPERTS + e]
        dst0 = dst_ref[tt * N_EXPERTS + e]

        def issue(kk, _):
            copy(cur, src0 + kk, dst0 + kk).start()
            return 0

        lax.fori_loop(0, n, issue, 0)
        return 0

    lax.fori_loop(0, N_EXPERTS, per_expert, 0)

    zero_chunk = SLOTS // ROW_CHUNK - 1

    def per_gap(e, _):
        n = jnp.where(tt == n_tt - 1, gap_len_ref[e], 0)
        g0 = gap_start_ref[e]

        def issue(kk, _):
            copy(cur, zero_chunk, g0 + kk).start()
            return 0

        lax.fori_loop(0, n, issue, 0)
        return 0

    lax.fori_loop(0, N_EXPERTS, per_gap, 0)

    def drain(slot, count):
        def one(_, c):
            copy(slot, 0, 0).wait()
            return c

        lax.fori_loop(0, count, one, 0)

    @pl.when(tt > 0)
    def _():
        drain(1 - cur, tot_ref[tt - 1])

    @pl.when(tt == n_tt - 1)
    def _():
        drain(cur, tot_ref[tt])


def _compact(tables, hn2, slots, top_w, n_rows):
    seg_src, seg_dst, nch, gap_start, gap_len, tot = tables
    t = hn2.shape[0]
    return pl.pallas_call(
        _compact_kernel,
        grid_spec=pltpu.PrefetchScalarGridSpec(
            num_scalar_prefetch=6,
            grid=(t // TOK_TILE,),
            in_specs=[
                pl.BlockSpec((TOK_TILE, D_MODEL), lambda i, *_: (i, 0)),
                pl.BlockSpec((TOP_K, TOK_TILE), lambda i, *_: (0, i)),
                pl.BlockSpec((TOP_K, TOK_TILE), lambda i, *_: (0, i)),
            ],
            out_specs=pl.BlockSpec(memory_space=pl.ANY),
            scratch_shapes=[pltpu.VMEM((2, SLOTS, XS_COLS), F32), pltpu.SemaphoreType.DMA((2,))],
        ),
        out_shape=jax.ShapeDtypeStruct((n_rows, XS_COLS), F32),
        compiler_params=_cparams(("arbitrary",)),
        name="compact",
    )(seg_src, seg_dst, nch, gap_start, gap_len, tot, hn2, slots, top_w)


def _per_fill(rows, compute):
    for n in range(MOE_SUB, MOE_TM + 1, MOE_SUB):
        @pl.when((rows > n - MOE_SUB) & (rows <= n))
        def _():
            compute(n)


def _moe_up_kernel(te_ref, first_ref, next_ref, rows_ref, nu_ref, x_ref, wg_hbm, wu_hbm, bg_ref, bu_ref, h_ref,
                   wg_st, wu_st, wg_bf, wu_bf, sems):
    c = pl.program_id(0)
    i = pl.program_id(1)

    def fetch(e, cc):
        col = pl.ds(pl.multiple_of(cc * MOE_FC, MOE_FC), MOE_FC)
        return (pltpu.make_async_copy(wg_hbm.at[e, :, col], wg_st, sems.at[0]),
                pltpu.make_async_copy(wu_hbm.at[e, :, col], wu_st, sems.at[1]))

    @pl.when((c == 0) & (i == 0))
    def _():
        for cp in fetch(te_ref[0], 0):
            cp.start()

    @pl.when(i < nu_ref[0])
    def _():
        @pl.when(first_ref[i] == 1)
        def _():
            for cp in fetch(te_ref[i], c):
                cp.wait()
            wg_bf[...] = wg_st[...].astype(BF16)
            wu_bf[...] = wu_st[...].astype(BF16)
            nxt = next_ref[i]

            @pl.when(nxt >= 0)
            def _():
                for cp in fetch(nxt, c):
                    cp.start()

            @pl.when((nxt < 0) & (c + 1 < pl.num_programs(0)))
            def _():
                for cp in fetch(te_ref[0], c + 1):
                    cp.start()

        def compute(n):
            x = x_ref[0:n, :].astype(BF16)
            gate = jnp.minimum(jnp.dot(x, wg_bf[...], preferred_element_type=F32) + bg_ref[...], SWIGLU_LIMIT)
            up = jnp.clip(jnp.dot(x, wu_bf[...], preferred_element_type=F32) + bu_ref[...],
                          -SWIGLU_LIMIT, SWIGLU_LIMIT)
            act = (up + 1.0) * gate * (1.0 / (1.0 + jnp.exp(-SWIGLU_ALPHA * gate)))
            h_ref[0:n, :] = act.astype(h_ref.dtype)

        _per_fill(rows_ref[i], compute)


def _moe_up(tiles, xs, wg, wu, bg, bu):
    n_rows = xs.shape[0]
    n_tiles = n_rows // MOE_TM
    row = lambda c, i, te, fi, nx, rw, nu: (jnp.minimum(i, nu[0] - 1), 0)
    bsel = lambda c, i, te, fi, nx, rw, nu: (te[i], 0, c)
    return pl.pallas_call(
        _moe_up_kernel,
        grid_spec=pltpu.PrefetchScalarGridSpec(
            num_scalar_prefetch=5,
            grid=(D_FF // MOE_FC, n_tiles),
            in_specs=[
                pl.BlockSpec((MOE_TM, D_MODEL), row),
                pl.BlockSpec(memory_space=pl.ANY),
                pl.BlockSpec(memory_space=pl.ANY),
                pl.BlockSpec((None, 1, MOE_FC), bsel),
                pl.BlockSpec((None, 1, MOE_FC), bsel),
            ],
            out_specs=pl.BlockSpec((MOE_TM, MOE_FC),
                                   lambda c, i, te, fi, nx, rw, nu: (jnp.minimum(i, nu[0] - 1), c)),
            scratch_shapes=[pltpu.VMEM((D_MODEL, MOE_FC), F32), pltpu.VMEM((D_MODEL, MOE_FC), F32),
                            pltpu.VMEM((D_MODEL, MOE_FC), BF16), pltpu.VMEM((D_MODEL, MOE_FC), BF16),
                            pltpu.SemaphoreType.DMA((2,))],
        ),
        out_shape=jax.ShapeDtypeStruct((n_rows, D_FF), BF16),
        compiler_params=_cparams(("arbitrary", "arbitrary")),
        name="moe_up",
    )(*tiles, xs, wg, wu, bg, bu)


def _moe_down_kernel(te_ref, first_ref, next_ref, rows_ref, nu_ref, h_ref, cw_ref, wd_hbm, bd_ref, y_ref,
                     wd_st, wd_bf, sem):
    i = pl.program_id(0)

    def fetch(e):
        return pltpu.make_async_copy(wd_hbm.at[e], wd_st, sem)

    @pl.when(i == 0)
    def _():
        fetch(te_ref[0]).start()

    @pl.when(i < nu_ref[0])
    def _():
        @pl.when(first_ref[i] == 1)
        def _():
            fetch(te_ref[i]).wait()
            wd_bf[...] = wd_st[...].astype(BF16)
            nxt = next_ref[i]

            @pl.when(nxt >= 0)
            def _():
                fetch(nxt).start()

        def compute(n):
            y = jnp.dot(h_ref[0:n, :], wd_bf[...], preferred_element_type=F32) + bd_ref[...]
            y_ref[0:n, :] = y * cw_ref[0:n, 0:1]

        _per_fill(rows_ref[i], compute)


def _moe_down(tiles, hs, xs, wd, bd):
    n_rows = hs.shape[0]
    n_tiles = n_rows // MOE_TM
    row = lambda i, te, fi, nx, rw, nu: (jnp.minimum(i, nu[0] - 1), 0)
    return pl.pallas_call(
        _moe_down_kernel,
        grid_spec=pltpu.PrefetchScalarGridSpec(
            num_scalar_prefetch=5,
            grid=(n_tiles,),
            in_specs=[
                pl.BlockSpec((MOE_TM, D_FF), row),
                pl.BlockSpec((MOE_TM, LANE),
                             lambda i, te, fi, nx, rw, nu: (jnp.minimum(i, nu[0] - 1), D_MODEL // LANE)),
                pl.BlockSpec(memory_space=pl.ANY),
                pl.BlockSpec((None, 1, D_MODEL), lambda i, te, fi, nx, rw, nu: (te[i], 0, 0)),
            ],
            out_specs=pl.BlockSpec((MOE_TM, D_MODEL), row),
            scratch_shapes=[pltpu.VMEM((D_FF, D_MODEL), F32), pltpu.VMEM((D_FF, D_MODEL), BF16),
                            pltpu.SemaphoreType.DMA(())],
        ),
        out_shape=jax.ShapeDtypeStruct((n_rows, D_MODEL), F32),
        compiler_params=_cparams(("arbitrary",)),
        name="moe_down",
    )(*tiles, hs, xs, wd, bd)


def _combine_kernel(src_ref, dst_ref, nch_ref, tot_ref, h_ref, slot_ref, ys_ref, o_ref, buf_ref, sems):
    tt = pl.program_id(0)
    n_tt = pl.num_programs(0)
    cur = tt & 1

    def copy(slot, sorted_chunk, local_chunk):
        return pltpu.make_async_copy(_chunk_rows(ys_ref, sorted_chunk), _chunk_rows(buf_ref.at[slot], local_chunk),
                                     sems.at[slot])

    def gather(tile, slot):
        def per_expert(e, _):
            n = nch_ref[tile * N_EXPERTS + e]
            loc0 = src_ref[tile * N_EXPERTS + e]
            srt0 = dst_ref[tile * N_EXPERTS + e]

            def issue(kk, _):
                copy(slot, srt0 + kk, loc0 + kk).start()
                return 0

            lax.fori_loop(0, n, issue, 0)
            return 0

        lax.fori_loop(0, N_EXPERTS, per_expert, 0)

    @pl.when(tt == 0)
    def _():
        buf_ref[...] = jnp.zeros_like(buf_ref)
        gather(0, 0)

    @pl.when(tt + 1 < n_tt)
    def _():
        gather(tt + 1, 1 - cur)

    def one(_, c):
        copy(cur, 0, 0).wait()
        return c

    lax.fori_loop(0, tot_ref[tt], one, 0)

    o_ref[...] = h_ref[...]
    for blk in range(SLOTS // SLOT_BLK):
        sid = lax.broadcasted_iota(I32, (TOK_TILE, SLOT_BLK), 1) + blk * SLOT_BLK
        hit = jnp.zeros((TOK_TILE, SLOT_BLK), F32)
        for kk in range(TOP_K):
            hit = jnp.where(sid == slot_ref[:, kk:kk + 1], 1.0, hit)
        yb = buf_ref[cur, blk * SLOT_BLK:(blk + 1) * SLOT_BLK, :].astype(BF16)
        o_ref[...] += jnp.dot(hit.astype(BF16), yb, preferred_element_type=F32)


def _combine(tables, h, slots_t, ys):
    seg_src, seg_dst, nch, tot = tables
    t = h.shape[0]
    return pl.pallas_call(
        _combine_kernel,
        grid_spec=pltpu.PrefetchScalarGridSpec(
            num_scalar_prefetch=4,
            grid=(t // TOK_TILE,),
            in_specs=[
                pl.BlockSpec((TOK_TILE, D_MODEL), lambda i, *_: (i, 0)),
                pl.BlockSpec((TOK_TILE, TOP_K), lambda i, *_: (i, 0)),
                pl.BlockSpec(memory_space=pl.ANY),
            ],
            out_specs=pl.BlockSpec((TOK_TILE, D_MODEL), lambda i, *_: (i, 0)),
            scratch_shapes=[pltpu.VMEM((2, SLOTS, D_MODEL), F32), pltpu.SemaphoreType.DMA((2,))],
        ),
        out_shape=jax.ShapeDtypeStruct((t, D_MODEL), F32),
        compiler_params=_cparams(("arbitrary",)),
        name="combine",
    )(seg_src, seg_dst, nch, tot, h, slots_t, ys)


def kernel(x, attn_norm_w, w_in, da_q_norm_w, da_k_norm_w, da_lambda_q1, da_lambda_k1, da_lambda_q2,
           da_lambda_k2, da_subln_w, gla_gate_up_w, gla_gate_up_b, gla_norm_w, w_out, ffn_norm_w,
           router_w, router_b, exp_w_gate, exp_b_gate, exp_w_up, exp_b_up, exp_w_down, exp_b_down):
    b, s, d = x.shape
    t = b * s
    assert d == D_MODEL and t % TOK_TILE == 0 and s % CHUNK == 0 and attn_norm_w.shape[0] == 1
    x2 = x.reshape(t, d)

    w_tail = jnp.pad(w_in[0, :, IN_COLS_MAIN:].astype(BF16), ((0, 0), (0, LANE - GLA_RANK)))
    hn, gd = _rms_gd(x2, attn_norm_w, w_tail)
    proj3 = _in_proj(hn, w_in).reshape(b, s, IN_COLS_MAIN)

    out_a = _diff_attn(proj3, jnp.tile(da_q_norm_w, (1, 2 * ATT_HP)), jnp.tile(da_k_norm_w, (1, 2 * ATT_HP)),
                       da_lambda_q1, da_lambda_k1, da_lambda_q2, da_lambda_k2, da_subln_w)
    wup_pad = jnp.pad(gla_gate_up_w[0].astype(BF16), ((0, LANE - GLA_RANK), (0, 0)))
    out_b = _gla(proj3, gd.reshape(b, s, LANE), wup_pad, gla_gate_up_b, gla_norm_w)

    h, hn2, slots, top_w, nch_b = _out_router(
        x2, out_a.reshape(t, -1), out_b.reshape(t, -1), w_out[0].astype(BF16), ffn_norm_w,
        router_w[0].T, router_b.reshape(N_EXPERTS, 1))

    n_tok_tiles = t // TOK_TILE
    nch = nch_b[:, 0].reshape(n_tok_tiles, N_EXPERTS)
    n_row_tiles = (TOP_K * t + n_tok_tiles * N_EXPERTS * (ROW_CHUNK - 1)) // MOE_TM + N_EXPERTS
    seg_src, seg_dst, nch_f, gap_start, gap_len, tot_scatter, tot_gather, tiles = _routing_tables(nch, n_row_tiles)

    xs = _compact((seg_src, seg_dst, nch_f, gap_start, gap_len, tot_scatter), hn2, slots, top_w,
                  n_row_tiles * MOE_TM)
    hs = _moe_up(tiles, xs,
                 exp_w_gate.reshape(N_EXPERTS, D_MODEL, D_FF), exp_w_up.reshape(N_EXPERTS, D_MODEL, D_FF),
                 exp_b_gate.reshape(N_EXPERTS, 1, D_FF), exp_b_up.reshape(N_EXPERTS, 1, D_FF))
    ys = _moe_down(tiles, hs, xs, exp_w_down.reshape(N_EXPERTS, D_FF, D_MODEL),
                   exp_b_down.reshape(N_EXPERTS, 1, D_MODEL))
    out = _combine((seg_src, seg_dst, nch_f, tot_gather), h, slots.T, ys)
    return out.reshape(b, s, d)
```

```python
import functools
import math

import jax
import jax.numpy as jnp
from jax import lax
from jax.experimental import pallas as pl
from jax.experimental.pallas import tpu as pltpu

F32 = jnp.float32
BF16 = jnp.bfloat16
I32 = jnp.int32

D_MODEL = 2048
CHUNK = 64
EPS = 1e-6
DA_HEADS = 8
DA_QK = 64
DA_V = 128
GLA_HEADS = 4
GLA_DK = 128
GLA_DV = 256
GLA_RANK = 16
GLA_TAU = 16.0
N_EXPERTS = 32
TOP_K = 4
D_FF = 2048
SWIGLU_LIMIT = 7.0
SWIGLU_ALPHA = 1.702
LAMBDA_INIT = 0.8 - 0.6 * math.exp(-0.3 * 0)

IN_COLS = 6160
IN_COLS_MAIN = IN_COLS - GLA_RANK
LANE = 128
ROW_CHUNK = 16

IN_TM = 1024
IN_TN = 1024
ATT_BQ = 256
ATT_HP = 4
ATT_SAFE_BOUND = 45.0
TOK_TILE = 512
SLOTS = TOP_K * TOK_TILE + N_EXPERTS * ROW_CHUNK
SLOT_BLK = 256
XS_COLS = D_MODEL + LANE
MOE_TM = 512
MOE_SUB = 128
MOE_FC = 1024
VMEM_LIMIT = 56 * 1024 * 1024


def _cparams(sem):
    return pltpu.CompilerParams(dimension_semantics=sem, vmem_limit_bytes=VMEM_LIMIT)


def _rms_gd_kernel(x_ref, nw_ref, wt_ref, hn_ref, gd_ref):
    x = x_ref[...]
    ms = jnp.mean(x * x, axis=-1, keepdims=True)
    hn = (x * lax.rsqrt(ms + EPS) * nw_ref[...]).astype(BF16)
    hn_ref[...] = hn
    gd_ref[...] = jnp.dot(hn, wt_ref[...], preferred_element_type=F32).astype(gd_ref.dtype)


def _rms_gd(x2, norm_w, w_tail):
    t = x2.shape[0]
    tm = min(IN_TM, t)
    return pl.pallas_call(
        _rms_gd_kernel,
        grid=(t // tm,),
        in_specs=[
            pl.BlockSpec((tm, D_MODEL), lambda i: (i, 0)),
            pl.BlockSpec((1, D_MODEL), lambda i: (0, 0)),
            pl.BlockSpec((D_MODEL, LANE), lambda i: (0, 0)),
        ],
        out_specs=[pl.BlockSpec((tm, D_MODEL), lambda i: (i, 0)), pl.BlockSpec((tm, LANE), lambda i: (i, 0))],
        out_shape=[jax.ShapeDtypeStruct((t, D_MODEL), BF16), jax.ShapeDtypeStruct((t, LANE), BF16)],
        compiler_params=_cparams(("parallel",)),
        name="rms_gd",
    )(x2, norm_w, w_tail)


def _in_proj_kernel(hn_ref, w_ref, o_ref, w_bf):
    @pl.when(pl.program_id(1) == 0)
    def _():
        w_bf[...] = w_ref[...].astype(BF16)

    o_ref[...] = jnp.dot(hn_ref[...], w_bf[...], preferred_element_type=F32).astype(o_ref.dtype)


def _in_proj(hn, w_in):
    t = hn.shape[0]
    tm = min(IN_TM, t)
    return pl.pallas_call(
        _in_proj_kernel,
        grid=(IN_COLS_MAIN // IN_TN, t // tm),
        in_specs=[
            pl.BlockSpec((tm, D_MODEL), lambda j, i: (i, 0)),
            pl.BlockSpec((D_MODEL, IN_TN), lambda j, i: (0, j)),
        ],
        out_specs=pl.BlockSpec((tm, IN_TN), lambda j, i: (i, j)),
        out_shape=jax.ShapeDtypeStruct((t, IN_COLS_MAIN), BF16),
        scratch_shapes=[pltpu.VMEM((D_MODEL, IN_TN), BF16)],
        compiler_params=_cparams(("arbitrary", "arbitrary")),
        name="in_proj",
    )(hn, w_in)


def _attn_kernel(q_ref, k_ref, v_ref, qw_ref, kw_ref, lq1_ref, lk1_ref, lq2_ref, lk2_ref, sw_ref,
                 o_ref, kn_ref, vt_ref, *, bq):
    qi = pl.program_id(2)
    first_map = lax.broadcasted_iota(I32, (1, LANE), 1) < DA_QK
    n_grp = 2 * bq // LANE
    shift = CHUNK.bit_length() - 1

    def qk_norm(x, w):
        x2 = x * x
        scale = []
        for hh in range(ATT_HP):
            xh = x2[:, hh * LANE:(hh + 1) * LANE]
            s1 = jnp.sum(jnp.where(first_map, xh, 0.0), axis=-1, keepdims=True)
            s2 = jnp.sum(jnp.where(first_map, 0.0, xh), axis=-1, keepdims=True)
            scale.append(jnp.where(first_map, lax.rsqrt(s1 / DA_QK + EPS), lax.rsqrt(s2 / DA_QK + EPS)))
        return x * jnp.concatenate(scale, axis=1) * w

    @pl.when(qi == 0)
    def _():
        kn_ref[...] = qk_norm(k_ref[...].astype(F32), kw_ref[...]).astype(BF16)
        vt_ref[...] = v_ref[...].astype(F32).T.astype(BF16)

    q = qk_norm(q_ref[...].astype(F32), qw_ref[...]) * (DA_QK ** -0.5 * math.log2(math.e))
    q_t = []
    for hh in range(ATT_HP):
        qh = q[:, hh * LANE:(hh + 1) * LANE]
        q_t.append(jnp.concatenate([jnp.where(first_map, qh, 0.0), jnp.where(first_map, 0.0, qh)],
                                   axis=0).T.astype(BF16))

    bound = jnp.max(jnp.abs(qw_ref[...] * kw_ref[...]), axis=-1, keepdims=True) * (
        DA_QK ** 0.5 * math.log2(math.e) * 1.02)

    def step(j, carry, masked, fixed_shift):
        r0 = pl.multiple_of(j * bq, bq)
        scores = [jnp.dot(kn_ref[pl.ds(r0, bq), hh * LANE:(hh + 1) * LANE], q_t[hh], preferred_element_type=F32)
                  for hh in range(ATT_HP)]
        out = []
        for hh in range(ATT_HP):
            ms, ls, alphas, ps = [], [], [], []
            for g in range(n_grp):
                m, l, _ = carry[hh][g]
                s = scores[hh][:, g * LANE:(g + 1) * LANE]
                if masked:
                    key_chunk = lax.broadcasted_iota(I32, (bq, LANE), 0) >> shift
                    qry_chunk = (lax.broadcasted_iota(I32, (bq, LANE), 1) + (g * LANE) % bq) >> shift
                    s = jnp.where(key_chunk <= qry_chunk, s, -jnp.inf)
                if fixed_shift:
                    p = jnp.exp2(s - bound)
                    ms.append(m)
                    ls.append(l + jnp.sum(p, axis=0, keepdims=True))
                else:
                    m_new = jnp.maximum(m, jnp.max(s, axis=0, keepdims=True))
                    alpha = jnp.exp2(m - m_new)
                    p = jnp.exp2(s - m_new)
                    ms.append(m_new)
                    ls.append(alpha * l + jnp.sum(p, axis=0, keepdims=True))
                    alphas.append(alpha)
                ps.append(p.astype(BF16))
            pv = jnp.dot(vt_ref[hh * DA_V:(hh + 1) * DA_V, pl.ds(r0, bq)], jnp.concatenate(ps, axis=1),
                         preferred_element_type=F32)
            out.append(tuple((ms[g], ls[g],
                              (carry[hh][g][2] if fixed_shift else alphas[g] * carry[hh][g][2])
                              + pv[:, g * LANE:(g + 1) * LANE]) for g in range(n_grp)))
        return tuple(out)

    lam = (jnp.exp(jnp.sum(lq1_ref[...] * lk1_ref[...], axis=-1, keepdims=True))
           - jnp.exp(jnp.sum(lq2_ref[...] * lk2_ref[...], axis=-1, keepdims=True)) + LAMBDA_INIT)

    def attend(fixed_shift):
        init = tuple(tuple((jnp.full((1, LANE), -jnp.inf, F32), jnp.zeros((1, LANE), F32),
                            jnp.zeros((DA_V, LANE), F32)) for _ in range(n_grp)) for _ in range(ATT_HP))
        carry = lax.fori_loop(0, qi, lambda j, c: step(j, c, False, fixed_shift), init)
        carry = step(qi, carry, True, fixed_shift)
        half = n_grp // 2
        for hh in range(ATT_HP):
            for g in range(half):
                o1 = carry[hh][g][2] / carry[hh][g][1]
                o2 = carry[hh][g + half][2] / carry[hh][g + half][1]
                out = o1 - lam * o2
                ms = jnp.mean(out * out, axis=0, keepdims=True)
                y = out * lax.rsqrt(ms + EPS) * sw_ref[...] * (1.0 - LAMBDA_INIT)
                o_ref[g * LANE:(g + 1) * LANE, hh * DA_V:(hh + 1) * DA_V] = y.T.astype(o_ref.dtype)

    small = jnp.max(bound) <= ATT_SAFE_BOUND

    @pl.when(small)
    def _():
        attend(True)

    @pl.when(jnp.logical_not(small))
    def _():
        attend(False)


def _diff_attn(proj3, qw, kw, lq1, lk1, lq2, lk2, sw):
    b, s, _ = proj3.shape
    bq = min(ATT_BQ, s)
    hw = ATT_HP * LANE
    n_hg = DA_HEADS // ATT_HP
    vec = lambda n: pl.BlockSpec((1, n), lambda bi, h, qi: (0, 0))
    return pl.pallas_call(
        functools.partial(_attn_kernel, bq=bq),
        grid=(b, n_hg, s // bq),
        in_specs=[
            pl.BlockSpec((None, bq, hw), lambda bi, h, qi: (bi, qi, h)),
            pl.BlockSpec((None, s, hw), lambda bi, h, qi: (bi, 0, n_hg + h)),
            pl.BlockSpec((None, s, hw), lambda bi, h, qi: (bi, 0, 2 * n_hg + h)),
            vec(hw), vec(hw), vec(DA_QK), vec(DA_QK), vec(DA_QK), vec(DA_QK),
            pl.BlockSpec((DA_V, 1), lambda bi, h, qi: (0, 0)),
        ],
        out_specs=pl.BlockSpec((None, bq, hw), lambda bi, h, qi: (bi, qi, h)),
        out_shape=jax.ShapeDtypeStruct((b, s, DA_HEADS * DA_V), BF16),
        scratch_shapes=[pltpu.VMEM((s, hw), BF16), pltpu.VMEM((hw, s), BF16)],
        compiler_params=_cparams(("parallel", "parallel", "arbitrary")),
        name="diff_attn",
    )(proj3, proj3, proj3, qw, kw, lq1, lk1, lq2, lk2, sw.reshape(DA_V, 1))


GLA_SUB = 16


def _gla_kernel(q_ref, k_ref, v_ref, r_ref, gd_ref, wup_ref, bup_ref, nw_ref, o_ref, st_ref, *, n_chunks):
    st_ref[...] = jnp.zeros_like(st_ref)
    rr = lax.broadcasted_iota(I32, (CHUNK, CHUNK), 0)
    cc = lax.broadcasted_iota(I32, (CHUNK, CHUNK), 1)
    tril = jnp.where(cc <= rr, 1.0, 0.0).astype(BF16)
    sub_row = lax.broadcasted_iota(I32, (GLA_SUB, GLA_DK), 0)
    sub_lane = lax.broadcasted_iota(I32, (GLA_SUB, CHUNK), 1)
    chunk_row = lax.broadcasted_iota(I32, (CHUNK, GLA_DK), 0)
    nt = (((1,), (1,)), ((), ()))
    tn = (((0,), (0,)), ((), ()))

    def chunk(c, _):
        r0 = pl.multiple_of(c * CHUNK, CHUNK)
        z = jnp.dot(gd_ref[pl.ds(r0, CHUNK), :], wup_ref[...], preferred_element_type=F32) + bup_ref[...]
        lg = (jnp.minimum(z, 0.0) - jnp.log(1.0 + jnp.exp(-jnp.abs(z)))) / GLA_TAU
        lg_hi = lg.astype(BF16)
        lg_lo = (lg - lg_hi.astype(F32)).astype(BF16)
        bcum = (jnp.dot(tril, lg_hi, preferred_element_type=F32)
                + jnp.dot(tril, lg_lo, preferred_element_type=F32))
        bprev = bcum - lg
        for hh in range(GLA_HEADS):
            kcol = slice(hh * GLA_DK, (hh + 1) * GLA_DK)
            head_chunk(r0, hh, bcum[:, kcol], bprev[:, kcol])
        return 0

    def head_chunk(r0, hh, bcum, bprev):
        kcol = slice(hh * GLA_DK, (hh + 1) * GLA_DK)
        vcol = slice(hh * GLA_DV, (hh + 1) * GLA_DV)
        q = q_ref[pl.ds(r0, CHUNK), kcol].astype(F32) * (GLA_DK ** -0.5)
        k = k_ref[pl.ds(r0, CHUNK), kcol].astype(F32)
        v = v_ref[pl.ds(r0, CHUNK), vcol]
        st = st_ref[hh]
        o_inter = lax.dot_general((q * jnp.exp(bcum)).astype(BF16), st.astype(BF16), nt,
                                  preferred_element_type=F32)
        a_rows = []
        for blk in range(CHUNK // GLA_SUB):
            lo = blk * GLA_SUB
            hi = lo + GLA_SUB
            b_blk = bcum[lo:hi]
            q_blk = q[lo:hi]
            if blk > 0:
                ref = bprev[lo:lo + 1]
                qt = (q_blk * jnp.exp(b_blk - ref)).astype(BF16)
                kt = (k * jnp.exp(jnp.where(chunk_row < lo, ref - bcum, -jnp.inf))).astype(BF16)
                a_blk = lax.dot_general(qt, kt, nt, preferred_element_type=F32)
            else:
                a_blk = jnp.zeros((GLA_SUB, CHUNK), F32)
            for jj in range(GLA_SUB):
                dlt = jnp.where(sub_row >= jj, b_blk - bcum[lo + jj:lo + jj + 1], -jnp.inf)
                col = jnp.sum(q_blk * k[lo + jj:lo + jj + 1] * jnp.exp(dlt), axis=-1, keepdims=True)
                a_blk = a_blk + jnp.where(sub_lane == lo + jj, col, 0.0)
            a_rows.append(a_blk.astype(BF16))
        o = o_inter + jnp.dot(jnp.concatenate(a_rows, axis=0), v, preferred_element_type=F32)
        ms = jnp.mean(o * o, axis=-1, keepdims=True)
        rg = r_ref[pl.ds(r0, CHUNK), vcol].astype(F32)
        y = o * lax.rsqrt(ms + EPS) * nw_ref[...] * (rg / (1.0 + jnp.exp(-rg)))
        o_ref[pl.ds(r0, CHUNK), vcol] = y.astype(o_ref.dtype)
        b_last = bcum[CHUNK - 1:CHUNK]
        k_dec = (k * jnp.exp(b_last - bcum)).astype(BF16)
        st_ref[hh] = st * jnp.exp(b_last) + lax.dot_general(v, k_dec, tn, preferred_element_type=F32)

    lax.fori_loop(0, n_chunks, chunk, 0)


def _gla(proj3, gd3, wup_pad, bup, nw):
    b, s, _ = proj3.shape
    kw = GLA_HEADS * GLA_DK
    vw = GLA_HEADS * GLA_DV
    q0 = 3 * DA_HEADS * DA_V // kw
    k0 = q0 + 1
    v0 = (k0 + 1) * kw // vw
    r0 = v0 + 1
    return pl.pallas_call(
        functools.partial(_gla_kernel, n_chunks=s // CHUNK),
        grid=(b,),
        in_specs=[
            pl.BlockSpec((None, s, kw), lambda bi: (bi, 0, q0)),
            pl.BlockSpec((None, s, kw), lambda bi: (bi, 0, k0)),
            pl.BlockSpec((None, s, vw), lambda bi: (bi, 0, v0)),
            pl.BlockSpec((None, s, vw), lambda bi: (bi, 0, r0)),
            pl.BlockSpec((None, s, LANE), lambda bi: (bi, 0, 0)),
            pl.BlockSpec((LANE, kw), lambda bi: (0, 0)),
            pl.BlockSpec((1, kw), lambda bi: (0, 0)),
            pl.BlockSpec((1, GLA_DV), lambda bi: (0, 0)),
        ],
        out_specs=pl.BlockSpec((None, s, vw), lambda bi: (bi, 0, 0)),
        out_shape=jax.ShapeDtypeStruct((b, s, vw), BF16),
        scratch_shapes=[pltpu.VMEM((GLA_HEADS, GLA_DV, GLA_DK), F32)],
        compiler_params=_cparams(("parallel",)),
        name="gla",
    )(proj3, proj3, proj3, proj3, gd3, wup_pad, bup, nw)


def _out_router_kernel(x_ref, a_ref, b_ref, wo_ref, nw_ref, rwt_ref, rb_ref,
                       h_ref, hn_ref, slot_ref, w_ref, nch_ref):
    tm = x_ref.shape[0]
    half = a_ref.shape[1]
    h = (x_ref[...]
         + jnp.dot(a_ref[...], wo_ref[0:half, :], preferred_element_type=F32)
         + jnp.dot(b_ref[...], wo_ref[half:2 * half, :], preferred_element_type=F32))
    h_ref[...] = h
    ms = jnp.mean(h * h, axis=-1, keepdims=True)
    hn = h * lax.rsqrt(ms + EPS) * nw_ref[...]
    hn_hi = hn.astype(BF16)
    hn_ref[...] = hn_hi
    hn_lo = (hn - hn_hi.astype(F32)).astype(BF16)
    rwt = rwt_ref[...]
    rw_hi = rwt.astype(BF16)
    rw_lo = (rwt - rw_hi.astype(F32)).astype(BF16)
    nt = (((1,), (1,)), ((), ()))
    logits = (lax.dot_general(rw_hi, hn_hi, nt, preferred_element_type=F32)
              + lax.dot_general(rw_hi, hn_lo, nt, preferred_element_type=F32)
              + lax.dot_general(rw_lo, hn_hi, nt, preferred_element_type=F32)
              + rb_ref[...])
    e_iota = lax.broadcasted_iota(I32, (N_EXPERTS, tm), 0).astype(F32)
    vals, hots = [], []
    for _ in range(TOP_K):
        m = jnp.max(logits, axis=0, keepdims=True)
        idx = jnp.min(jnp.where(logits == m, e_iota, float(N_EXPERTS)), axis=0, keepdims=True)
        hot = e_iota == idx
        vals.append(m)
        hots.append(hot)
        logits = jnp.where(hot, -jnp.inf, logits)
    exps = [jnp.exp(v - vals[0]) for v in vals]
    denom = exps[0] + exps[1] + exps[2] + exps[3]
    for kk in range(TOP_K):
        w_ref[kk:kk + 1, :] = exps[kk] / denom
    sel = jnp.zeros((N_EXPERTS, tm), F32)
    for hot in hots:
        sel = sel + jnp.where(hot, 1.0, 0.0)
    tr = lax.broadcasted_iota(I32, (tm, tm), 0)
    tc = lax.broadcasted_iota(I32, (tm, tm), 1)
    upper = jnp.where(tr < tc, 1.0, 0.0).astype(BF16)
    rank = jnp.dot(sel.astype(BF16), upper, preferred_element_type=F32)
    cnt = jnp.sum(sel, axis=1, keepdims=True)
    nch = jnp.floor((cnt + (ROW_CHUNK - 1)) * (1.0 / ROW_CHUNK))
    nch_b = jnp.broadcast_to(nch, (N_EXPERTS, LANE))
    er = lax.broadcasted_iota(I32, (N_EXPERTS, N_EXPERTS), 0)
    ec = lax.broadcasted_iota(I32, (N_EXPERTS, N_EXPERTS), 1)
    lower = jnp.where(ec < er, 1.0, 0.0).astype(BF16)
    seg = jnp.dot(lower, nch_b.astype(BF16), preferred_element_type=F32)
    pos = seg[:, 0:1] * ROW_CHUNK + rank
    for kk in range(TOP_K):
        slot_ref[kk:kk + 1, :] = jnp.sum(jnp.where(hots[kk], pos, 0.0), axis=0, keepdims=True).astype(I32)
    nch_ref[...] = nch_b.astype(I32)


def _out_router(x2, out_a, out_b, wo_bf, nw, rwt, rb):
    t = x2.shape[0]
    tm = TOK_TILE
    half = out_a.shape[1]
    nt = t // tm
    return pl.pallas_call(
        _out_router_kernel,
        grid=(nt,),
        in_specs=[
            pl.BlockSpec((tm, D_MODEL), lambda i: (i, 0)),
            pl.BlockSpec((tm, half), lambda i: (i, 0)),
            pl.BlockSpec((tm, half), lambda i: (i, 0)),
            pl.BlockSpec((2 * half, D_MODEL), lambda i: (0, 0)),
            pl.BlockSpec((1, D_MODEL), lambda i: (0, 0)),
            pl.BlockSpec((N_EXPERTS, D_MODEL), lambda i: (0, 0)),
            pl.BlockSpec((N_EXPERTS, 1), lambda i: (0, 0)),
        ],
        out_specs=[
            pl.BlockSpec((tm, D_MODEL), lambda i: (i, 0)),
            pl.BlockSpec((tm, D_MODEL), lambda i: (i, 0)),
            pl.BlockSpec((TOP_K, tm), lambda i: (0, i)),
            pl.BlockSpec((TOP_K, tm), lambda i: (0, i)),
            pl.BlockSpec((N_EXPERTS, LANE), lambda i: (i, 0)),
        ],
        out_shape=[
            jax.ShapeDtypeStruct((t, D_MODEL), F32),
            jax.ShapeDtypeStruct((t, D_MODEL), BF16),
            jax.ShapeDtypeStruct((TOP_K, t), I32),
            jax.ShapeDtypeStruct((TOP_K, t), F32),
            jax.ShapeDtypeStruct((nt * N_EXPERTS, LANE), I32),
        ],
        compiler_params=_cparams(("parallel",)),
        name="out_router",
    )(x2, out_a, out_b, wo_bf, nw, rwt, rb)


def _routing_tables(nch, n_row_tiles):
    cpt = MOE_TM // ROW_CHUNK
    seg_src = jnp.cumsum(nch, axis=1) - nch
    rows_e = jnp.sum(nch, axis=0)
    tiles_e = (rows_e + cpt - 1) // cpt
    cum_tiles = jnp.cumsum(tiles_e)
    base_e = (cum_tiles - tiles_e) * cpt
    seg_dst = base_e[None, :] + jnp.cumsum(nch, axis=0) - nch
    gap_start = base_e + rows_e
    cps = MOE_SUB // ROW_CHUNK
    gap_len = (rows_e + cps - 1) // cps * cps - rows_e
    n_used = cum_tiles[-1]
    tile_ids = jnp.arange(n_row_tiles, dtype=I32)
    tile_e = jnp.sum((cum_tiles[None, :] <= jnp.minimum(tile_ids, n_used - 1)[:, None]).astype(I32), axis=1)
    tile_e = jnp.minimum(tile_e, N_EXPERTS - 1)
    first = jnp.concatenate([jnp.ones((1,), I32), (tile_e[1:] != tile_e[:-1]).astype(I32)])
    e_ids = jnp.arange(N_EXPERTS, dtype=I32)
    later = (e_ids[None, :] > e_ids[:, None]) & (tiles_e[None, :] > 0)
    next_e = jnp.min(jnp.where(later, e_ids[None, :], N_EXPERTS), axis=1)
    next_e = jnp.where(next_e == N_EXPERTS, -1, next_e)
    own = (tile_e[:, None] == e_ids[None, :]).astype(I32)
    pick = lambda per_expert: jnp.sum(own * per_expert[None, :], axis=1)
    tile_next = pick(next_e)
    tile_k = tile_ids - pick(cum_tiles - tiles_e)
    tile_rows = jnp.clip(pick(rows_e) * ROW_CHUNK - tile_k * MOE_TM, 0, MOE_TM)
    tot_gather = jnp.sum(nch, axis=1)
    is_last = jnp.arange(nch.shape[0], dtype=I32) == nch.shape[0] - 1
    tot_scatter = tot_gather + jnp.where(is_last, jnp.sum(gap_len), 0)
    flat = lambda a: a.reshape(-1).astype(I32)
    return (flat(seg_src), flat(seg_dst), flat(nch), flat(gap_start), flat(gap_len),
            flat(tot_scatter), flat(tot_gather),
            (tile_e.astype(I32), first, tile_next.astype(I32), tile_rows.astype(I32),
             n_used.reshape(1).astype(I32)))


def _chunk_rows(ref, chunk_idx):
    return ref.at[pl.ds(pl.multiple_of(chunk_idx * ROW_CHUNK, ROW_CHUNK), ROW_CHUNK), :]


def _compact_kernel(src_ref, dst_ref, nch_ref, gap_start_ref, gap_len_ref, tot_ref,
                    x_ref, slot_ref, w_ref, xs_ref, buf_ref, sems):
    tt = pl.program_id(0)
    n_tt = pl.num_programs(0)
    cur = tt & 1
    x = x_ref[...]
    s0, s1, s2, s3 = (slot_ref[kk:kk + 1, :] for kk in range(TOP_K))
    w0, w1, w2, w3 = (w_ref[kk:kk + 1, :] for kk in range(TOP_K))
    lane = lax.broadcasted_iota(I32, (SLOT_BLK, LANE), 1)
    for blk in range(SLOTS // SLOT_BLK):
        rows = slice(blk * SLOT_BLK, (blk + 1) * SLOT_BLK)
        sid = lax.broadcasted_iota(I32, (SLOT_BLK, TOK_TILE), 0) + blk * SLOT_BLK
        hit = jnp.where(sid == s0, 1.0, jnp.where(sid == s1, 1.0, jnp.where(sid == s2, 1.0,
                        jnp.where(sid == s3, 1.0, 0.0))))
        buf_ref[cur, rows, 0:D_MODEL] = jnp.dot(hit.astype(BF16), x, preferred_element_type=F32).astype(BF16)
        wsel = jnp.where(sid == s0, w0, jnp.where(sid == s1, w1, jnp.where(sid == s2, w2,
                         jnp.where(sid == s3, w3, 0.0))))
        wrow = jnp.sum(wsel, axis=1, keepdims=True)
        t0 = wrow.astype(BF16).astype(F32)
        t1 = (wrow - t0).astype(BF16).astype(F32)
        t2 = wrow - t0 - t1
        buf_ref[cur, rows, D_MODEL:XS_COLS] = jnp.where(
            lane == 0, t0, jnp.where(lane == 1, t1, jnp.where(lane == 2, t2, 0.0))).astype(BF16)

    def copy(slot, src_chunk, dst_chunk):
        return pltpu.make_async_copy(_chunk_rows(buf_ref.at[slot], src_chunk), _chunk_rows(xs_ref, dst_chunk),
                                     sems.at[slot])

    def per_expert(e, _):
        n = nch_ref[tt * N_EXPERTS + e]
        src0 = src_ref[tt * N_EXPERTS + e]
        dst0 = dst_ref[tt * N_EXPERTS + e]

        def issue(kk, _):
            copy(cur, src0 + kk, dst0 + kk).start()
            return 0

        lax.fori_loop(0, n, issue, 0)
        return 0

    lax.fori_loop(0, N_EXPERTS, per_expert, 0)

    zero_chunk = SLOTS // ROW_CHUNK - 1

    def per_gap(e, _):
        n = jnp.where(tt == n_tt - 1, gap_len_ref[e], 0)
        g0 = gap_start_ref[e]

        def issue(kk, _):
            copy(cur, zero_chunk, g0 + kk).start()
            return 0

        lax.fori_loop(0, n, issue, 0)
        return 0

    lax.fori_loop(0, N_EXPERTS, per_gap, 0)

    def drain(slot, count):
        def one(_, c):
            copy(slot, 0, 0).wait()
            return c

        lax.fori_loop(0, count, one, 0)

    @pl.when(tt > 0)
    def _():
        drain(1 - cur, tot_ref[tt - 1])

    @pl.when(tt == n_tt - 1)
    def _():
        drain(cur, tot_ref[tt])


def _compact(tables, hn2, slots, top_w, n_rows):
    seg_src, seg_dst, nch, gap_start, gap_len, tot = tables
    t = hn2.shape[0]
    return pl.pallas_call(
        _compact_kernel,
        grid_spec=pltpu.PrefetchScalarGridSpec(
            num_scalar_prefetch=6,
            grid=(t // TOK_TILE,),
            in_specs=[
                pl.BlockSpec((TOK_TILE, D_MODEL), lambda i, *_: (i, 0)),
                pl.BlockSpec((TOP_K, TOK_TILE), lambda i, *_: (0, i)),
                pl.BlockSpec((TOP_K, TOK_TILE), lambda i, *_: (0, i)),
            ],
            out_specs=pl.BlockSpec(memory_space=pl.ANY),
            scratch_shapes=[pltpu.VMEM((2, SLOTS, XS_COLS), BF16), pltpu.SemaphoreType.DMA((2,))],
        ),
        out_shape=jax.ShapeDtypeStruct((n_rows, XS_COLS), BF16),
        compiler_params=_cparams(("arbitrary",)),
        name="compact",
    )(seg_src, seg_dst, nch, gap_start, gap_len, tot, hn2, slots, top_w)


def _per_fill(rows, compute):
    for n in range(MOE_SUB, MOE_TM + 1, MOE_SUB):
        @pl.when((rows > n - MOE_SUB) & (rows <= n))
        def _():
            compute(n)


def _moe_up_kernel(te_ref, first_ref, next_ref, rows_ref, nu_ref, x_ref, wg_hbm, wu_hbm, bg_ref, bu_ref, h_ref,
                   wg_st, wu_st, wg_bf, wu_bf, sems):
    c = pl.program_id(0)
    i = pl.program_id(1)

    def fetch(e, cc):
        col = pl.ds(pl.multiple_of(cc * MOE_FC, MOE_FC), MOE_FC)
        return (pltpu.make_async_copy(wg_hbm.at[e, :, col], wg_st, sems.at[0]),
                pltpu.make_async_copy(wu_hbm.at[e, :, col], wu_st, sems.at[1]))

    @pl.when((c == 0) & (i == 0))
    def _():
        for cp in fetch(te_ref[0], 0):
            cp.start()

    @pl.when(i < nu_ref[0])
    def _():
        @pl.when(first_ref[i] == 1)
        def _():
            for cp in fetch(te_ref[i], c):
                cp.wait()
            wg_bf[...] = wg_st[...].astype(BF16)
            wu_bf[...] = wu_st[...].astype(BF16)
            nxt = next_ref[i]

            @pl.when(nxt >= 0)
            def _():
                for cp in fetch(nxt, c):
                    cp.start()

            @pl.when((nxt < 0) & (c + 1 < pl.num_programs(0)))
            def _():
                for cp in fetch(te_ref[0], c + 1):
                    cp.start()

        def compute(n):
            x = x_ref[0:n, :]
            gate = jnp.minimum(jnp.dot(x, wg_bf[...], preferred_element_type=F32) + bg_ref[...], SWIGLU_LIMIT)
            up = jnp.clip(jnp.dot(x, wu_bf[...], preferred_element_type=F32) + bu_ref[...],
                          -SWIGLU_LIMIT, SWIGLU_LIMIT)
            act = (up + 1.0) * gate * (1.0 / (1.0 + jnp.exp(-SWIGLU_ALPHA * gate)))
            h_ref[0:n, :] = act.astype(h_ref.dtype)

        _per_fill(rows_ref[i], compute)


def _moe_up(tiles, xs, wg, wu, bg, bu):
    n_rows = xs.shape[0]
    n_tiles = n_rows // MOE_TM
    row = lambda c, i, te, fi, nx, rw, nu: (jnp.minimum(i, nu[0] - 1), 0)
    bsel = lambda c, i, te, fi, nx, rw, nu: (te[i], 0, c)
    return pl.pallas_call(
        _moe_up_kernel,
        grid_spec=pltpu.PrefetchScalarGridSpec(
            num_scalar_prefetch=5,
            grid=(D_FF // MOE_FC, n_tiles),
            in_specs=[
                pl.BlockSpec((MOE_TM, D_MODEL), row),
                pl.BlockSpec(memory_space=pl.ANY),
                pl.BlockSpec(memory_space=pl.ANY),
                pl.BlockSpec((None, 1, MOE_FC), bsel),
                pl.BlockSpec((None, 1, MOE_FC), bsel),
            ],
            out_specs=pl.BlockSpec((MOE_TM, MOE_FC),
                                   lambda c, i, te, fi, nx, rw, nu: (jnp.minimum(i, nu[0] - 1), c)),
            scratch_shapes=[pltpu.VMEM((D_MODEL, MOE_FC), F32), pltpu.VMEM((D_MODEL, MOE_FC), F32),
                            pltpu.VMEM((D_MODEL, MOE_FC), BF16), pltpu.VMEM((D_MODEL, MOE_FC), BF16),
                            pltpu.SemaphoreType.DMA((2,))],
        ),
        out_shape=jax.ShapeDtypeStruct((n_rows, D_FF), BF16),
        compiler_params=_cparams(("arbitrary", "arbitrary")),
        name="moe_up",
    )(*tiles, xs, wg, wu, bg, bu)


def _moe_down_kernel(te_ref, first_ref, next_ref, rows_ref, nu_ref, h_ref, cw_ref, wd_hbm, bd_ref, y_ref,
                     wd_st, wd_bf, sem):
    i = pl.program_id(0)

    def fetch(e):
        return pltpu.make_async_copy(wd_hbm.at[e], wd_st, sem)

    @pl.when(i == 0)
    def _():
        fetch(te_ref[0]).start()

    @pl.when(i < nu_ref[0])
    def _():
        @pl.when(first_ref[i] == 1)
        def _():
            fetch(te_ref[i]).wait()
            wd_bf[...] = wd_st[...].astype(BF16)
            nxt = next_ref[i]

            @pl.when(nxt >= 0)
            def _():
                fetch(nxt).start()

        def compute(n):
            cw = cw_ref[0:n, :].astype(F32)
            cw = cw[:, 0:1] + cw[:, 1:2] + cw[:, 2:3]
            y = (jnp.dot(h_ref[0:n, :], wd_bf[...], preferred_element_type=F32) + bd_ref[...]) * cw
            y_ref[0:n, :] = y.astype(y_ref.dtype)

        _per_fill(rows_ref[i], compute)


def _moe_down(tiles, hs, xs, wd, bd):
    n_rows = hs.shape[0]
    n_tiles = n_rows // MOE_TM
    row = lambda i, te, fi, nx, rw, nu: (jnp.minimum(i, nu[0] - 1), 0)
    return pl.pallas_call(
        _moe_down_kernel,
        grid_spec=pltpu.PrefetchScalarGridSpec(
            num_scalar_prefetch=5,
            grid=(n_tiles,),
            in_specs=[
                pl.BlockSpec((MOE_TM, D_FF), row),
                pl.BlockSpec((MOE_TM, LANE),
                             lambda i, te, fi, nx, rw, nu: (jnp.minimum(i, nu[0] - 1), D_MODEL // LANE)),
                pl.BlockSpec(memory_space=pl.ANY),
                pl.BlockSpec((None, 1, D_MODEL), lambda i, te, fi, nx, rw, nu: (te[i], 0, 0)),
            ],
            out_specs=pl.BlockSpec((MOE_TM, D_MODEL), row),
            scratch_shapes=[pltpu.VMEM((D_FF, D_MODEL), F32), pltpu.VMEM((D_FF, D_MODEL), BF16),
                            pltpu.SemaphoreType.DMA(())],
        ),
        out_shape=jax.ShapeDtypeStruct((n_rows, D_MODEL), BF16),
        compiler_params=_cparams(("arbitrary",)),
        name="moe_down",
    )(*tiles, hs, xs, wd, bd)


def _combine_kernel(src_ref, dst_ref, nch_ref, tot_ref, h_ref, slot_ref, ys_ref, o_ref, buf_ref, sems):
    tt = pl.program_id(0)
    n_tt = pl.num_programs(0)
    cur = tt & 1

    def copy(slot, sorted_chunk, local_chunk):
        return pltpu.make_async_copy(_chunk_rows(ys_ref, sorted_chunk), _chunk_rows(buf_ref.at[slot], local_chunk),
                                     sems.at[slot])

    def gather(tile, slot):
        def per_expert(e, _):
            n = nch_ref[tile * N_EXPERTS + e]
            loc0 = src_ref[tile * N_EXPERTS + e]
            srt0 = dst_ref[tile * N_EXPERTS + e]

            def issue(kk, _):
                copy(slot, srt0 + kk, loc0 + kk).start()
                return 0

            lax.fori_loop(0, n, issue, 0)
            return 0

        lax.fori_loop(0, N_EXPERTS, per_expert, 0)

    @pl.when(tt == 0)
    def _():
        buf_ref[...] = jnp.zeros_like(buf_ref)
        gather(0, 0)

    @pl.when(tt + 1 < n_tt)
    def _():
        gather(tt + 1, 1 - cur)

    def one(_, c):
        copy(cur, 0, 0).wait()
        return c

    lax.fori_loop(0, tot_ref[tt], one, 0)

    o_ref[...] = h_ref[...]
    for blk in range(SLOTS // SLOT_BLK):
        sid = lax.broadcasted_iota(I32, (TOK_TILE, SLOT_BLK), 1) + blk * SLOT_BLK
        hit = jnp.zeros((TOK_TILE, SLOT_BLK), F32)
        for kk in range(TOP_K):
            hit = jnp.where(sid == slot_ref[:, kk:kk + 1], 1.0, hit)
        yb = buf_ref[cur, blk * SLOT_BLK:(blk + 1) * SLOT_BLK, :]
        o_ref[...] += jnp.dot(hit.astype(BF16), yb, preferred_element_type=F32)


def _combine(tables, h, slots_t, ys):
    seg_src, seg_dst, nch, tot = tables
    t = h.shape[0]
    return pl.pallas_call(
        _combine_kernel,
        grid_spec=pltpu.PrefetchScalarGridSpec(
            num_scalar_prefetch=4,
            grid=(t // TOK_TILE,),
            in_specs=[
                pl.BlockSpec((TOK_TILE, D_MODEL), lambda i, *_: (i, 0)),
                pl.BlockSpec((TOK_TILE, TOP_K), lambda i, *_: (i, 0)),
                pl.BlockSpec(memory_space=pl.ANY),
            ],
            out_specs=pl.BlockSpec((TOK_TILE, D_MODEL), lambda i, *_: (i, 0)),
            scratch_shapes=[pltpu.VMEM((2, SLOTS, D_MODEL), BF16), pltpu.SemaphoreType.DMA((2,))],
        ),
        out_shape=jax.ShapeDtypeStruct((t, D_MODEL), F32),
        compiler_params=_cparams(("arbitrary",)),
        name="combine",
    )(seg_src, seg_dst, nch, tot, h, slots_t, ys)


def kernel(x, attn_norm_w, w_in, da_q_norm_w, da_k_norm_w, da_lambda_q1, da_lambda_k1, da_lambda_q2,
           da_lambda_k2, da_subln_w, gla_gate_up_w, gla_gate_up_b, gla_norm_w, w_out, ffn_norm_w,
           router_w, router_b, exp_w_gate, exp_b_gate, exp_w_up, exp_b_up, exp_w_down, exp_b_down):
    b, s, d = x.shape
    t = b * s
    assert d == D_MODEL and t % TOK_TILE == 0 and s % CHUNK == 0 and attn_norm_w.shape[0] == 1
    x2 = x.reshape(t, d)

    w_tail = jnp.pad(w_in[0, :, IN_COLS_MAIN:].astype(BF16), ((0, 0), (0, LANE - GLA_RANK)))
    hn, gd = _rms_gd(x2, attn_norm_w, w_tail)
    proj3 = _in_proj(hn, w_in.reshape(D_MODEL, IN_COLS)).reshape(b, s, IN_COLS_MAIN)

    out_a = _diff_attn(proj3, jnp.tile(da_q_norm_w, (1, 2 * ATT_HP)), jnp.tile(da_k_norm_w, (1, 2 * ATT_HP)),
                       da_lambda_q1, da_lambda_k1, da_lambda_q2, da_lambda_k2, da_subln_w)
    wup_pad = jnp.pad(gla_gate_up_w[0].astype(BF16), ((0, LANE - GLA_RANK), (0, 0)))
    out_b = _gla(proj3, gd.reshape(b, s, LANE), wup_pad, gla_gate_up_b, gla_norm_w)

    h, hn2, slots, top_w, nch_b = _out_router(
        x2, out_a.reshape(t, -1), out_b.reshape(t, -1), w_out[0].astype(BF16), ffn_norm_w,
        router_w[0].T, router_b.reshape(N_EXPERTS, 1))

    n_tok_tiles = t // TOK_TILE
    nch = nch_b[:, 0].reshape(n_tok_tiles, N_EXPERTS)
    n_row_tiles = (TOP_K * t + n_tok_tiles * N_EXPERTS * (ROW_CHUNK - 1)) // MOE_TM + N_EXPERTS
    seg_src, seg_dst, nch_f, gap_start, gap_len, tot_scatter, tot_gather, tiles = _routing_tables(nch, n_row_tiles)

    xs = _compact((seg_src, seg_dst, nch_f, gap_start, gap_len, tot_scatter), hn2, slots, top_w,
                  n_row_tiles * MOE_TM)
    hs = _moe_up(tiles, xs,
                 exp_w_gate.reshape(N_EXPERTS, D_MODEL, D_FF), exp_w_up.reshape(N_EXPERTS, D_MODEL, D_FF),
                 exp_b_gate.reshape(N_EXPERTS, 1, D_FF), exp_b_up.reshape(N_EXPERTS, 1, D_FF))
    ys = _moe_down(tiles, hs, xs, exp_w_down.reshape(N_EXPERTS, D_FF, D_MODEL),
                   exp_b_down.reshape(N_EXPERTS, 1, D_MODEL))
    out = _combine((seg_src, seg_dst, nch_f, tot_gather), h, slots.T, ys)
    return out.reshape(b, s, d)
```

```python
import functools
import math

import jax
import jax.numpy as jnp
from jax import lax
from jax.experimental import pallas as pl
from jax.experimental.pallas import tpu as pltpu

F32 = jnp.float32
BF16 = jnp.bfloat16
I32 = jnp.int32

D_MODEL = 2048
CHUNK = 64
EPS = 1e-6
DA_HEADS = 8
DA_QK = 64
DA_V = 128
GLA_HEADS = 4
GLA_DK = 128
GLA_DV = 256
GLA_RANK = 16
GLA_TAU = 16.0
N_EXPERTS = 32
TOP_K = 4
D_FF = 2048
SWIGLU_LIMIT = 7.0
SWIGLU_ALPHA = 1.702
LAMBDA_INIT = 0.8 - 0.6 * math.exp(-0.3 * 0)

IN_COLS = 6160
IN_COLS_MAIN = IN_COLS - GLA_RANK
LANE = 128
ROW_CHUNK = 16

IN_TM = 1024
IN_TN = 1024
ATT_BQ = 256
ATT_HP = 4
ATT_SAFE_BOUND = 45.0
TOK_TILE = 512
SLOTS = TOP_K * TOK_TILE + N_EXPERTS * ROW_CHUNK
SLOT_BLK = 256
XS_COLS = D_MODEL + LANE
MOE_TM = 512
MOE_SUB = 128
MOE_FC = 1024
MOE_NB = 256
VMEM_LIMIT = 56 * 1024 * 1024


def _cparams(sem):
    return pltpu.CompilerParams(dimension_semantics=sem, vmem_limit_bytes=VMEM_LIMIT)


def _rms_gd_kernel(x_ref, nw_ref, wt_ref, hn_ref, gd_ref):
    x = x_ref[...]
    ms = jnp.mean(x * x, axis=-1, keepdims=True)
    hn = (x * lax.rsqrt(ms + EPS) * nw_ref[...]).astype(BF16)
    hn_ref[...] = hn
    gd_ref[...] = jnp.dot(hn, wt_ref[...], preferred_element_type=F32).astype(gd_ref.dtype)


def _rms_gd(x2, norm_w, w_tail):
    t = x2.shape[0]
    tm = min(IN_TM, t)
    return pl.pallas_call(
        _rms_gd_kernel,
        grid=(t // tm,),
        in_specs=[
            pl.BlockSpec((tm, D_MODEL), lambda i: (i, 0)),
            pl.BlockSpec((1, D_MODEL), lambda i: (0, 0)),
            pl.BlockSpec((D_MODEL, LANE), lambda i: (0, 0)),
        ],
        out_specs=[pl.BlockSpec((tm, D_MODEL), lambda i: (i, 0)), pl.BlockSpec((tm, LANE), lambda i: (i, 0))],
        out_shape=[jax.ShapeDtypeStruct((t, D_MODEL), BF16), jax.ShapeDtypeStruct((t, LANE), BF16)],
        compiler_params=_cparams(("parallel",)),
        name="rms_gd",
    )(x2, norm_w, w_tail)


def _in_proj_kernel(hn_ref, w_ref, o_ref, w_bf):
    @pl.when(pl.program_id(1) == 0)
    def _():
        w_bf[...] = w_ref[...].astype(BF16)

    o_ref[...] = jnp.dot(hn_ref[...], w_bf[...], preferred_element_type=F32).astype(o_ref.dtype)


def _in_proj(hn, w_in):
    t = hn.shape[0]
    tm = min(IN_TM, t)
    return pl.pallas_call(
        _in_proj_kernel,
        grid=(IN_COLS_MAIN // IN_TN, t // tm),
        in_specs=[
            pl.BlockSpec((tm, D_MODEL), lambda j, i: (i, 0)),
            pl.BlockSpec((D_MODEL, IN_TN), lambda j, i: (0, j)),
        ],
        out_specs=pl.BlockSpec((tm, IN_TN), lambda j, i: (i, j)),
        out_shape=jax.ShapeDtypeStruct((t, IN_COLS_MAIN), BF16),
        scratch_shapes=[pltpu.VMEM((D_MODEL, IN_TN), BF16)],
        compiler_params=_cparams(("arbitrary", "arbitrary")),
        name="in_proj",
    )(hn, w_in)


def _attn_kernel(q_ref, k_ref, v_ref, qw_ref, kw_ref, lq1_ref, lk1_ref, lq2_ref, lk2_ref, sw_ref,
                 o_ref, kn_ref, vt_ref, *, bq):
    qi = pl.program_id(2)
    first_map = lax.broadcasted_iota(I32, (1, LANE), 1) < DA_QK
    n_grp = 2 * bq // LANE
    shift = CHUNK.bit_length() - 1

    def qk_norm(x, w):
        x2 = x * x
        scale = []
        for hh in range(ATT_HP):
            xh = x2[:, hh * LANE:(hh + 1) * LANE]
            s1 = jnp.sum(jnp.where(first_map, xh, 0.0), axis=-1, keepdims=True)
            s2 = jnp.sum(jnp.where(first_map, 0.0, xh), axis=-1, keepdims=True)
            scale.append(jnp.where(first_map, lax.rsqrt(s1 / DA_QK + EPS), lax.rsqrt(s2 / DA_QK + EPS)))
        return x * jnp.concatenate(scale, axis=1) * w

    @pl.when(qi == 0)
    def _():
        kn_ref[...] = qk_norm(k_ref[...].astype(F32), kw_ref[...]).astype(BF16)
        vt_ref[...] = v_ref[...].astype(F32).T.astype(BF16)

    q = qk_norm(q_ref[...].astype(F32), qw_ref[...]) * (DA_QK ** -0.5 * math.log2(math.e))
    q_t = []
    for hh in range(ATT_HP):
        qh = q[:, hh * LANE:(hh + 1) * LANE]
        q_t.append(jnp.concatenate([jnp.where(first_map, qh, 0.0), jnp.where(first_map, 0.0, qh)],
                                   axis=0).T.astype(BF16))

    bound = jnp.max(jnp.abs(qw_ref[...] * kw_ref[...]), axis=-1, keepdims=True) * (
        DA_QK ** 0.5 * math.log2(math.e) * 1.02)

    def step(j, carry, masked, fixed_shift):
        r0 = pl.multiple_of(j * bq, bq)
        scores = [jnp.dot(kn_ref[pl.ds(r0, bq), hh * LANE:(hh + 1) * LANE], q_t[hh], preferred_element_type=F32)
                  for hh in range(ATT_HP)]
        out = []
        for hh in range(ATT_HP):
            ms, ls, alphas, ps = [], [], [], []
            for g in range(n_grp):
                m, l, _ = carry[hh][g]
                s = scores[hh][:, g * LANE:(g + 1) * LANE]
                if masked:
                    key_chunk = lax.broadcasted_iota(I32, (bq, LANE), 0) >> shift
                    qry_chunk = (lax.broadcasted_iota(I32, (bq, LANE), 1) + (g * LANE) % bq) >> shift
                    s = jnp.where(key_chunk <= qry_chunk, s, -jnp.inf)
                if fixed_shift:
                    p = jnp.exp2(s - bound)
                    ms.append(m)
                    ls.append(l + jnp.sum(p, axis=0, keepdims=True))
                else:
                    m_new = jnp.maximum(m, jnp.max(s, axis=0, keepdims=True))
                    alpha = jnp.exp2(m - m_new)
                    p = jnp.exp2(s - m_new)
                    ms.append(m_new)
                    ls.append(alpha * l + jnp.sum(p, axis=0, keepdims=True))
                    alphas.append(alpha)
                ps.append(p.astype(BF16))
            pv = jnp.dot(vt_ref[hh * DA_V:(hh + 1) * DA_V, pl.ds(r0, bq)], jnp.concatenate(ps, axis=1),
                         preferred_element_type=F32)
            out.append(tuple((ms[g], ls[g],
                              (carry[hh][g][2] if fixed_shift else alphas[g] * carry[hh][g][2])
                              + pv[:, g * LANE:(g + 1) * LANE]) for g in range(n_grp)))
        return tuple(out)

    lam = (jnp.exp(jnp.sum(lq1_ref[...] * lk1_ref[...], axis=-1, keepdims=True))
           - jnp.exp(jnp.sum(lq2_ref[...] * lk2_ref[...], axis=-1, keepdims=True)) + LAMBDA_INIT)

    def attend(fixed_shift):
        init = tuple(tuple((jnp.full((1, LANE), -jnp.inf, F32), jnp.zeros((1, LANE), F32),
                            jnp.zeros((DA_V, LANE), F32)) for _ in range(n_grp)) for _ in range(ATT_HP))
        carry = lax.fori_loop(0, qi, lambda j, c: step(j, c, False, fixed_shift), init)
        carry = step(qi, carry, True, fixed_shift)
        half = n_grp // 2
        for hh in range(ATT_HP):
            for g in range(half):
                o1 = carry[hh][g][2] / carry[hh][g][1]
                o2 = carry[hh][g + half][2] / carry[hh][g + half][1]
                out = o1 - lam * o2
                ms = jnp.mean(out * out, axis=0, keepdims=True)
                y = out * lax.rsqrt(ms + EPS) * sw_ref[...] * (1.0 - LAMBDA_INIT)
                o_ref[g * LANE:(g + 1) * LANE, hh * DA_V:(hh + 1) * DA_V] = y.T.astype(o_ref.dtype)

    small = jnp.max(bound) <= ATT_SAFE_BOUND

    @pl.when(small)
    def _():
        attend(True)

    @pl.when(jnp.logical_not(small))
    def _():
        attend(False)


def _diff_attn(proj3, qw, kw, lq1, lk1, lq2, lk2, sw):
    b, s, _ = proj3.shape
    bq = min(ATT_BQ, s)
    hw = ATT_HP * LANE
    n_hg = DA_HEADS // ATT_HP
    vec = lambda n: pl.BlockSpec((1, n), lambda bi, h, qi: (0, 0))
    return pl.pallas_call(
        functools.partial(_attn_kernel, bq=bq),
        grid=(b, n_hg, s // bq),
        in_specs=[
            pl.BlockSpec((None, bq, hw), lambda bi, h, qi: (bi, qi, h)),
            pl.BlockSpec((None, s, hw), lambda bi, h, qi: (bi, 0, n_hg + h)),
            pl.BlockSpec((None, s, hw), lambda bi, h, qi: (bi, 0, 2 * n_hg + h)),
            vec(hw), vec(hw), vec(DA_QK), vec(DA_QK), vec(DA_QK), vec(DA_QK),
            pl.BlockSpec((DA_V, 1), lambda bi, h, qi: (0, 0)),
        ],
        out_specs=pl.BlockSpec((None, bq, hw), lambda bi, h, qi: (bi, qi, h)),
        out_shape=jax.ShapeDtypeStruct((b, s, DA_HEADS * DA_V), BF16),
        scratch_shapes=[pltpu.VMEM((s, hw), BF16), pltpu.VMEM((hw, s), BF16)],
        compiler_params=_cparams(("parallel", "parallel", "arbitrary")),
        name="diff_attn",
    )(proj3, proj3, proj3, qw, kw, lq1, lk1, lq2, lk2, sw.reshape(DA_V, 1))


GLA_SUB = 16


def _gla_kernel(q_ref, k_ref, v_ref, r_ref, gd_ref, wup_ref, bup_ref, nw_ref, o_ref, st_ref, *, n_chunks):
    st_ref[...] = jnp.zeros_like(st_ref)
    rr = lax.broadcasted_iota(I32, (CHUNK, CHUNK), 0)
    cc = lax.broadcasted_iota(I32, (CHUNK, CHUNK), 1)
    tril = jnp.where(cc <= rr, 1.0, 0.0).astype(BF16)
    sub_row = lax.broadcasted_iota(I32, (GLA_SUB, GLA_DK), 0)
    sub_lane = lax.broadcasted_iota(I32, (GLA_SUB, CHUNK), 1)
    chunk_row = lax.broadcasted_iota(I32, (CHUNK, GLA_DK), 0)
    nt = (((1,), (1,)), ((), ()))
    tn = (((0,), (0,)), ((), ()))

    def chunk(c, _):
        r0 = pl.multiple_of(c * CHUNK, CHUNK)
        z = jnp.dot(gd_ref[pl.ds(r0, CHUNK), :], wup_ref[...], preferred_element_type=F32) + bup_ref[...]
        lg = (jnp.minimum(z, 0.0) - jnp.log(1.0 + jnp.exp(-jnp.abs(z)))) / GLA_TAU
        lg_hi = lg.astype(BF16)
        lg_lo = (lg - lg_hi.astype(F32)).astype(BF16)
        bcum = (jnp.dot(tril, lg_hi, preferred_element_type=F32)
                + jnp.dot(tril, lg_lo, preferred_element_type=F32))
        bprev = bcum - lg
        for hh in range(GLA_HEADS):
            kcol = slice(hh * GLA_DK, (hh + 1) * GLA_DK)
            head_chunk(r0, hh, bcum[:, kcol], bprev[:, kcol])
        return 0

    def head_chunk(r0, hh, bcum, bprev):
        kcol = slice(hh * GLA_DK, (hh + 1) * GLA_DK)
        vcol = slice(hh * GLA_DV, (hh + 1) * GLA_DV)
        q = q_ref[pl.ds(r0, CHUNK), kcol].astype(F32) * (GLA_DK ** -0.5)
        k = k_ref[pl.ds(r0, CHUNK), kcol].astype(F32)
        v = v_ref[pl.ds(r0, CHUNK), vcol]
        st = st_ref[hh]
        o_inter = lax.dot_general((q * jnp.exp(bcum)).astype(BF16), st.astype(BF16), nt,
                                  preferred_element_type=F32)
        a_rows = []
        for blk in range(CHUNK // GLA_SUB):
            lo = blk * GLA_SUB
            hi = lo + GLA_SUB
            b_blk = bcum[lo:hi]
            q_blk = q[lo:hi]
            if blk > 0:
                ref = bprev[lo:lo + 1]
                qt = (q_blk * jnp.exp(b_blk - ref)).astype(BF16)
                kt = (k * jnp.exp(jnp.where(chunk_row < lo, ref - bcum, -jnp.inf))).astype(BF16)
                a_blk = lax.dot_general(qt, kt, nt, preferred_element_type=F32)
            else:
                a_blk = jnp.zeros((GLA_SUB, CHUNK), F32)
            for jj in range(GLA_SUB):
                dlt = jnp.where(sub_row >= jj, b_blk - bcum[lo + jj:lo + jj + 1], -jnp.inf)
                col = jnp.sum(q_blk * k[lo + jj:lo + jj + 1] * jnp.exp(dlt), axis=-1, keepdims=True)
                a_blk = a_blk + jnp.where(sub_lane == lo + jj, col, 0.0)
            a_rows.append(a_blk.astype(BF16))
        o = o_inter + jnp.dot(jnp.concatenate(a_rows, axis=0), v, preferred_element_type=F32)
        ms = jnp.mean(o * o, axis=-1, keepdims=True)
        rg = r_ref[pl.ds(r0, CHUNK), vcol].astype(F32)
        y = o * lax.rsqrt(ms + EPS) * nw_ref[...] * (rg / (1.0 + jnp.exp(-rg)))
        o_ref[pl.ds(r0, CHUNK), vcol] = y.astype(o_ref.dtype)
        b_last = bcum[CHUNK - 1:CHUNK]
        k_dec = (k * jnp.exp(b_last - bcum)).astype(BF16)
        st_ref[hh] = st * jnp.exp(b_last) + lax.dot_general(v, k_dec, tn, preferred_element_type=F32)

    lax.fori_loop(0, n_chunks, chunk, 0)


def _gla(proj3, gd3, wup_pad, bup, nw):
    b, s, _ = proj3.shape
    kw = GLA_HEADS * GLA_DK
    vw = GLA_HEADS * GLA_DV
    q0 = 3 * DA_HEADS * DA_V // kw
    k0 = q0 + 1
    v0 = (k0 + 1) * kw // vw
    r0 = v0 + 1
    return pl.pallas_call(
        functools.partial(_gla_kernel, n_chunks=s // CHUNK),
        grid=(b,),
        in_specs=[
            pl.BlockSpec((None, s, kw), lambda bi: (bi, 0, q0)),
            pl.BlockSpec((None, s, kw), lambda bi: (bi, 0, k0)),
            pl.BlockSpec((None, s, vw), lambda bi: (bi, 0, v0)),
            pl.BlockSpec((None, s, vw), lambda bi: (bi, 0, r0)),
            pl.BlockSpec((None, s, LANE), lambda bi: (bi, 0, 0)),
            pl.BlockSpec((LANE, kw), lambda bi: (0, 0)),
            pl.BlockSpec((1, kw), lambda bi: (0, 0)),
            pl.BlockSpec((1, GLA_DV), lambda bi: (0, 0)),
        ],
        out_specs=pl.BlockSpec((None, s, vw), lambda bi: (bi, 0, 0)),
        out_shape=jax.ShapeDtypeStruct((b, s, vw), BF16),
        scratch_shapes=[pltpu.VMEM((GLA_HEADS, GLA_DV, GLA_DK), F32)],
        compiler_params=_cparams(("parallel",)),
        name="gla",
    )(proj3, proj3, proj3, proj3, gd3, wup_pad, bup, nw)


def _out_router_kernel(x_ref, a_ref, b_ref, wo_ref, nw_ref, rwt_ref, rb_ref,
                       h_ref, hn_ref, slot_ref, w_ref, nch_ref):
    tm = x_ref.shape[0]
    half = a_ref.shape[1]
    h = (x_ref[...]
         + jnp.dot(a_ref[...], wo_ref[0:half, :], preferred_element_type=F32)
         + jnp.dot(b_ref[...], wo_ref[half:2 * half, :], preferred_element_type=F32))
    h_ref[...] = h
    ms = jnp.mean(h * h, axis=-1, keepdims=True)
    hn = h * lax.rsqrt(ms + EPS) * nw_ref[...]
    hn_hi = hn.astype(BF16)
    hn_ref[...] = hn_hi
    hn_lo = (hn - hn_hi.astype(F32)).astype(BF16)
    rwt = rwt_ref[...]
    rw_hi = rwt.astype(BF16)
    rw_lo = (rwt - rw_hi.astype(F32)).astype(BF16)
    nt = (((1,), (1,)), ((), ()))
    logits = (lax.dot_general(rw_hi, hn_hi, nt, preferred_element_type=F32)
              + lax.dot_general(rw_hi, hn_lo, nt, preferred_element_type=F32)
              + lax.dot_general(rw_lo, hn_hi, nt, preferred_element_type=F32)
              + rb_ref[...])
    e_iota = lax.broadcasted_iota(I32, (N_EXPERTS, tm), 0).astype(F32)
    vals, hots = [], []
    for _ in range(TOP_K):
        m = jnp.max(logits, axis=0, keepdims=True)
        idx = jnp.min(jnp.where(logits == m, e_iota, float(N_EXPERTS)), axis=0, keepdims=True)
        hot = e_iota == idx
        vals.append(m)
        hots.append(hot)
        logits = jnp.where(hot, -jnp.inf, logits)
    exps = [jnp.exp(v - vals[0]) for v in vals]
    denom = exps[0] + exps[1] + exps[2] + exps[3]
    for kk in range(TOP_K):
        w_ref[kk:kk + 1, :] = exps[kk] / denom
    sel = jnp.zeros((N_EXPERTS, tm), F32)
    for hot in hots:
        sel = sel + jnp.where(hot, 1.0, 0.0)
    tr = lax.broadcasted_iota(I32, (tm, tm), 0)
    tc = lax.broadcasted_iota(I32, (tm, tm), 1)
    upper = jnp.where(tr < tc, 1.0, 0.0).astype(BF16)
    rank = jnp.dot(sel.astype(BF16), upper, preferred_element_type=F32)
    cnt = jnp.sum(sel, axis=1, keepdims=True)
    nch = jnp.floor((cnt + (ROW_CHUNK - 1)) * (1.0 / ROW_CHUNK))
    nch_b = jnp.broadcast_to(nch, (N_EXPERTS, LANE))
    er = lax.broadcasted_iota(I32, (N_EXPERTS, N_EXPERTS), 0)
    ec = lax.broadcasted_iota(I32, (N_EXPERTS, N_EXPERTS), 1)
    lower = jnp.where(ec < er, 1.0, 0.0).astype(BF16)
    seg = jnp.dot(lower, nch_b.astype(BF16), preferred_element_type=F32)
    pos = seg[:, 0:1] * ROW_CHUNK + rank
    for kk in range(TOP_K):
        slot_ref[kk:kk + 1, :] = jnp.sum(jnp.where(hots[kk], pos, 0.0), axis=0, keepdims=True).astype(I32)
    nch_ref[...] = nch_b.astype(I32)


def _out_router(x2, out_a, out_b, wo_bf, nw, rwt, rb):
    t = x2.shape[0]
    tm = TOK_TILE
    half = out_a.shape[1]
    nt = t // tm
    return pl.pallas_call(
        _out_router_kernel,
        grid=(nt,),
        in_specs=[
            pl.BlockSpec((tm, D_MODEL), lambda i: (i, 0)),
            pl.BlockSpec((tm, half), lambda i: (i, 0)),
            pl.BlockSpec((tm, half), lambda i: (i, 0)),
            pl.BlockSpec((2 * half, D_MODEL), lambda i: (0, 0)),
            pl.BlockSpec((1, D_MODEL), lambda i: (0, 0)),
            pl.BlockSpec((N_EXPERTS, D_MODEL), lambda i: (0, 0)),
            pl.BlockSpec((N_EXPERTS, 1), lambda i: (0, 0)),
        ],
        out_specs=[
            pl.BlockSpec((tm, D_MODEL), lambda i: (i, 0)),
            pl.BlockSpec((tm, D_MODEL), lambda i: (i, 0)),
            pl.BlockSpec((TOP_K, tm), lambda i: (0, i)),
            pl.BlockSpec((TOP_K, tm), lambda i: (0, i)),
            pl.BlockSpec((N_EXPERTS, LANE), lambda i: (i, 0)),
        ],
        out_shape=[
            jax.ShapeDtypeStruct((t, D_MODEL), F32),
            jax.ShapeDtypeStruct((t, D_MODEL), BF16),
            jax.ShapeDtypeStruct((TOP_K, t), I32),
            jax.ShapeDtypeStruct((TOP_K, t), F32),
            jax.ShapeDtypeStruct((nt * N_EXPERTS, LANE), I32),
        ],
        compiler_params=_cparams(("parallel",)),
        name="out_router",
    )(x2, out_a, out_b, wo_bf, nw, rwt, rb)


def _routing_tables(nch, n_row_tiles):
    cpt = MOE_TM // ROW_CHUNK
    seg_src = jnp.cumsum(nch, axis=1) - nch
    rows_e = jnp.sum(nch, axis=0)
    tiles_e = (rows_e + cpt - 1) // cpt
    cum_tiles = jnp.cumsum(tiles_e)
    base_e = (cum_tiles - tiles_e) * cpt
    seg_dst = base_e[None, :] + jnp.cumsum(nch, axis=0) - nch
    gap_start = base_e + rows_e
    cps = MOE_SUB // ROW_CHUNK
    gap_len = (rows_e + cps - 1) // cps * cps - rows_e
    n_used = cum_tiles[-1]
    tile_ids = jnp.arange(n_row_tiles, dtype=I32)
    tile_e = jnp.sum((cum_tiles[None, :] <= jnp.minimum(tile_ids, n_used - 1)[:, None]).astype(I32), axis=1)
    tile_e = jnp.minimum(tile_e, N_EXPERTS - 1)
    first = jnp.concatenate([jnp.ones((1,), I32), (tile_e[1:] != tile_e[:-1]).astype(I32)])
    e_ids = jnp.arange(N_EXPERTS, dtype=I32)
    later = (e_ids[None, :] > e_ids[:, None]) & (tiles_e[None, :] > 0)
    next_e = jnp.min(jnp.where(later, e_ids[None, :], N_EXPERTS), axis=1)
    next_e = jnp.where(next_e == N_EXPERTS, -1, next_e)
    own = (tile_e[:, None] == e_ids[None, :]).astype(I32)
    pick = lambda per_expert: jnp.sum(own * per_expert[None, :], axis=1)
    tile_next = pick(next_e)
    tile_k = tile_ids - pick(cum_tiles - tiles_e)
    tile_rows = jnp.clip(pick(rows_e) * ROW_CHUNK - tile_k * MOE_TM, 0, MOE_TM)
    tot_gather = jnp.sum(nch, axis=1)
    is_last = jnp.arange(nch.shape[0], dtype=I32) == nch.shape[0] - 1
    tot_scatter = tot_gather + jnp.where(is_last, jnp.sum(gap_len), 0)
    flat = lambda a: a.reshape(-1).astype(I32)
    return (flat(seg_src), flat(seg_dst), flat(nch), flat(gap_start), flat(gap_len),
            flat(tot_scatter), flat(tot_gather),
            (tile_e.astype(I32), first, tile_next.astype(I32), tile_rows.astype(I32),
             n_used.reshape(1).astype(I32)))


def _chunk_rows(ref, chunk_idx):
    return ref.at[pl.ds(pl.multiple_of(chunk_idx * ROW_CHUNK, ROW_CHUNK), ROW_CHUNK), :]


def _compact_kernel(src_ref, dst_ref, nch_ref, gap_start_ref, gap_len_ref, tot_ref,
                    x_ref, slot_ref, w_ref, xs_ref, buf_ref, sems):
    tt = pl.program_id(0)
    n_tt = pl.num_programs(0)
    cur = tt & 1
    x = x_ref[...]
    s0, s1, s2, s3 = (slot_ref[kk:kk + 1, :] for kk in range(TOP_K))
    w0, w1, w2, w3 = (w_ref[kk:kk + 1, :] for kk in range(TOP_K))
    lane = lax.broadcasted_iota(I32, (SLOT_BLK, LANE), 1)
    for blk in range(SLOTS // SLOT_BLK):
        rows = slice(blk * SLOT_BLK, (blk + 1) * SLOT_BLK)
        sid = lax.broadcasted_iota(I32, (SLOT_BLK, TOK_TILE), 0) + blk * SLOT_BLK
        hit = jnp.where(sid == s0, 1.0, jnp.where(sid == s1, 1.0, jnp.where(sid == s2, 1.0,
                        jnp.where(sid == s3, 1.0, 0.0))))
        buf_ref[cur, rows, 0:D_MODEL] = jnp.dot(hit.astype(BF16), x, preferred_element_type=F32).astype(BF16)
        wsel = jnp.where(sid == s0, w0, jnp.where(sid == s1, w1, jnp.where(sid == s2, w2,
                         jnp.where(sid == s3, w3, 0.0))))
        wrow = jnp.sum(wsel, axis=1, keepdims=True)
        t0 = wrow.astype(BF16).astype(F32)
        t1 = (wrow - t0).astype(BF16).astype(F32)
        t2 = wrow - t0 - t1
        buf_ref[cur, rows, D_MODEL:XS_COLS] = jnp.where(
            lane == 0, t0, jnp.where(lane == 1, t1, jnp.where(lane == 2, t2, 0.0))).astype(BF16)

    def copy(slot, src_chunk, dst_chunk):
        return pltpu.make_async_copy(_chunk_rows(buf_ref.at[slot], src_chunk), _chunk_rows(xs_ref, dst_chunk),
                                     sems.at[slot])

    def per_expert(e, _):
        n = nch_ref[tt * N_EXPERTS + e]
        src0 = src_ref[tt * N_EXPERTS + e]
        dst0 = dst_ref[tt * N_EXPERTS + e]

        def issue(kk, _):
            copy(cur, src0 + kk, dst0 + kk).start()
            return 0

        lax.fori_loop(0, n, issue, 0)
        return 0

    lax.fori_loop(0, N_EXPERTS, per_expert, 0)

    zero_chunk = SLOTS // ROW_CHUNK - 1

    def per_gap(e, _):
        n = jnp.where(tt == n_tt - 1, gap_len_ref[e], 0)
        g0 = gap_start_ref[e]

        def issue(kk, _):
            copy(cur, zero_chunk, g0 + kk).start()
            return 0

        lax.fori_loop(0, n, issue, 0)
        return 0

    lax.fori_loop(0, N_EXPERTS, per_gap, 0)

    def drain(slot, count):
        def one(_, c):
            copy(slot, 0, 0).wait()
            return c

        lax.fori_loop(0, count, one, 0)

    @pl.when(tt > 0)
    def _():
        drain(1 - cur, tot_ref[tt - 1])

    @pl.when(tt == n_tt - 1)
    def _():
        drain(cur, tot_ref[tt])


def _compact(tables, hn2, slots, top_w, n_rows):
    seg_src, seg_dst, nch, gap_start, gap_len, tot = tables
    t = hn2.shape[0]
    return pl.pallas_call(
        _compact_kernel,
        grid_spec=pltpu.PrefetchScalarGridSpec(
            num_scalar_prefetch=6,
            grid=(t // TOK_TILE,),
            in_specs=[
                pl.BlockSpec((TOK_TILE, D_MODEL), lambda i, *_: (i, 0)),
                pl.BlockSpec((TOP_K, TOK_TILE), lambda i, *_: (0, i)),
                pl.BlockSpec((TOP_K, TOK_TILE), lambda i, *_: (0, i)),
            ],
            out_specs=pl.BlockSpec(memory_space=pl.ANY),
            scratch_shapes=[pltpu.VMEM((2, SLOTS, XS_COLS), BF16), pltpu.SemaphoreType.DMA((2,))],
        ),
        out_shape=jax.ShapeDtypeStruct((n_rows, XS_COLS), BF16),
        compiler_params=_cparams(("arbitrary",)),
        name="compact",
    )(seg_src, seg_dst, nch, gap_start, gap_len, tot, hn2, slots, top_w)


def _per_fill(rows, is_first, compute):
    for n in range(MOE_SUB, MOE_TM + 1, MOE_SUB):
        fits = (rows > n - MOE_SUB) & (rows <= n)

        @pl.when(fits & is_first)
        def _():
            compute(n, True)

        @pl.when(fits & jnp.logical_not(is_first))
        def _():
            compute(n, False)


def _moe_up_kernel(te_ref, first_ref, next_ref, rows_ref, nu_ref, x_ref, wg_hbm, wu_hbm, bg_ref, bu_ref, h_ref,
                   wg_st, wu_st, wg_bf, wu_bf, sems):
    c = pl.program_id(0)
    i = pl.program_id(1)

    def fetch(e, cc):
        col = pl.ds(pl.multiple_of(cc * MOE_FC, MOE_FC), MOE_FC)
        return (pltpu.make_async_copy(wg_hbm.at[e, :, col], wg_st, sems.at[0]),
                pltpu.make_async_copy(wu_hbm.at[e, :, col], wu_st, sems.at[1]))

    @pl.when((c == 0) & (i == 0))
    def _():
        for cp in fetch(te_ref[0], 0):
            cp.start()

    @pl.when(i < nu_ref[0])
    def _():
        is_first = first_ref[i] == 1
        nxt = next_ref[i]
        wraps = nxt < 0
        last_pass = c + 1 == pl.num_programs(0)
        nxt_e = jnp.where(wraps, te_ref[0], nxt)
        nxt_c = jnp.where(wraps, jnp.where(last_pass, 0, c + 1), c)

        @pl.when(is_first)
        def _():
            for cp in fetch(te_ref[i], c):
                cp.wait()

        def compute(n, casting):
            x = x_ref[0:n, :]

            def cast(s):
                cols = slice(s * MOE_NB, (s + 1) * MOE_NB)
                wg_bf[:, cols] = wg_st[:, cols].astype(BF16)
                wu_bf[:, cols] = wu_st[:, cols].astype(BF16)

            n_sub = MOE_FC // MOE_NB
            if casting:
                cast(0)
            for s in range(n_sub):
                if casting and s + 1 < n_sub:
                    cast(s + 1)
                if casting and s + 1 == n_sub:
                    for cp in fetch(nxt_e, nxt_c):
                        cp.start()
                cols = slice(s * MOE_NB, (s + 1) * MOE_NB)
                gate = jnp.minimum(jnp.dot(x, wg_bf[:, cols], preferred_element_type=F32) + bg_ref[:, cols],
                                   SWIGLU_LIMIT)
                up = jnp.clip(jnp.dot(x, wu_bf[:, cols], preferred_element_type=F32) + bu_ref[:, cols],
                              -SWIGLU_LIMIT, SWIGLU_LIMIT)
                act = (up + 1.0) * gate * (1.0 / (1.0 + jnp.exp(-SWIGLU_ALPHA * gate)))
                h_ref[0:n, cols] = act.astype(h_ref.dtype)

        _per_fill(rows_ref[i], is_first, compute)

        @pl.when(is_first & wraps & last_pass)
        def _():
            for cp in fetch(te_ref[0], 0):
                cp.wait()


def _moe_up(tiles, xs, wg, wu, bg, bu):
    n_rows = xs.shape[0]
    n_tiles = n_rows // MOE_TM
    row = lambda c, i, te, fi, nx, rw, nu: (jnp.minimum(i, nu[0] - 1), 0)
    bsel = lambda c, i, te, fi, nx, rw, nu: (te[i], 0, c)
    return pl.pallas_call(
        _moe_up_kernel,
        grid_spec=pltpu.PrefetchScalarGridSpec(
            num_scalar_prefetch=5,
            grid=(D_FF // MOE_FC, n_tiles),
            in_specs=[
                pl.BlockSpec((MOE_TM, D_MODEL), row),
                pl.BlockSpec(memory_space=pl.ANY),
                pl.BlockSpec(memory_space=pl.ANY),
                pl.BlockSpec((None, 1, MOE_FC), bsel),
                pl.BlockSpec((None, 1, MOE_FC), bsel),
            ],
            out_specs=pl.BlockSpec((MOE_TM, MOE_FC),
                                   lambda c, i, te, fi, nx, rw, nu: (jnp.minimum(i, nu[0] - 1), c)),
            scratch_shapes=[pltpu.VMEM((D_MODEL, MOE_FC), F32), pltpu.VMEM((D_MODEL, MOE_FC), F32),
                            pltpu.VMEM((D_MODEL, MOE_FC), BF16), pltpu.VMEM((D_MODEL, MOE_FC), BF16),
                            pltpu.SemaphoreType.DMA((2,))],
        ),
        out_shape=jax.ShapeDtypeStruct((n_rows, D_FF), BF16),
        compiler_params=_cparams(("arbitrary", "arbitrary")),
        name="moe_up",
    )(*tiles, xs, wg, wu, bg, bu)


def _moe_down_kernel(te_ref, first_ref, next_ref, rows_ref, nu_ref, h_ref, cw_ref, wd_hbm, bd_ref, y_ref,
                     wd_st, wd_bf, sem):
    i = pl.program_id(0)

    def fetch(e):
        return pltpu.make_async_copy(wd_hbm.at[e], wd_st, sem)

    @pl.when(i == 0)
    def _():
        fetch(te_ref[0]).start()

    @pl.when(i < nu_ref[0])
    def _():
        is_first = first_ref[i] == 1
        nxt = next_ref[i]
        nxt_e = jnp.where(nxt < 0, te_ref[0], nxt)

        @pl.when(is_first)
        def _():
            fetch(te_ref[i]).wait()

        def compute(n, casting):
            cw = cw_ref[0:n, :].astype(F32)
            cw = cw[:, 0:1] + cw[:, 1:2] + cw[:, 2:3]
            hrow = h_ref[0:n, :]

            def cast(s):
                cols = slice(s * MOE_NB, (s + 1) * MOE_NB)
                wd_bf[:, cols] = wd_st[:, cols].astype(BF16)

            n_sub = D_MODEL // MOE_NB
            if casting:
                cast(0)
            for s in range(n_sub):
                if casting and s + 1 < n_sub:
                    cast(s + 1)
                if casting and s + 1 == n_sub:
                    fetch(nxt_e).start()
                cols = slice(s * MOE_NB, (s + 1) * MOE_NB)
                y = (jnp.dot(hrow, wd_bf[:, cols], preferred_element_type=F32) + bd_ref[:, cols]) * cw
                y_ref[0:n, cols] = y.astype(y_ref.dtype)

        _per_fill(rows_ref[i], is_first, compute)

        @pl.when(is_first & (nxt < 0))
        def _():
            fetch(te_ref[0]).wait()


def _moe_down(tiles, hs, xs, wd, bd):
    n_rows = hs.shape[0]
    n_tiles = n_rows // MOE_TM
    row = lambda i, te, fi, nx, rw, nu: (jnp.minimum(i, nu[0] - 1), 0)
    return pl.pallas_call(
        _moe_down_kernel,
        grid_spec=pltpu.PrefetchScalarGridSpec(
            num_scalar_prefetch=5,
            grid=(n_tiles,),
            in_specs=[
                pl.BlockSpec((MOE_TM, D_FF), row),
                pl.BlockSpec((MOE_TM, LANE),
                             lambda i, te, fi, nx, rw, nu: (jnp.minimum(i, nu[0] - 1), D_MODEL // LANE)),
                pl.BlockSpec(memory_space=pl.ANY),
                pl.BlockSpec((None, 1, D_MODEL), lambda i, te, fi, nx, rw, nu: (te[i], 0, 0)),
            ],
            out_specs=pl.BlockSpec((MOE_TM, D_MODEL), row),
            scratch_shapes=[pltpu.VMEM((D_FF, D_MODEL), F32), pltpu.VMEM((D_FF, D_MODEL), BF16),
                            pltpu.SemaphoreType.DMA(())],
        ),
        out_shape=jax.ShapeDtypeStruct((n_rows, D_MODEL), BF16),
        compiler_params=_cparams(("arbitrary",)),
        name="moe_down",
    )(*tiles, hs, xs, wd, bd)


def _combine_kernel(src_ref, dst_ref, nch_ref, tot_ref, h_ref, slot_ref, ys_ref, o_ref, buf_ref, sems):
    tt = pl.program_id(0)
    n_tt = pl.num_programs(0)
    cur = tt & 1

    def copy(slot, sorted_chunk, local_chunk):
        return pltpu.make_async_copy(_chunk_rows(ys_ref, sorted_chunk), _chunk_rows(buf_ref.at[slot], local_chunk),
                                     sems.at[slot])

    def gather(tile, slot):
        def per_expert(e, _):
            n = nch_ref[tile * N_EXPERTS + e]
            loc0 = src_ref[tile * N_EXPERTS + e]
            srt0 = dst_ref[tile * N_EXPERTS + e]

            def issue(kk, _):
                copy(slot, srt0 + kk, loc0 + kk).start()
                return 0

            lax.fori_loop(0, n, issue, 0)
            return 0

        lax.fori_loop(0, N_EXPERTS, per_expert, 0)

    @pl.when(tt == 0)
    def _():
        buf_ref[...] = jnp.zeros_like(buf_ref)
        gather(0, 0)

    @pl.when(tt + 1 < n_tt)
    def _():
        gather(tt + 1, 1 - cur)

    def one(_, c):
        copy(cur, 0, 0).wait()
        return c

    lax.fori_loop(0, tot_ref[tt], one, 0)

    o_ref[...] = h_ref[...]
    for blk in range(SLOTS // SLOT_BLK):
        sid = lax.broadcasted_iota(I32, (TOK_TILE, SLOT_BLK), 1) + blk * SLOT_BLK
        hit = jnp.zeros((TOK_TILE, SLOT_BLK), F32)
        for kk in range(TOP_K):
            hit = jnp.where(sid == slot_ref[:, kk:kk + 1], 1.0, hit)
        yb = buf_ref[cur, blk * SLOT_BLK:(blk + 1) * SLOT_BLK, :]
        o_ref[...] += jnp.dot(hit.astype(BF16), yb, preferred_element_type=F32)


def _combine(tables, h, slots_t, ys):
    seg_src, seg_dst, nch, tot = tables
    t = h.shape[0]
    return pl.pallas_call(
        _combine_kernel,
        grid_spec=pltpu.PrefetchScalarGridSpec(
            num_scalar_prefetch=4,
            grid=(t // TOK_TILE,),
            in_specs=[
                pl.BlockSpec((TOK_TILE, D_MODEL), lambda i, *_: (i, 0)),
                pl.BlockSpec((TOK_TILE, TOP_K), lambda i, *_: (i, 0)),
                pl.BlockSpec(memory_space=pl.ANY),
            ],
            out_specs=pl.BlockSpec((TOK_TILE, D_MODEL), lambda i, *_: (i, 0)),
            scratch_shapes=[pltpu.VMEM((2, SLOTS, D_MODEL), BF16), pltpu.SemaphoreType.DMA((2,))],
        ),
        out_shape=jax.ShapeDtypeStruct((t, D_MODEL), F32),
        compiler_params=_cparams(("arbitrary",)),
        name="combine",
    )(seg_src, seg_dst, nch, tot, h, slots_t, ys)


def kernel(x, attn_norm_w, w_in, da_q_norm_w, da_k_norm_w, da_lambda_q1, da_lambda_k1, da_lambda_q2,
           da_lambda_k2, da_subln_w, gla_gate_up_w, gla_gate_up_b, gla_norm_w, w_out, ffn_norm_w,
           router_w, router_b, exp_w_gate, exp_b_gate, exp_w_up, exp_b_up, exp_w_down, exp_b_down):
    b, s, d = x.shape
    t = b * s
    assert d == D_MODEL and t % TOK_TILE == 0 and s % CHUNK == 0 and attn_norm_w.shape[0] == 1
    x2 = x.reshape(t, d)

    w_tail = jnp.pad(w_in[0, :, IN_COLS_MAIN:].astype(BF16), ((0, 0), (0, LANE - GLA_RANK)))
    hn, gd = _rms_gd(x2, attn_norm_w, w_tail)
    proj3 = _in_proj(hn, w_in.reshape(D_MODEL, IN_COLS)).reshape(b, s, IN_COLS_MAIN)

    out_a = _diff_attn(proj3, jnp.tile(da_q_norm_w, (1, 2 * ATT_HP)), jnp.tile(da_k_norm_w, (1, 2 * ATT_HP)),
                       da_lambda_q1, da_lambda_k1, da_lambda_q2, da_lambda_k2, da_subln_w)
    wup_pad = jnp.pad(gla_gate_up_w[0].astype(BF16), ((0, LANE - GLA_RANK), (0, 0)))
    out_b = _gla(proj3, gd.reshape(b, s, LANE), wup_pad, gla_gate_up_b, gla_norm_w)

    h, hn2, slots, top_w, nch_b = _out_router(
        x2, out_a.reshape(t, -1), out_b.reshape(t, -1), w_out[0].astype(BF16), ffn_norm_w,
        router_w[0].T, router_b.reshape(N_EXPERTS, 1))

    n_tok_tiles = t // TOK_TILE
    nch = nch_b[:, 0].reshape(n_tok_tiles, N_EXPERTS)
    n_row_tiles = (TOP_K * t + n_tok_tiles * N_EXPERTS * (ROW_CHUNK - 1)) // MOE_TM + N_EXPERTS
    seg_src, seg_dst, nch_f, gap_start, gap_len, tot_scatter, tot_gather, tiles = _routing_tables(nch, n_row_tiles)

    xs = _compact((seg_src, seg_dst, nch_f, gap_start, gap_len, tot_scatter), hn2, slots, top_w,
                  n_row_tiles * MOE_TM)
    hs = _moe_up(tiles, xs,
                 exp_w_gate.reshape(N_EXPERTS, D_MODEL, D_FF), exp_w_up.reshape(N_EXPERTS, D_MODEL, D_FF),
                 exp_b_gate.reshape(N_EXPERTS, 1, D_FF), exp_b_up.reshape(N_EXPERTS, 1, D_FF))
    ys = _moe_down(tiles, hs, xs, exp_w_down.reshape(N_EXPERTS, D_FF, D_MODEL),
                   exp_b_down.reshape(N_EXPERTS, 1, D_MODEL))
    out = _combine((seg_src, seg_dst, nch_f, tot_gather), h, slots.T, ys)
    return out.reshape(b, s, d)
```

```python
import functools
import math

import jax
import jax.numpy as jnp
from jax import lax
from jax.experimental import pallas as pl
from jax.experimental.pallas import tpu as pltpu

F32 = jnp.float32
BF16 = jnp.bfloat16
I32 = jnp.int32

D_MODEL = 2048
CHUNK = 64
EPS = 1e-6
DA_HEADS = 8
DA_QK = 64
DA_V = 128
GLA_HEADS = 4
GLA_DK = 128
GLA_DV = 256
GLA_RANK = 16
GLA_TAU = 16.0
N_EXPERTS = 32
TOP_K = 4
D_FF = 2048
SWIGLU_LIMIT = 7.0
SWIGLU_ALPHA = 1.702
LAMBDA_INIT = 0.8 - 0.6 * math.exp(-0.3 * 0)

IN_COLS = 6160
IN_COLS_MAIN = IN_COLS - GLA_RANK
LANE = 128
ROW_CHUNK = 16

IN_TM = 1024
IN_TN = 1024
ATT_BQ = 256
ATT_HP = 4
ATT_SAFE_BOUND = 45.0
TOK_TILE = 512
SLOTS = TOP_K * TOK_TILE + N_EXPERTS * ROW_CHUNK
SLOT_BLK = 256
XS_COLS = D_MODEL + LANE
MOE_TM = 512
MOE_SUB = 128
MOE_FC = 1024
MOE_NB = 256
VMEM_LIMIT = 56 * 1024 * 1024


def _cparams(sem):
    return pltpu.CompilerParams(dimension_semantics=sem, vmem_limit_bytes=VMEM_LIMIT)


def _rms_gd_kernel(x_ref, nw_ref, wt_ref, hn_ref, gd_ref):
    x = x_ref[...]
    ms = jnp.mean(x * x, axis=-1, keepdims=True)
    hn = (x * lax.rsqrt(ms + EPS) * nw_ref[...]).astype(BF16)
    hn_ref[...] = hn
    gd_ref[...] = jnp.dot(hn, wt_ref[...], preferred_element_type=F32).astype(gd_ref.dtype)


def _rms_gd(x2, norm_w, w_tail):
    t = x2.shape[0]
    tm = min(IN_TM, t)
    return pl.pallas_call(
        _rms_gd_kernel,
        grid=(t // tm,),
        in_specs=[
            pl.BlockSpec((tm, D_MODEL), lambda i: (i, 0)),
            pl.BlockSpec((1, D_MODEL), lambda i: (0, 0)),
            pl.BlockSpec((D_MODEL, LANE), lambda i: (0, 0)),
        ],
        out_specs=[pl.BlockSpec((tm, D_MODEL), lambda i: (i, 0)), pl.BlockSpec((tm, LANE), lambda i: (i, 0))],
        out_shape=[jax.ShapeDtypeStruct((t, D_MODEL), BF16), jax.ShapeDtypeStruct((t, LANE), BF16)],
        compiler_params=_cparams(("parallel",)),
        name="rms_gd",
    )(x2, norm_w, w_tail)


def _in_proj_kernel(hn_ref, w_ref, o_ref, w_bf):
    @pl.when(pl.program_id(1) == 0)
    def _():
        w_bf[...] = w_ref[...].astype(BF16)

    o_ref[...] = jnp.dot(hn_ref[...], w_bf[...], preferred_element_type=F32).astype(o_ref.dtype)


def _in_proj(hn, w_in):
    t = hn.shape[0]
    tm = min(IN_TM, t)
    return pl.pallas_call(
        _in_proj_kernel,
        grid=(IN_COLS_MAIN // IN_TN, t // tm),
        in_specs=[
            pl.BlockSpec((tm, D_MODEL), lambda j, i: (i, 0)),
            pl.BlockSpec((D_MODEL, IN_TN), lambda j, i: (0, j)),
        ],
        out_specs=pl.BlockSpec((tm, IN_TN), lambda j, i: (i, j)),
        out_shape=jax.ShapeDtypeStruct((t, IN_COLS_MAIN), BF16),
        scratch_shapes=[pltpu.VMEM((D_MODEL, IN_TN), BF16)],
        compiler_params=_cparams(("arbitrary", "arbitrary")),
        name="in_proj",
    )(hn, w_in)


def _attn_kernel(q_ref, k_ref, v_ref, qw_ref, kw_ref, lq1_ref, lk1_ref, lq2_ref, lk2_ref, sw_ref,
                 o_ref, kn_ref, vt_ref, *, bq):
    qi = pl.program_id(2)
    first_map = lax.broadcasted_iota(I32, (1, LANE), 1) < DA_QK
    n_grp = 2 * bq // LANE
    shift = CHUNK.bit_length() - 1

    def qk_norm(x, w):
        x2 = x * x
        scale = []
        for hh in range(ATT_HP):
            xh = x2[:, hh * LANE:(hh + 1) * LANE]
            s1 = jnp.sum(jnp.where(first_map, xh, 0.0), axis=-1, keepdims=True)
            s2 = jnp.sum(jnp.where(first_map, 0.0, xh), axis=-1, keepdims=True)
            scale.append(jnp.where(first_map, lax.rsqrt(s1 / DA_QK + EPS), lax.rsqrt(s2 / DA_QK + EPS)))
        return x * jnp.concatenate(scale, axis=1) * w

    @pl.when(qi == 0)
    def _():
        kn_ref[...] = qk_norm(k_ref[...].astype(F32), kw_ref[...]).astype(BF16)
        vt_ref[...] = v_ref[...].astype(F32).T.astype(BF16)

    q = qk_norm(q_ref[...].astype(F32), qw_ref[...]) * (DA_QK ** -0.5 * math.log2(math.e))
    q_t = []
    for hh in range(ATT_HP):
        qh = q[:, hh * LANE:(hh + 1) * LANE]
        q_t.append(jnp.concatenate([jnp.where(first_map, qh, 0.0), jnp.where(first_map, 0.0, qh)],
                                   axis=0).T.astype(BF16))

    bound = jnp.max(jnp.abs(qw_ref[...] * kw_ref[...]), axis=-1, keepdims=True) * (
        DA_QK ** 0.5 * math.log2(math.e) * 1.02)

    def step(j, carry, masked, fixed_shift):
        r0 = pl.multiple_of(j * bq, bq)
        scores = [jnp.dot(kn_ref[pl.ds(r0, bq), hh * LANE:(hh + 1) * LANE], q_t[hh], preferred_element_type=F32)
                  for hh in range(ATT_HP)]
        out = []
        for hh in range(ATT_HP):
            ms, ls, alphas, ps = [], [], [], []
            for g in range(n_grp):
                m, l, _ = carry[hh][g]
                s = scores[hh][:, g * LANE:(g + 1) * LANE]
                if masked:
                    key_chunk = lax.broadcasted_iota(I32, (bq, LANE), 0) >> shift
                    qry_chunk = (lax.broadcasted_iota(I32, (bq, LANE), 1) + (g * LANE) % bq) >> shift
                    s = jnp.where(key_chunk <= qry_chunk, s, -jnp.inf)
                if fixed_shift:
                    p = jnp.exp2(s - bound)
                    ms.append(m)
                    ls.append(l + jnp.sum(p, axis=0, keepdims=True))
                else:
                    m_new = jnp.maximum(m, jnp.max(s, axis=0, keepdims=True))
                    alpha = jnp.exp2(m - m_new)
                    p = jnp.exp2(s - m_new)
                    ms.append(m_new)
                    ls.append(alpha * l + jnp.sum(p, axis=0, keepdims=True))
                    alphas.append(alpha)
                ps.append(p.astype(BF16))
            pv = jnp.dot(vt_ref[hh * DA_V:(hh + 1) * DA_V, pl.ds(r0, bq)], jnp.concatenate(ps, axis=1),
                         preferred_element_type=F32)
            out.append(tuple((ms[g], ls[g],
                              (carry[hh][g][2] if fixed_shift else alphas[g] * carry[hh][g][2])
                              + pv[:, g * LANE:(g + 1) * LANE]) for g in range(n_grp)))
        return tuple(out)

    lam = (jnp.exp(jnp.sum(lq1_ref[...] * lk1_ref[...], axis=-1, keepdims=True))
           - jnp.exp(jnp.sum(lq2_ref[...] * lk2_ref[...], axis=-1, keepdims=True)) + LAMBDA_INIT)

    def attend(fixed_shift):
        init = tuple(tuple((jnp.full((1, LANE), -jnp.inf, F32), jnp.zeros((1, LANE), F32),
                            jnp.zeros((DA_V, LANE), F32)) for _ in range(n_grp)) for _ in range(ATT_HP))
        carry = lax.fori_loop(0, qi, lambda j, c: step(j, c, False, fixed_shift), init)
        carry = step(qi, carry, True, fixed_shift)
        half = n_grp // 2
        for hh in range(ATT_HP):
            for g in range(half):
                o1 = carry[hh][g][2] / carry[hh][g][1]
                o2 = carry[hh][g + half][2] / carry[hh][g + half][1]
                out = o1 - lam * o2
                ms = jnp.mean(out * out, axis=0, keepdims=True)
                y = out * lax.rsqrt(ms + EPS) * sw_ref[...] * (1.0 - LAMBDA_INIT)
                o_ref[g * LANE:(g + 1) * LANE, hh * DA_V:(hh + 1) * DA_V] = y.T.astype(o_ref.dtype)

    small = jnp.max(bound) <= ATT_SAFE_BOUND

    @pl.when(small)
    def _():
        attend(True)

    @pl.when(jnp.logical_not(small))
    def _():
        attend(False)


def _diff_attn(proj3, qw, kw, lq1, lk1, lq2, lk2, sw):
    b, s, _ = proj3.shape
    bq = min(ATT_BQ, s)
    hw = ATT_HP * LANE
    n_hg = DA_HEADS // ATT_HP
    vec = lambda n: pl.BlockSpec((1, n), lambda bi, h, qi: (0, 0))
    return pl.pallas_call(
        functools.partial(_attn_kernel, bq=bq),
        grid=(b, n_hg, s // bq),
        in_specs=[
            pl.BlockSpec((None, bq, hw), lambda bi, h, qi: (bi, qi, h)),
            pl.BlockSpec((None, s, hw), lambda bi, h, qi: (bi, 0, n_hg + h)),
            pl.BlockSpec((None, s, hw), lambda bi, h, qi: (bi, 0, 2 * n_hg + h)),
            vec(hw), vec(hw), vec(DA_QK), vec(DA_QK), vec(DA_QK), vec(DA_QK),
            pl.BlockSpec((DA_V, 1), lambda bi, h, qi: (0, 0)),
        ],
        out_specs=pl.BlockSpec((None, bq, hw), lambda bi, h, qi: (bi, qi, h)),
        out_shape=jax.ShapeDtypeStruct((b, s, DA_HEADS * DA_V), BF16),
        scratch_shapes=[pltpu.VMEM((s, hw), BF16), pltpu.VMEM((hw, s), BF16)],
        compiler_params=_cparams(("parallel", "parallel", "arbitrary")),
        name="diff_attn",
    )(proj3, proj3, proj3, qw, kw, lq1, lk1, lq2, lk2, sw.reshape(DA_V, 1))


GLA_SUB = 16


def _gla_kernel(q_ref, k_ref, v_ref, r_ref, gd_ref, wup_ref, bup_ref, nw_ref, o_ref, st_ref, *, n_chunks):
    st_ref[...] = jnp.zeros_like(st_ref)
    rr = lax.broadcasted_iota(I32, (CHUNK, CHUNK), 0)
    cc = lax.broadcasted_iota(I32, (CHUNK, CHUNK), 1)
    tril = jnp.where(cc <= rr, 1.0, 0.0).astype(BF16)
    sub_row = lax.broadcasted_iota(I32, (GLA_SUB, GLA_DK), 0)
    sub_lane = lax.broadcasted_iota(I32, (GLA_SUB, CHUNK), 1)
    chunk_row = lax.broadcasted_iota(I32, (CHUNK, GLA_DK), 0)
    nt = (((1,), (1,)), ((), ()))
    tn = (((0,), (0,)), ((), ()))

    def chunk(c, _):
        r0 = pl.multiple_of(c * CHUNK, CHUNK)
        z = jnp.dot(gd_ref[pl.ds(r0, CHUNK), :], wup_ref[...], preferred_element_type=F32) + bup_ref[...]
        lg = (jnp.minimum(z, 0.0) - jnp.log(1.0 + jnp.exp(-jnp.abs(z)))) / GLA_TAU
        lg_hi = lg.astype(BF16)
        lg_lo = (lg - lg_hi.astype(F32)).astype(BF16)
        bcum = (jnp.dot(tril, lg_hi, preferred_element_type=F32)
                + jnp.dot(tril, lg_lo, preferred_element_type=F32))
        bprev = bcum - lg
        for hh in range(GLA_HEADS):
            kcol = slice(hh * GLA_DK, (hh + 1) * GLA_DK)
            head_chunk(r0, hh, bcum[:, kcol], bprev[:, kcol])
        return 0

    def head_chunk(r0, hh, bcum, bprev):
        kcol = slice(hh * GLA_DK, (hh + 1) * GLA_DK)
        vcol = slice(hh * GLA_DV, (hh + 1) * GLA_DV)
        q = q_ref[pl.ds(r0, CHUNK), kcol].astype(F32) * (GLA_DK ** -0.5)
        k = k_ref[pl.ds(r0, CHUNK), kcol].astype(F32)
        v = v_ref[pl.ds(r0, CHUNK), vcol]
        st = st_ref[hh]
        o_inter = lax.dot_general((q * jnp.exp(bcum)).astype(BF16), st.astype(BF16), nt,
                                  preferred_element_type=F32)
        a_rows = []
        for blk in range(CHUNK // GLA_SUB):
            lo = blk * GLA_SUB
            hi = lo + GLA_SUB
            b_blk = bcum[lo:hi]
            q_blk = q[lo:hi]
            if blk > 0:
                ref = bprev[lo:lo + 1]
                qt = (q_blk * jnp.exp(b_blk - ref)).astype(BF16)
                kt = (k * jnp.exp(jnp.where(chunk_row < lo, ref - bcum, -jnp.inf))).astype(BF16)
                a_blk = lax.dot_general(qt, kt, nt, preferred_element_type=F32)
            else:
                a_blk = jnp.zeros((GLA_SUB, CHUNK), F32)
            for jj in range(GLA_SUB):
                dlt = jnp.where(sub_row >= jj, b_blk - bcum[lo + jj:lo + jj + 1], -jnp.inf)
                col = jnp.sum(q_blk * k[lo + jj:lo + jj + 1] * jnp.exp(dlt), axis=-1, keepdims=True)
                a_blk = a_blk + jnp.where(sub_lane == lo + jj, col, 0.0)
            a_rows.append(a_blk.astype(BF16))
        o = o_inter + jnp.dot(jnp.concatenate(a_rows, axis=0), v, preferred_element_type=F32)
        ms = jnp.mean(o * o, axis=-1, keepdims=True)
        rg = r_ref[pl.ds(r0, CHUNK), vcol].astype(F32)
        y = o * lax.rsqrt(ms + EPS) * nw_ref[...] * (rg / (1.0 + jnp.exp(-rg)))
        o_ref[pl.ds(r0, CHUNK), vcol] = y.astype(o_ref.dtype)
        b_last = bcum[CHUNK - 1:CHUNK]
        k_dec = (k * jnp.exp(b_last - bcum)).astype(BF16)
        st_ref[hh] = st * jnp.exp(b_last) + lax.dot_general(v, k_dec, tn, preferred_element_type=F32)

    lax.fori_loop(0, n_chunks, chunk, 0)


def _gla(proj3, gd3, wup_pad, bup, nw):
    b, s, _ = proj3.shape
    kw = GLA_HEADS * GLA_DK
    vw = GLA_HEADS * GLA_DV
    q0 = 3 * DA_HEADS * DA_V // kw
    k0 = q0 + 1
    v0 = (k0 + 1) * kw // vw
    r0 = v0 + 1
    return pl.pallas_call(
        functools.partial(_gla_kernel, n_chunks=s // CHUNK),
        grid=(b,),
        in_specs=[
            pl.BlockSpec((None, s, kw), lambda bi: (bi, 0, q0)),
            pl.BlockSpec((None, s, kw), lambda bi: (bi, 0, k0)),
            pl.BlockSpec((None, s, vw), lambda bi: (bi, 0, v0)),
            pl.BlockSpec((None, s, vw), lambda bi: (bi, 0, r0)),
            pl.BlockSpec((None, s, LANE), lambda bi: (bi, 0, 0)),
            pl.BlockSpec((LANE, kw), lambda bi: (0, 0)),
            pl.BlockSpec((1, kw), lambda bi: (0, 0)),
            pl.BlockSpec((1, GLA_DV), lambda bi: (0, 0)),
        ],
        out_specs=pl.BlockSpec((None, s, vw), lambda bi: (bi, 0, 0)),
        out_shape=jax.ShapeDtypeStruct((b, s, vw), BF16),
        scratch_shapes=[pltpu.VMEM((GLA_HEADS, GLA_DV, GLA_DK), F32)],
        compiler_params=_cparams(("parallel",)),
        name="gla",
    )(proj3, proj3, proj3, proj3, gd3, wup_pad, bup, nw)


def _out_router_kernel(x_ref, a_ref, b_ref, wo_ref, nw_ref, rwt_ref, rb_ref,
                       h_ref, hn_ref, slot_ref, w_ref, nch_ref):
    tm = x_ref.shape[0]
    half = a_ref.shape[1]
    h = (x_ref[...]
         + jnp.dot(a_ref[...], wo_ref[0:half, :], preferred_element_type=F32)
         + jnp.dot(b_ref[...], wo_ref[half:2 * half, :], preferred_element_type=F32))
    h_ref[...] = h
    ms = jnp.mean(h * h, axis=-1, keepdims=True)
    hn = h * lax.rsqrt(ms + EPS) * nw_ref[...]
    hn_hi = hn.astype(BF16)
    hn_ref[...] = hn_hi
    hn_lo = (hn - hn_hi.astype(F32)).astype(BF16)
    rwt = rwt_ref[...]
    rw_hi = rwt.astype(BF16)
    rw_lo = (rwt - rw_hi.astype(F32)).astype(BF16)
    nt = (((1,), (1,)), ((), ()))
    logits = (lax.dot_general(rw_hi, hn_hi, nt, preferred_element_type=F32)
              + lax.dot_general(rw_hi, hn_lo, nt, preferred_element_type=F32)
              + lax.dot_general(rw_lo, hn_hi, nt, preferred_element_type=F32)
              + rb_ref[...])
    e_iota = lax.broadcasted_iota(I32, (N_EXPERTS, tm), 0).astype(F32)
    vals, hots = [], []
    for _ in range(TOP_K):
        m = jnp.max(logits, axis=0, keepdims=True)
        idx = jnp.min(jnp.where(logits == m, e_iota, float(N_EXPERTS)), axis=0, keepdims=True)
        hot = e_iota == idx
        vals.append(m)
        hots.append(hot)
        logits = jnp.where(hot, -jnp.inf, logits)
    exps = [jnp.exp(v - vals[0]) for v in vals]
    denom = exps[0] + exps[1] + exps[2] + exps[3]
    for kk in range(TOP_K):
        w_ref[kk:kk + 1, :] = exps[kk] / denom
    sel = jnp.zeros((N_EXPERTS, tm), F32)
    for hot in hots:
        sel = sel + jnp.where(hot, 1.0, 0.0)
    tr = lax.broadcasted_iota(I32, (tm, tm), 0)
    tc = lax.broadcasted_iota(I32, (tm, tm), 1)
    upper = jnp.where(tr < tc, 1.0, 0.0).astype(BF16)
    rank = jnp.dot(sel.astype(BF16), upper, preferred_element_type=F32)
    cnt = jnp.sum(sel, axis=1, keepdims=True)
    nch = jnp.floor((cnt + (ROW_CHUNK - 1)) * (1.0 / ROW_CHUNK))
    nch_b = jnp.broadcast_to(nch, (N_EXPERTS, LANE))
    er = lax.broadcasted_iota(I32, (N_EXPERTS, N_EXPERTS), 0)
    ec = lax.broadcasted_iota(I32, (N_EXPERTS, N_EXPERTS), 1)
    lower = jnp.where(ec < er, 1.0, 0.0).astype(BF16)
    seg = jnp.dot(lower, nch_b.astype(BF16), preferred_element_type=F32)
    pos = seg[:, 0:1] * ROW_CHUNK + rank
    for kk in range(TOP_K):
        slot_ref[kk:kk + 1, :] = jnp.sum(jnp.where(hots[kk], pos, 0.0), axis=0, keepdims=True).astype(I32)
    nch_ref[...] = nch_b.astype(I32)


def _out_router(x2, out_a, out_b, wo_bf, nw, rwt, rb):
    t = x2.shape[0]
    tm = TOK_TILE
    half = out_a.shape[1]
    nt = t // tm
    return pl.pallas_call(
        _out_router_kernel,
        grid=(nt,),
        in_specs=[
            pl.BlockSpec((tm, D_MODEL), lambda i: (i, 0)),
            pl.BlockSpec((tm, half), lambda i: (i, 0)),
            pl.BlockSpec((tm, half), lambda i: (i, 0)),
            pl.BlockSpec((2 * half, D_MODEL), lambda i: (0, 0)),
            pl.BlockSpec((1, D_MODEL), lambda i: (0, 0)),
            pl.BlockSpec((N_EXPERTS, D_MODEL), lambda i: (0, 0)),
            pl.BlockSpec((N_EXPERTS, 1), lambda i: (0, 0)),
        ],
        out_specs=[
            pl.BlockSpec((tm, D_MODEL), lambda i: (i, 0)),
            pl.BlockSpec((tm, D_MODEL), lambda i: (i, 0)),
            pl.BlockSpec((TOP_K, tm), lambda i: (0, i)),
            pl.BlockSpec((TOP_K, tm), lambda i: (0, i)),
            pl.BlockSpec((N_EXPERTS, LANE), lambda i: (i, 0)),
        ],
        out_shape=[
            jax.ShapeDtypeStruct((t, D_MODEL), F32),
            jax.ShapeDtypeStruct((t, D_MODEL), BF16),
            jax.ShapeDtypeStruct((TOP_K, t), I32),
            jax.ShapeDtypeStruct((TOP_K, t), F32),
            jax.ShapeDtypeStruct((nt * N_EXPERTS, LANE), I32),
        ],
        compiler_params=_cparams(("parallel",)),
        name="out_router",
    )(x2, out_a, out_b, wo_bf, nw, rwt, rb)


def _routing_tables(nch, n_row_tiles):
    cpt = MOE_TM // ROW_CHUNK
    seg_src = jnp.cumsum(nch, axis=1) - nch
    rows_e = jnp.sum(nch, axis=0)
    tiles_e = (rows_e + cpt - 1) // cpt
    cum_tiles = jnp.cumsum(tiles_e)
    base_e = (cum_tiles - tiles_e) * cpt
    seg_dst = base_e[None, :] + jnp.cumsum(nch, axis=0) - nch
    gap_start = base_e + rows_e
    cps = MOE_SUB // ROW_CHUNK
    gap_len = (rows_e + cps - 1) // cps * cps - rows_e
    n_used = cum_tiles[-1]
    tile_ids = jnp.arange(n_row_tiles, dtype=I32)
    tile_e = jnp.sum((cum_tiles[None, :] <= jnp.minimum(tile_ids, n_used - 1)[:, None]).astype(I32), axis=1)
    tile_e = jnp.minimum(tile_e, N_EXPERTS - 1)
    first = jnp.concatenate([jnp.ones((1,), I32), (tile_e[1:] != tile_e[:-1]).astype(I32)])
    e_ids = jnp.arange(N_EXPERTS, dtype=I32)
    later = (e_ids[None, :] > e_ids[:, None]) & (tiles_e[None, :] > 0)
    next_e = jnp.min(jnp.where(later, e_ids[None, :], N_EXPERTS), axis=1)
    next_e = jnp.where(next_e == N_EXPERTS, -1, next_e)
    own = (tile_e[:, None] == e_ids[None, :]).astype(I32)
    pick = lambda per_expert: jnp.sum(own * per_expert[None, :], axis=1)
    tile_next = pick(next_e)
    tile_k = tile_ids - pick(cum_tiles - tiles_e)
    tile_rows = jnp.clip(pick(rows_e) * ROW_CHUNK - tile_k * MOE_TM, 0, MOE_TM)
    group_idx = jnp.cumsum(first) - 1
    n_groups = jnp.sum((tiles_e > 0).astype(I32))
    stage_slot = (jnp.arange(D_FF // MOE_FC, dtype=I32)[:, None] * n_groups + group_idx[None, :]) & 1
    tot_gather = jnp.sum(nch, axis=1)
    is_last = jnp.arange(nch.shape[0], dtype=I32) == nch.shape[0] - 1
    tot_scatter = tot_gather + jnp.where(is_last, jnp.sum(gap_len), 0)
    flat = lambda a: a.reshape(-1).astype(I32)
    return (flat(seg_src), flat(seg_dst), flat(nch), flat(gap_start), flat(gap_len),
            flat(tot_scatter), flat(tot_gather),
            (tile_e.astype(I32), first, tile_next.astype(I32), tile_rows.astype(I32),
             n_used.reshape(1).astype(I32), flat(stage_slot)))


def _chunk_rows(ref, chunk_idx):
    return ref.at[pl.ds(pl.multiple_of(chunk_idx * ROW_CHUNK, ROW_CHUNK), ROW_CHUNK), :]


def _compact_kernel(src_ref, dst_ref, nch_ref, gap_start_ref, gap_len_ref, tot_ref,
                    x_ref, slot_ref, w_ref, xs_ref, buf_ref, sems):
    tt = pl.program_id(0)
    n_tt = pl.num_programs(0)
    cur = tt & 1
    x = x_ref[...]
    s0, s1, s2, s3 = (slot_ref[kk:kk + 1, :] for kk in range(TOP_K))
    w0, w1, w2, w3 = (w_ref[kk:kk + 1, :] for kk in range(TOP_K))
    lane = lax.broadcasted_iota(I32, (SLOT_BLK, LANE), 1)
    for blk in range(SLOTS // SLOT_BLK):
        rows = slice(blk * SLOT_BLK, (blk + 1) * SLOT_BLK)
        sid = lax.broadcasted_iota(I32, (SLOT_BLK, TOK_TILE), 0) + blk * SLOT_BLK
        hit = jnp.where(sid == s0, 1.0, jnp.where(sid == s1, 1.0, jnp.where(sid == s2, 1.0,
                        jnp.where(sid == s3, 1.0, 0.0))))
        buf_ref[cur, rows, 0:D_MODEL] = jnp.dot(hit.astype(BF16), x, preferred_element_type=F32).astype(BF16)
        wsel = jnp.where(sid == s0, w0, jnp.where(sid == s1, w1, jnp.where(sid == s2, w2,
                         jnp.where(sid == s3, w3, 0.0))))
        wrow = jnp.sum(wsel, axis=1, keepdims=True)
        t0 = wrow.astype(BF16).astype(F32)
        t1 = (wrow - t0).astype(BF16).astype(F32)
        t2 = wrow - t0 - t1
        buf_ref[cur, rows, D_MODEL:XS_COLS] = jnp.where(
            lane == 0, t0, jnp.where(lane == 1, t1, jnp.where(lane == 2, t2, 0.0))).astype(BF16)

    def copy(slot, src_chunk, dst_chunk):
        return pltpu.make_async_copy(_chunk_rows(buf_ref.at[slot], src_chunk), _chunk_rows(xs_ref, dst_chunk),
                                     sems.at[slot])

    def per_expert(e, _):
        n = nch_ref[tt * N_EXPERTS + e]
        src0 = src_ref[tt * N_EXPERTS + e]
        dst0 = dst_ref[tt * N_EXPERTS + e]

        def issue(kk, _):
            copy(cur, src0 + kk, dst0 + kk).start()
            return 0

        lax.fori_loop(0, n, issue, 0)
        return 0

    lax.fori_loop(0, N_EXPERTS, per_expert, 0)

    zero_chunk = SLOTS // ROW_CHUNK - 1

    def per_gap(e, _):
        n = jnp.where(tt == n_tt - 1, gap_len_ref[e], 0)
        g0 = gap_start_ref[e]

        def issue(kk, _):
            copy(cur, zero_chunk, g0 + kk).start()
            return 0

        lax.fori_loop(0, n, issue, 0)
        return 0

    lax.fori_loop(0, N_EXPERTS, per_gap, 0)

    def drain(slot, count):
        def one(_, c):
            copy(slot, 0, 0).wait()
            return c

        lax.fori_loop(0, count, one, 0)

    @pl.when(tt > 0)
    def _():
        drain(1 - cur, tot_ref[tt - 1])

    @pl.when(tt == n_tt - 1)
    def _():
        drain(cur, tot_ref[tt])


def _compact(tables, hn2, slots, top_w, n_rows):
    seg_src, seg_dst, nch, gap_start, gap_len, tot = tables
    t = hn2.shape[0]
    return pl.pallas_call(
        _compact_kernel,
        grid_spec=pltpu.PrefetchScalarGridSpec(
            num_scalar_prefetch=6,
            grid=(t // TOK_TILE,),
            in_specs=[
                pl.BlockSpec((TOK_TILE, D_MODEL), lambda i, *_: (i, 0)),
                pl.BlockSpec((TOP_K, TOK_TILE), lambda i, *_: (0, i)),
                pl.BlockSpec((TOP_K, TOK_TILE), lambda i, *_: (0, i)),
            ],
            out_specs=pl.BlockSpec(memory_space=pl.ANY),
            scratch_shapes=[pltpu.VMEM((2, SLOTS, XS_COLS), BF16), pltpu.SemaphoreType.DMA((2,))],
        ),
        out_shape=jax.ShapeDtypeStruct((n_rows, XS_COLS), BF16),
        compiler_params=_cparams(("arbitrary",)),
        name="compact",
    )(seg_src, seg_dst, nch, gap_start, gap_len, tot, hn2, slots, top_w)


def _per_fill(rows, is_first, compute):
    for n in range(MOE_SUB, MOE_TM + 1, MOE_SUB):
        fits = (rows > n - MOE_SUB) & (rows <= n)

        @pl.when(fits & is_first)
        def _():
            compute(n, True)

        @pl.when(fits & jnp.logical_not(is_first))
        def _():
            compute(n, False)


def _moe_up_kernel(te_ref, first_ref, next_ref, rows_ref, nu_ref, slot_ref, x_ref, wg_hbm, wu_hbm, bg_ref, bu_ref,
                   h_ref, wg_st, wu_st, wg_bf, wu_bf, sems):
    c = pl.program_id(0)
    i = pl.program_id(1)
    slot = slot_ref[c * pl.num_programs(1) + i]

    def fetch(e, cc, sl):
        col = pl.ds(pl.multiple_of(cc * MOE_FC, MOE_FC), MOE_FC)
        return (pltpu.make_async_copy(wg_hbm.at[e, :, col], wg_st.at[sl], sems.at[sl, 0]),
                pltpu.make_async_copy(wu_hbm.at[e, :, col], wu_st.at[sl], sems.at[sl, 1]))

    @pl.when((c == 0) & (i == 0))
    def _():
        for cp in fetch(te_ref[0], 0, slot):
            cp.start()

    @pl.when(i < nu_ref[0])
    def _():
        is_first = first_ref[i] == 1
        nxt = next_ref[i]
        wraps = nxt < 0
        last_pass = c + 1 == pl.num_programs(0)
        nxt_e = jnp.where(wraps, te_ref[0], nxt)
        nxt_c = jnp.where(wraps, jnp.where(last_pass, 0, c + 1), c)

        @pl.when(is_first)
        def _():
            for cp in fetch(te_ref[i], c, slot):
                cp.wait()

        def compute(n, casting):
            x = x_ref[0:n, :]

            def cast(s):
                cols = slice(s * MOE_NB, (s + 1) * MOE_NB)
                wg_bf[:, cols] = wg_st[slot, :, cols].astype(BF16)
                wu_bf[:, cols] = wu_st[slot, :, cols].astype(BF16)

            n_sub = MOE_FC // MOE_NB
            if casting:
                for cp in fetch(nxt_e, nxt_c, 1 - slot):
                    cp.start()
                cast(0)
            for s in range(n_sub):
                if casting and s + 1 < n_sub:
                    cast(s + 1)
                cols = slice(s * MOE_NB, (s + 1) * MOE_NB)
                gate = jnp.minimum(jnp.dot(x, wg_bf[:, cols], preferred_element_type=F32) + bg_ref[:, cols],
                                   SWIGLU_LIMIT)
                up = jnp.clip(jnp.dot(x, wu_bf[:, cols], preferred_element_type=F32) + bu_ref[:, cols],
                              -SWIGLU_LIMIT, SWIGLU_LIMIT)
                act = (up + 1.0) * gate * (1.0 / (1.0 + jnp.exp(-SWIGLU_ALPHA * gate)))
                h_ref[0:n, cols] = act.astype(h_ref.dtype)

        _per_fill(rows_ref[i], is_first, compute)

        @pl.when(is_first & wraps & last_pass)
        def _():
            for cp in fetch(te_ref[0], 0, 1 - slot):
                cp.wait()


def _moe_up(tiles, xs, wg, wu, bg, bu):
    n_rows = xs.shape[0]
    n_tiles = n_rows // MOE_TM
    row = lambda c, i, te, fi, nx, rw, nu, pr: (jnp.minimum(i, nu[0] - 1), 0)
    bsel = lambda c, i, te, fi, nx, rw, nu, pr: (te[i], 0, c)
    return pl.pallas_call(
        _moe_up_kernel,
        grid_spec=pltpu.PrefetchScalarGridSpec(
            num_scalar_prefetch=6,
            grid=(D_FF // MOE_FC, n_tiles),
            in_specs=[
                pl.BlockSpec((MOE_TM, D_MODEL), row),
                pl.BlockSpec(memory_space=pl.ANY),
                pl.BlockSpec(memory_space=pl.ANY),
                pl.BlockSpec((None, 1, MOE_FC), bsel),
                pl.BlockSpec((None, 1, MOE_FC), bsel),
            ],
            out_specs=pl.BlockSpec((MOE_TM, MOE_FC),
                                   lambda c, i, te, fi, nx, rw, nu, pr: (jnp.minimum(i, nu[0] - 1), c)),
            scratch_shapes=[pltpu.VMEM((2, D_MODEL, MOE_FC), F32), pltpu.VMEM((2, D_MODEL, MOE_FC), F32),
                            pltpu.VMEM((D_MODEL, MOE_FC), BF16), pltpu.VMEM((D_MODEL, MOE_FC), BF16),
                            pltpu.SemaphoreType.DMA((2, 2))],
        ),
        out_shape=jax.ShapeDtypeStruct((n_rows, D_FF), BF16),
        compiler_params=_cparams(("arbitrary", "arbitrary")),
        name="moe_up",
    )(*tiles, xs, wg, wu, bg, bu)


def _moe_down_kernel(te_ref, first_ref, next_ref, rows_ref, nu_ref, slot_ref, h_ref, cw_ref, wd_hbm, bd_ref, y_ref,
                     wd_st, wd_bf, sems):
    i = pl.program_id(0)
    slot = slot_ref[i]

    def fetch(e, sl):
        return pltpu.make_async_copy(wd_hbm.at[e], wd_st.at[sl], sems.at[sl])

    @pl.when(i == 0)
    def _():
        fetch(te_ref[0], slot).start()

    @pl.when(i < nu_ref[0])
    def _():
        is_first = first_ref[i] == 1
        nxt = next_ref[i]
        nxt_e = jnp.where(nxt < 0, te_ref[0], nxt)

        @pl.when(is_first)
        def _():
            fetch(te_ref[i], slot).wait()

        def compute(n, casting):
            cw = cw_ref[0:n, :].astype(F32)
            cw = cw[:, 0:1] + cw[:, 1:2] + cw[:, 2:3]
            hrow = h_ref[0:n, :]

            def cast(s):
                cols = slice(s * MOE_NB, (s + 1) * MOE_NB)
                wd_bf[:, cols] = wd_st[slot, :, cols].astype(BF16)

            n_sub = D_MODEL // MOE_NB
            if casting:
                fetch(nxt_e, 1 - slot).start()
                cast(0)
            for s in range(n_sub):
                if casting and s + 1 < n_sub:
                    cast(s + 1)
                cols = slice(s * MOE_NB, (s + 1) * MOE_NB)
                y = (jnp.dot(hrow, wd_bf[:, cols], preferred_element_type=F32) + bd_ref[:, cols]) * cw
                y_ref[0:n, cols] = y.astype(y_ref.dtype)

        _per_fill(rows_ref[i], is_first, compute)

        @pl.when(is_first & (nxt < 0))
        def _():
            fetch(te_ref[0], 1 - slot).wait()


def _moe_down(tiles, hs, xs, wd, bd):
    n_rows = hs.shape[0]
    n_tiles = n_rows // MOE_TM
    row = lambda i, te, fi, nx, rw, nu, pr: (jnp.minimum(i, nu[0] - 1), 0)
    return pl.pallas_call(
        _moe_down_kernel,
        grid_spec=pltpu.PrefetchScalarGridSpec(
            num_scalar_prefetch=6,
            grid=(n_tiles,),
            in_specs=[
                pl.BlockSpec((MOE_TM, D_FF), row),
                pl.BlockSpec((MOE_TM, LANE),
                             lambda i, te, fi, nx, rw, nu, pr: (jnp.minimum(i, nu[0] - 1), D_MODEL // LANE)),
                pl.BlockSpec(memory_space=pl.ANY),
                pl.BlockSpec((None, 1, D_MODEL), lambda i, te, fi, nx, rw, nu, pr: (te[i], 0, 0)),
            ],
            out_specs=pl.BlockSpec((MOE_TM, D_MODEL), row),
            scratch_shapes=[pltpu.VMEM((2, D_FF, D_MODEL), F32), pltpu.VMEM((D_FF, D_MODEL), BF16),
                            pltpu.SemaphoreType.DMA((2,))],
        ),
        out_shape=jax.ShapeDtypeStruct((n_rows, D_MODEL), BF16),
        compiler_params=_cparams(("arbitrary",)),
        name="moe_down",
    )(*tiles, hs, xs, wd, bd)


def _combine_kernel(src_ref, dst_ref, nch_ref, tot_ref, h_ref, slot_ref, ys_ref, o_ref, buf_ref, sems):
    tt = pl.program_id(0)
    n_tt = pl.num_programs(0)
    cur = tt & 1

    def copy(slot, sorted_chunk, local_chunk):
        return pltpu.make_async_copy(_chunk_rows(ys_ref, sorted_chunk), _chunk_rows(buf_ref.at[slot], local_chunk),
                                     sems.at[slot])

    def gather(tile, slot):
        def per_expert(e, _):
            n = nch_ref[tile * N_EXPERTS + e]
            loc0 = src_ref[tile * N_EXPERTS + e]
            srt0 = dst_ref[tile * N_EXPERTS + e]

            def issue(kk, _):
                copy(slot, srt0 + kk, loc0 + kk).start()
                return 0

            lax.fori_loop(0, n, issue, 0)
            return 0

        lax.fori_loop(0, N_EXPERTS, per_expert, 0)

    @pl.when(tt == 0)
    def _():
        buf_ref[...] = jnp.zeros_like(buf_ref)
        gather(0, 0)

    @pl.when(tt + 1 < n_tt)
    def _():
        gather(tt + 1, 1 - cur)

    def one(_, c):
        copy(cur, 0, 0).wait()
        return c

    lax.fori_loop(0, tot_ref[tt], one, 0)

    o_ref[...] = h_ref[...]
    for blk in range(SLOTS // SLOT_BLK):
        sid = lax.broadcasted_iota(I32, (TOK_TILE, SLOT_BLK), 1) + blk * SLOT_BLK
        hit = jnp.zeros((TOK_TILE, SLOT_BLK), F32)
        for kk in range(TOP_K):
            hit = jnp.where(sid == slot_ref[:, kk:kk + 1], 1.0, hit)
        yb = buf_ref[cur, blk * SLOT_BLK:(blk + 1) * SLOT_BLK, :]
        o_ref[...] += jnp.dot(hit.astype(BF16), yb, preferred_element_type=F32)


def _combine(tables, h, slots_t, ys):
    seg_src, seg_dst, nch, tot = tables
    t = h.shape[0]
    return pl.pallas_call(
        _combine_kernel,
        grid_spec=pltpu.PrefetchScalarGridSpec(
            num_scalar_prefetch=4,
            grid=(t // TOK_TILE,),
            in_specs=[
                pl.BlockSpec((TOK_TILE, D_MODEL), lambda i, *_: (i, 0)),
                pl.BlockSpec((TOK_TILE, TOP_K), lambda i, *_: (i, 0)),
                pl.BlockSpec(memory_space=pl.ANY),
            ],
            out_specs=pl.BlockSpec((TOK_TILE, D_MODEL), lambda i, *_: (i, 0)),
            scratch_shapes=[pltpu.VMEM((2, SLOTS, D_MODEL), BF16), pltpu.SemaphoreType.DMA((2,))],
        ),
        out_shape=jax.ShapeDtypeStruct((t, D_MODEL), F32),
        compiler_params=_cparams(("arbitrary",)),
        name="combine",
    )(seg_src, seg_dst, nch, tot, h, slots_t, ys)


def kernel(x, attn_norm_w, w_in, da_q_norm_w, da_k_norm_w, da_lambda_q1, da_lambda_k1, da_lambda_q2,
           da_lambda_k2, da_subln_w, gla_gate_up_w, gla_gate_up_b, gla_norm_w, w_out, ffn_norm_w,
           router_w, router_b, exp_w_gate, exp_b_gate, exp_w_up, exp_b_up, exp_w_down, exp_b_down):
    b, s, d = x.shape
    t = b * s
    assert d == D_MODEL and t % TOK_TILE == 0 and s % CHUNK == 0 and attn_norm_w.shape[0] == 1
    x2 = x.reshape(t, d)

    w_tail = jnp.pad(w_in[0, :, IN_COLS_MAIN:].astype(BF16), ((0, 0), (0, LANE - GLA_RANK)))
    hn, gd = _rms_gd(x2, attn_norm_w, w_tail)
    proj3 = _in_proj(hn, w_in.reshape(D_MODEL, IN_COLS)).reshape(b, s, IN_COLS_MAIN)

    out_a = _diff_attn(proj3, jnp.tile(da_q_norm_w, (1, 2 * ATT_HP)), jnp.tile(da_k_norm_w, (1, 2 * ATT_HP)),
                       da_lambda_q1, da_lambda_k1, da_lambda_q2, da_lambda_k2, da_subln_w)
    wup_pad = jnp.pad(gla_gate_up_w[0].astype(BF16), ((0, LANE - GLA_RANK), (0, 0)))
    out_b = _gla(proj3, gd.reshape(b, s, LANE), wup_pad, gla_gate_up_b, gla_norm_w)

    h, hn2, slots, top_w, nch_b = _out_router(
        x2, out_a.reshape(t, -1), out_b.reshape(t, -1), w_out[0].astype(BF16), ffn_norm_w,
        router_w[0].T, router_b.reshape(N_EXPERTS, 1))

    n_tok_tiles = t // TOK_TILE
    nch = nch_b[:, 0].reshape(n_tok_tiles, N_EXPERTS)
    n_row_tiles = (TOP_K * t + n_tok_tiles * N_EXPERTS * (ROW_CHUNK - 1)) // MOE_TM + N_EXPERTS
    seg_src, seg_dst, nch_f, gap_start, gap_len, tot_scatter, tot_gather, tiles = _routing_tables(nch, n_row_tiles)

    xs = _compact((seg_src, seg_dst, nch_f, gap_start, gap_len, tot_scatter), hn2, slots, top_w,
                  n_row_tiles * MOE_TM)
    hs = _moe_up(tiles, xs,
                 exp_w_gate.reshape(N_EXPERTS, D_MODEL, D_FF), exp_w_up.reshape(N_EXPERTS, D_MODEL, D_FF),
                 exp_b_gate.reshape(N_EXPERTS, 1, D_FF), exp_b_up.reshape(N_EXPERTS, 1, D_FF))
    ys = _moe_down(tiles, hs, xs, exp_w_down.reshape(N_EXPERTS, D_FF, D_MODEL),
                   exp_b_down.reshape(N_EXPERTS, 1, D_MODEL))
    out = _combine((seg_src, seg_dst, nch_f, tot_gather), h, slots.T, ys)
    return out.reshape(b, s, d)
```

```python
import functools
import math

import jax
import jax.numpy as jnp
from jax import lax
from jax.experimental import pallas as pl
from jax.experimental.pallas import tpu as pltpu

F32 = jnp.float32
BF16 = jnp.bfloat16
I32 = jnp.int32

D_MODEL = 2048
CHUNK = 64
EPS = 1e-6
DA_HEADS = 8
DA_QK = 64
DA_V = 128
GLA_HEADS = 4
GLA_DK = 128
GLA_DV = 256
GLA_RANK = 16
GLA_TAU = 16.0
N_EXPERTS = 32
TOP_K = 4
D_FF = 2048
SWIGLU_LIMIT = 7.0
SWIGLU_ALPHA = 1.702
LAMBDA_INIT = 0.8 - 0.6 * math.exp(-0.3 * 0)

IN_COLS = 6160
IN_COLS_MAIN = IN_COLS - GLA_RANK
LANE = 128
ROW_CHUNK = 16

IN_TM = 1024
IN_TN = 1024
ATT_BQ = 256
ATT_HP = 4
ATT_SAFE_BOUND = 45.0
TOK_TILE = 512
SLOTS = TOP_K * TOK_TILE + N_EXPERTS * ROW_CHUNK
SLOT_BLK = 256
XS_COLS = D_MODEL + LANE
MOE_TM = 512
MOE_SUB = 128
MOE_FC = 1024
MOE_NB = 256
VMEM_LIMIT = 56 * 1024 * 1024


def _cparams(sem):
    return pltpu.CompilerParams(dimension_semantics=sem, vmem_limit_bytes=VMEM_LIMIT)


def _rms_gd_kernel(x_ref, nw_ref, wt_ref, hn_ref, gd_ref):
    x = x_ref[...]
    ms = jnp.mean(x * x, axis=-1, keepdims=True)
    hn = (x * lax.rsqrt(ms + EPS) * nw_ref[...]).astype(BF16)
    hn_ref[...] = hn
    gd_ref[...] = jnp.dot(hn, wt_ref[...], preferred_element_type=F32).astype(gd_ref.dtype)


def _rms_gd(x2, norm_w, w_tail):
    t = x2.shape[0]
    tm = min(IN_TM, t)
    return pl.pallas_call(
        _rms_gd_kernel,
        grid=(t // tm,),
        in_specs=[
            pl.BlockSpec((tm, D_MODEL), lambda i: (i, 0)),
            pl.BlockSpec((1, D_MODEL), lambda i: (0, 0)),
            pl.BlockSpec((D_MODEL, LANE), lambda i: (0, 0)),
        ],
        out_specs=[pl.BlockSpec((tm, D_MODEL), lambda i: (i, 0)), pl.BlockSpec((tm, LANE), lambda i: (i, 0))],
        out_shape=[jax.ShapeDtypeStruct((t, D_MODEL), BF16), jax.ShapeDtypeStruct((t, LANE), BF16)],
        compiler_params=_cparams(("parallel",)),
        name="rms_gd",
    )(x2, norm_w, w_tail)


def _in_proj_kernel(hn_ref, wt_ref, o_ref, w_bf):
    @pl.when(pl.program_id(1) == 0)
    def _():
        w_bf[...] = wt_ref[...].T.astype(BF16)

    o_ref[...] = jnp.dot(hn_ref[...], w_bf[...], preferred_element_type=F32).astype(o_ref.dtype)


def _in_proj(hn, w_in_t):
    t = hn.shape[0]
    tm = min(IN_TM, t)
    return pl.pallas_call(
        _in_proj_kernel,
        grid=(IN_COLS_MAIN // IN_TN, t // tm),
        in_specs=[
            pl.BlockSpec((tm, D_MODEL), lambda j, i: (i, 0)),
            pl.BlockSpec((IN_TN, D_MODEL), lambda j, i: (j, 0)),
        ],
        out_specs=pl.BlockSpec((tm, IN_TN), lambda j, i: (i, j)),
        out_shape=jax.ShapeDtypeStruct((t, IN_COLS_MAIN), BF16),
        scratch_shapes=[pltpu.VMEM((D_MODEL, IN_TN), BF16)],
        compiler_params=_cparams(("arbitrary", "arbitrary")),
        name="in_proj",
    )(hn, w_in_t)


def _attn_kernel(q_ref, k_ref, v_ref, qw_ref, kw_ref, lq1_ref, lk1_ref, lq2_ref, lk2_ref, sw_ref,
                 o_ref, kn_ref, vt_ref, *, bq):
    qi = pl.program_id(2)
    first_map = lax.broadcasted_iota(I32, (1, LANE), 1) < DA_QK
    n_grp = 2 * bq // LANE
    shift = CHUNK.bit_length() - 1

    def qk_norm(x, w):
        x2 = x * x
        scale = []
        for hh in range(ATT_HP):
            xh = x2[:, hh * LANE:(hh + 1) * LANE]
            s1 = jnp.sum(jnp.where(first_map, xh, 0.0), axis=-1, keepdims=True)
            s2 = jnp.sum(jnp.where(first_map, 0.0, xh), axis=-1, keepdims=True)
            scale.append(jnp.where(first_map, lax.rsqrt(s1 / DA_QK + EPS), lax.rsqrt(s2 / DA_QK + EPS)))
        return x * jnp.concatenate(scale, axis=1) * w

    @pl.when(qi == 0)
    def _():
        kn_ref[...] = qk_norm(k_ref[...].astype(F32), kw_ref[...]).astype(BF16)
        vt_ref[...] = v_ref[...].astype(F32).T.astype(BF16)

    q = qk_norm(q_ref[...].astype(F32), qw_ref[...]) * (DA_QK ** -0.5 * math.log2(math.e))
    q_t = []
    for hh in range(ATT_HP):
        qh = q[:, hh * LANE:(hh + 1) * LANE]
        q_t.append(jnp.concatenate([jnp.where(first_map, qh, 0.0), jnp.where(first_map, 0.0, qh)],
                                   axis=0).T.astype(BF16))

    bound = jnp.max(jnp.abs(qw_ref[...] * kw_ref[...]), axis=-1, keepdims=True) * (
        DA_QK ** 0.5 * math.log2(math.e) * 1.02)

    def step(j, carry, masked, fixed_shift):
        r0 = pl.multiple_of(j * bq, bq)
        scores = [jnp.dot(kn_ref[pl.ds(r0, bq), hh * LANE:(hh + 1) * LANE], q_t[hh], preferred_element_type=F32)
                  for hh in range(ATT_HP)]
        out = []
        for hh in range(ATT_HP):
            ms, ls, alphas, ps = [], [], [], []
            for g in range(n_grp):
                m, l, _ = carry[hh][g]
                s = scores[hh][:, g * LANE:(g + 1) * LANE]
                if masked:
                    key_chunk = lax.broadcasted_iota(I32, (bq, LANE), 0) >> shift
                    qry_chunk = (lax.broadcasted_iota(I32, (bq, LANE), 1) + (g * LANE) % bq) >> shift
                    s = jnp.where(key_chunk <= qry_chunk, s, -jnp.inf)
                if fixed_shift:
                    p = jnp.exp2(s - bound)
                    ms.append(m)
                    ls.append(l + jnp.sum(p, axis=0, keepdims=True))
                else:
                    m_new = jnp.maximum(m, jnp.max(s, axis=0, keepdims=True))
                    alpha = jnp.exp2(m - m_new)
                    p = jnp.exp2(s - m_new)
                    ms.append(m_new)
                    ls.append(alpha * l + jnp.sum(p, axis=0, keepdims=True))
                    alphas.append(alpha)
                ps.append(p.astype(BF16))
            pv = jnp.dot(vt_ref[hh * DA_V:(hh + 1) * DA_V, pl.ds(r0, bq)], jnp.concatenate(ps, axis=1),
                         preferred_element_type=F32)
            out.append(tuple((ms[g], ls[g],
                              (carry[hh][g][2] if fixed_shift else alphas[g] * carry[hh][g][2])
                              + pv[:, g * LANE:(g + 1) * LANE]) for g in range(n_grp)))
        return tuple(out)

    lam = (jnp.exp(jnp.sum(lq1_ref[...] * lk1_ref[...], axis=-1, keepdims=True))
           - jnp.exp(jnp.sum(lq2_ref[...] * lk2_ref[...], axis=-1, keepdims=True)) + LAMBDA_INIT)

    def attend(fixed_shift):
        init = tuple(tuple((jnp.full((1, LANE), -jnp.inf, F32), jnp.zeros((1, LANE), F32),
                            jnp.zeros((DA_V, LANE), F32)) for _ in range(n_grp)) for _ in range(ATT_HP))
        carry = lax.fori_loop(0, qi, lambda j, c: step(j, c, False, fixed_shift), init)
        carry = step(qi, carry, True, fixed_shift)
        half = n_grp // 2
        for hh in range(ATT_HP):
            for g in range(half):
                o1 = carry[hh][g][2] / carry[hh][g][1]
                o2 = carry[hh][g + half][2] / carry[hh][g + half][1]
                out = o1 - lam * o2
                ms = jnp.mean(out * out, axis=0, keepdims=True)
                y = out * lax.rsqrt(ms + EPS) * sw_ref[...] * (1.0 - LAMBDA_INIT)
                o_ref[g * LANE:(g + 1) * LANE, hh * DA_V:(hh + 1) * DA_V] = y.T.astype(o_ref.dtype)

    small = jnp.max(bound) <= ATT_SAFE_BOUND

    @pl.when(small)
    def _():
        attend(True)

    @pl.when(jnp.logical_not(small))
    def _():
        attend(False)


def _diff_attn(proj3, qw, kw, lq1, lk1, lq2, lk2, sw):
    b, s, _ = proj3.shape
    bq = min(ATT_BQ, s)
    hw = ATT_HP * LANE
    n_hg = DA_HEADS // ATT_HP
    vec = lambda n: pl.BlockSpec((1, n), lambda bi, h, qi: (0, 0))
    return pl.pallas_call(
        functools.partial(_attn_kernel, bq=bq),
        grid=(b, n_hg, s // bq),
        in_specs=[
            pl.BlockSpec((None, bq, hw), lambda bi, h, qi: (bi, qi, h)),
            pl.BlockSpec((None, s, hw), lambda bi, h, qi: (bi, 0, n_hg + h)),
            pl.BlockSpec((None, s, hw), lambda bi, h, qi: (bi, 0, 2 * n_hg + h)),
            vec(hw), vec(hw), vec(DA_QK), vec(DA_QK), vec(DA_QK), vec(DA_QK),
            pl.BlockSpec((DA_V, 1), lambda bi, h, qi: (0, 0)),
        ],
        out_specs=pl.BlockSpec((None, bq, hw), lambda bi, h, qi: (bi, qi, h)),
        out_shape=jax.ShapeDtypeStruct((b, s, DA_HEADS * DA_V), BF16),
        scratch_shapes=[pltpu.VMEM((s, hw), BF16), pltpu.VMEM((hw, s), BF16)],
        compiler_params=_cparams(("parallel", "parallel", "arbitrary")),
        name="diff_attn",
    )(proj3, proj3, proj3, qw, kw, lq1, lk1, lq2, lk2, sw.reshape(DA_V, 1))


GLA_SUB = 16


def _gla_kernel(q_ref, k_ref, v_ref, r_ref, gd_ref, wup_ref, bup_ref, nw_ref, o_ref, st_ref, *, n_chunks):
    st_ref[...] = jnp.zeros_like(st_ref)
    rr = lax.broadcasted_iota(I32, (CHUNK, CHUNK), 0)
    cc = lax.broadcasted_iota(I32, (CHUNK, CHUNK), 1)
    tril = jnp.where(cc <= rr, 1.0, 0.0).astype(BF16)
    sub_row = lax.broadcasted_iota(I32, (GLA_SUB, GLA_DK), 0)
    sub_lane = lax.broadcasted_iota(I32, (GLA_SUB, CHUNK), 1)
    chunk_row = lax.broadcasted_iota(I32, (CHUNK, GLA_DK), 0)
    nt = (((1,), (1,)), ((), ()))
    tn = (((0,), (0,)), ((), ()))

    def chunk(c, _):
        r0 = pl.multiple_of(c * CHUNK, CHUNK)
        z = jnp.dot(gd_ref[pl.ds(r0, CHUNK), :], wup_ref[...], preferred_element_type=F32) + bup_ref[...]
        lg = (jnp.minimum(z, 0.0) - jnp.log(1.0 + jnp.exp(-jnp.abs(z)))) / GLA_TAU
        lg_hi = lg.astype(BF16)
        lg_lo = (lg - lg_hi.astype(F32)).astype(BF16)
        bcum = (jnp.dot(tril, lg_hi, preferred_element_type=F32)
                + jnp.dot(tril, lg_lo, preferred_element_type=F32))
        bprev = bcum - lg
        for hh in range(GLA_HEADS):
            kcol = slice(hh * GLA_DK, (hh + 1) * GLA_DK)
            head_chunk(r0, hh, bcum[:, kcol], bprev[:, kcol])
        return 0

    def head_chunk(r0, hh, bcum, bprev):
        kcol = slice(hh * GLA_DK, (hh + 1) * GLA_DK)
        vcol = slice(hh * GLA_DV, (hh + 1) * GLA_DV)
        q = q_ref[pl.ds(r0, CHUNK), kcol].astype(F32) * (GLA_DK ** -0.5)
        k = k_ref[pl.ds(r0, CHUNK), kcol].astype(F32)
        v = v_ref[pl.ds(r0, CHUNK), vcol]
        st = st_ref[hh]
        o_inter = lax.dot_general((q * jnp.exp(bcum)).astype(BF16), st.astype(BF16), nt,
                                  preferred_element_type=F32)
        a_rows = []
        for blk in range(CHUNK // GLA_SUB):
            lo = blk * GLA_SUB
            hi = lo + GLA_SUB
            b_blk = bcum[lo:hi]
            q_blk = q[lo:hi]
            if blk > 0:
                ref = bprev[lo:lo + 1]
                qt = (q_blk * jnp.exp(b_blk - ref)).astype(BF16)
                kt = (k * jnp.exp(jnp.where(chunk_row < lo, ref - bcum, -jnp.inf))).astype(BF16)
                a_blk = lax.dot_general(qt, kt, nt, preferred_element_type=F32)
            else:
                a_blk = jnp.zeros((GLA_SUB, CHUNK), F32)
            for jj in range(GLA_SUB):
                dlt = jnp.where(sub_row >= jj, b_blk - bcum[lo + jj:lo + jj + 1], -jnp.inf)
                col = jnp.sum(q_blk * k[lo + jj:lo + jj + 1] * jnp.exp(dlt), axis=-1, keepdims=True)
                a_blk = a_blk + jnp.where(sub_lane == lo + jj, col, 0.0)
            a_rows.append(a_blk.astype(BF16))
        o = o_inter + jnp.dot(jnp.concatenate(a_rows, axis=0), v, preferred_element_type=F32)
        ms = jnp.mean(o * o, axis=-1, keepdims=True)
        rg = r_ref[pl.ds(r0, CHUNK), vcol].astype(F32)
        y = o * lax.rsqrt(ms + EPS) * nw_ref[...] * (rg / (1.0 + jnp.exp(-rg)))
        o_ref[pl.ds(r0, CHUNK), vcol] = y.astype(o_ref.dtype)
        b_last = bcum[CHUNK - 1:CHUNK]
        k_dec = (k * jnp.exp(b_last - bcum)).astype(BF16)
        st_ref[hh] = st * jnp.exp(b_last) + lax.dot_general(v, k_dec, tn, preferred_element_type=F32)

    lax.fori_loop(0, n_chunks, chunk, 0)


def _gla(proj3, gd3, wup_pad, bup, nw):
    b, s, _ = proj3.shape
    kw = GLA_HEADS * GLA_DK
    vw = GLA_HEADS * GLA_DV
    q0 = 3 * DA_HEADS * DA_V // kw
    k0 = q0 + 1
    v0 = (k0 + 1) * kw // vw
    r0 = v0 + 1
    return pl.pallas_call(
        functools.partial(_gla_kernel, n_chunks=s // CHUNK),
        grid=(b,),
        in_specs=[
            pl.BlockSpec((None, s, kw), lambda bi: (bi, 0, q0)),
            pl.BlockSpec((None, s, kw), lambda bi: (bi, 0, k0)),
            pl.BlockSpec((None, s, vw), lambda bi: (bi, 0, v0)),
            pl.BlockSpec((None, s, vw), lambda bi: (bi, 0, r0)),
            pl.BlockSpec((None, s, LANE), lambda bi: (bi, 0, 0)),
            pl.BlockSpec((LANE, kw), lambda bi: (0, 0)),
            pl.BlockSpec((1, kw), lambda bi: (0, 0)),
            pl.BlockSpec((1, GLA_DV), lambda bi: (0, 0)),
        ],
        out_specs=pl.BlockSpec((None, s, vw), lambda bi: (bi, 0, 0)),
        out_shape=jax.ShapeDtypeStruct((b, s, vw), BF16),
        scratch_shapes=[pltpu.VMEM((GLA_HEADS, GLA_DV, GLA_DK), F32)],
        compiler_params=_cparams(("parallel",)),
        name="gla",
    )(proj3, proj3, proj3, proj3, gd3, wup_pad, bup, nw)


def _out_router_kernel(x_ref, a_ref, b_ref, wo_ref, nw_ref, rwt_ref, rb_ref,
                       h_ref, hn_ref, slot_ref, w_ref, nch_ref):
    tm = x_ref.shape[0]
    half = a_ref.shape[1]
    h = (x_ref[...]
         + jnp.dot(a_ref[...], wo_ref[0:half, :], preferred_element_type=F32)
         + jnp.dot(b_ref[...], wo_ref[half:2 * half, :], preferred_element_type=F32))
    h_ref[...] = h
    ms = jnp.mean(h * h, axis=-1, keepdims=True)
    hn = h * lax.rsqrt(ms + EPS) * nw_ref[...]
    hn_hi = hn.astype(BF16)
    hn_ref[...] = hn_hi
    hn_lo = (hn - hn_hi.astype(F32)).astype(BF16)
    rwt = rwt_ref[...]
    rw_hi = rwt.astype(BF16)
    rw_lo = (rwt - rw_hi.astype(F32)).astype(BF16)
    nt = (((1,), (1,)), ((), ()))
    logits = (lax.dot_general(rw_hi, hn_hi, nt, preferred_element_type=F32)
              + lax.dot_general(rw_hi, hn_lo, nt, preferred_element_type=F32)
              + lax.dot_general(rw_lo, hn_hi, nt, preferred_element_type=F32)
              + rb_ref[...])
    e_iota = lax.broadcasted_iota(I32, (N_EXPERTS, tm), 0).astype(F32)
    vals, hots = [], []
    for _ in range(TOP_K):
        m = jnp.max(logits, axis=0, keepdims=True)
        idx = jnp.min(jnp.where(logits == m, e_iota, float(N_EXPERTS)), axis=0, keepdims=True)
        hot = e_iota == idx
        vals.append(m)
        hots.append(hot)
        logits = jnp.where(hot, -jnp.inf, logits)
    exps = [jnp.exp(v - vals[0]) for v in vals]
    denom = exps[0] + exps[1] + exps[2] + exps[3]
    for kk in range(TOP_K):
        w_ref[kk:kk + 1, :] = exps[kk] / denom
    sel = jnp.zeros((N_EXPERTS, tm), F32)
    for hot in hots:
        sel = sel + jnp.where(hot, 1.0, 0.0)
    tr = lax.broadcasted_iota(I32, (tm, tm), 0)
    tc = lax.broadcasted_iota(I32, (tm, tm), 1)
    upper = jnp.where(tr < tc, 1.0, 0.0).astype(BF16)
    rank = jnp.dot(sel.astype(BF16), upper, preferred_element_type=F32)
    cnt = jnp.sum(sel, axis=1, keepdims=True)
    nch = jnp.floor((cnt + (ROW_CHUNK - 1)) * (1.0 / ROW_CHUNK))
    nch_b = jnp.broadcast_to(nch, (N_EXPERTS, LANE))
    er = lax.broadcasted_iota(I32, (N_EXPERTS, N_EXPERTS), 0)
    ec = lax.broadcasted_iota(I32, (N_EXPERTS, N_EXPERTS), 1)
    lower = jnp.where(ec < er, 1.0, 0.0).astype(BF16)
    seg = jnp.dot(lower, nch_b.astype(BF16), preferred_element_type=F32)
    pos = seg[:, 0:1] * ROW_CHUNK + rank
    for kk in range(TOP_K):
        slot_ref[kk:kk + 1, :] = jnp.sum(jnp.where(hots[kk], pos, 0.0), axis=0, keepdims=True).astype(I32)
    nch_ref[...] = nch_b.astype(I32)


def _out_router(x2, out_a, out_b, wo_bf, nw, rwt, rb):
    t = x2.shape[0]
    tm = TOK_TILE
    half = out_a.shape[1]
    nt = t // tm
    return pl.pallas_call(
        _out_router_kernel,
        grid=(nt,),
        in_specs=[
            pl.BlockSpec((tm, D_MODEL), lambda i: (i, 0)),
            pl.BlockSpec((tm, half), lambda i: (i, 0)),
            pl.BlockSpec((tm, half), lambda i: (i, 0)),
            pl.BlockSpec((2 * half, D_MODEL), lambda i: (0, 0)),
            pl.BlockSpec((1, D_MODEL), lambda i: (0, 0)),
            pl.BlockSpec((N_EXPERTS, D_MODEL), lambda i: (0, 0)),
            pl.BlockSpec((N_EXPERTS, 1), lambda i: (0, 0)),
        ],
        out_specs=[
            pl.BlockSpec((tm, D_MODEL), lambda i: (i, 0)),
            pl.BlockSpec((tm, D_MODEL), lambda i: (i, 0)),
            pl.BlockSpec((TOP_K, tm), lambda i: (0, i)),
            pl.BlockSpec((TOP_K, tm), lambda i: (0, i)),
            pl.BlockSpec((N_EXPERTS, LANE), lambda i: (i, 0)),
        ],
        out_shape=[
            jax.ShapeDtypeStruct((t, D_MODEL), F32),
            jax.ShapeDtypeStruct((t, D_MODEL), BF16),
            jax.ShapeDtypeStruct((TOP_K, t), I32),
            jax.ShapeDtypeStruct((TOP_K, t), F32),
            jax.ShapeDtypeStruct((nt * N_EXPERTS, LANE), I32),
        ],
        compiler_params=_cparams(("parallel",)),
        name="out_router",
    )(x2, out_a, out_b, wo_bf, nw, rwt, rb)


def _routing_tables(nch, n_row_tiles):
    cpt = MOE_TM // ROW_CHUNK
    seg_src = jnp.cumsum(nch, axis=1) - nch
    rows_e = jnp.sum(nch, axis=0)
    tiles_e = (rows_e + cpt - 1) // cpt
    cum_tiles = jnp.cumsum(tiles_e)
    base_e = (cum_tiles - tiles_e) * cpt
    seg_dst = base_e[None, :] + jnp.cumsum(nch, axis=0) - nch
    gap_start = base_e + rows_e
    cps = MOE_SUB // ROW_CHUNK
    gap_len = (rows_e + cps - 1) // cps * cps - rows_e
    n_used = cum_tiles[-1]
    tile_ids = jnp.arange(n_row_tiles, dtype=I32)
    tile_e = jnp.sum((cum_tiles[None, :] <= jnp.minimum(tile_ids, n_used - 1)[:, None]).astype(I32), axis=1)
    tile_e = jnp.minimum(tile_e, N_EXPERTS - 1)
    first = jnp.concatenate([jnp.ones((1,), I32), (tile_e[1:] != tile_e[:-1]).astype(I32)])
    e_ids = jnp.arange(N_EXPERTS, dtype=I32)
    later = (e_ids[None, :] > e_ids[:, None]) & (tiles_e[None, :] > 0)
    next_e = jnp.min(jnp.where(later, e_ids[None, :], N_EXPERTS), axis=1)
    next_e = jnp.where(next_e == N_EXPERTS, -1, next_e)
    own = (tile_e[:, None] == e_ids[None, :]).astype(I32)
    pick = lambda per_expert: jnp.sum(own * per_expert[None, :], axis=1)
    tile_next = pick(next_e)
    tile_k = tile_ids - pick(cum_tiles - tiles_e)
    tile_rows = jnp.clip(pick(rows_e) * ROW_CHUNK - tile_k * MOE_TM, 0, MOE_TM)
    group_idx = jnp.cumsum(first) - 1
    n_groups = jnp.sum((tiles_e > 0).astype(I32))
    stage_slot = (jnp.arange(D_FF // MOE_FC, dtype=I32)[:, None] * n_groups + group_idx[None, :]) & 1
    tot_gather = jnp.sum(nch, axis=1)
    is_last = jnp.arange(nch.shape[0], dtype=I32) == nch.shape[0] - 1
    tot_scatter = tot_gather + jnp.where(is_last, jnp.sum(gap_len), 0)
    flat = lambda a: a.reshape(-1).astype(I32)
    return (flat(seg_src), flat(seg_dst), flat(nch), flat(gap_start), flat(gap_len),
            flat(tot_scatter), flat(tot_gather),
            (tile_e.astype(I32), first, tile_next.astype(I32), tile_rows.astype(I32),
             n_used.reshape(1).astype(I32), flat(stage_slot)))


def _chunk_rows(ref, chunk_idx):
    return ref.at[pl.ds(pl.multiple_of(chunk_idx * ROW_CHUNK, ROW_CHUNK), ROW_CHUNK), :]


def _compact_kernel(src_ref, dst_ref, nch_ref, gap_start_ref, gap_len_ref, tot_ref,
                    x_ref, slot_ref, w_ref, xs_ref, buf_ref, sems):
    tt = pl.program_id(0)
    n_tt = pl.num_programs(0)
    cur = tt & 1
    x = x_ref[...]
    s0, s1, s2, s3 = (slot_ref[kk:kk + 1, :] for kk in range(TOP_K))
    w0, w1, w2, w3 = (w_ref[kk:kk + 1, :] for kk in range(TOP_K))
    lane = lax.broadcasted_iota(I32, (SLOT_BLK, LANE), 1)
    for blk in range(SLOTS // SLOT_BLK):
        rows = slice(blk * SLOT_BLK, (blk + 1) * SLOT_BLK)
        sid = lax.broadcasted_iota(I32, (SLOT_BLK, TOK_TILE), 0) + blk * SLOT_BLK
        hit = jnp.where(sid == s0, 1.0, jnp.where(sid == s1, 1.0, jnp.where(sid == s2, 1.0,
                        jnp.where(sid == s3, 1.0, 0.0))))
        buf_ref[cur, rows, 0:D_MODEL] = jnp.dot(hit.astype(BF16), x, preferred_element_type=F32).astype(BF16)
        wsel = jnp.where(sid == s0, w0, jnp.where(sid == s1, w1, jnp.where(sid == s2, w2,
                         jnp.where(sid == s3, w3, 0.0))))
        wrow = jnp.sum(wsel, axis=1, keepdims=True)
        t0 = wrow.astype(BF16).astype(F32)
        t1 = (wrow - t0).astype(BF16).astype(F32)
        t2 = wrow - t0 - t1
        buf_ref[cur, rows, D_MODEL:XS_COLS] = jnp.where(
            lane == 0, t0, jnp.where(lane == 1, t1, jnp.where(lane == 2, t2, 0.0))).astype(BF16)

    def copy(slot, src_chunk, dst_chunk):
        return pltpu.make_async_copy(_chunk_rows(buf_ref.at[slot], src_chunk), _chunk_rows(xs_ref, dst_chunk),
                                     sems.at[slot])

    def per_expert(e, _):
        n = nch_ref[tt * N_EXPERTS + e]
        src0 = src_ref[tt * N_EXPERTS + e]
        dst0 = dst_ref[tt * N_EXPERTS + e]

        def issue(kk, _):
            copy(cur, src0 + kk, dst0 + kk).start()
            return 0

        lax.fori_loop(0, n, issue, 0)
        return 0

    lax.fori_loop(0, N_EXPERTS, per_expert, 0)

    zero_chunk = SLOTS // ROW_CHUNK - 1

    def per_gap(e, _):
        n = jnp.where(tt == n_tt - 1, gap_len_ref[e], 0)
        g0 = gap_start_ref[e]

        def issue(kk, _):
            copy(cur, zero_chunk, g0 + kk).start()
            return 0

        lax.fori_loop(0, n, issue, 0)
        return 0

    lax.fori_loop(0, N_EXPERTS, per_gap, 0)

    def drain(slot, count):
        def one(_, c):
            copy(slot, 0, 0).wait()
            return c

        lax.fori_loop(0, count, one, 0)

    @pl.when(tt > 0)
    def _():
        drain(1 - cur, tot_ref[tt - 1])

    @pl.when(tt == n_tt - 1)
    def _():
        drain(cur, tot_ref[tt])


def _compact(tables, hn2, slots, top_w, n_rows):
    seg_src, seg_dst, nch, gap_start, gap_len, tot = tables
    t = hn2.shape[0]
    return pl.pallas_call(
        _compact_kernel,
        grid_spec=pltpu.PrefetchScalarGridSpec(
            num_scalar_prefetch=6,
            grid=(t // TOK_TILE,),
            in_specs=[
                pl.BlockSpec((TOK_TILE, D_MODEL), lambda i, *_: (i, 0)),
                pl.BlockSpec((TOP_K, TOK_TILE), lambda i, *_: (0, i)),
                pl.BlockSpec((TOP_K, TOK_TILE), lambda i, *_: (0, i)),
            ],
            out_specs=pl.BlockSpec(memory_space=pl.ANY),
            scratch_shapes=[pltpu.VMEM((2, SLOTS, XS_COLS), BF16), pltpu.SemaphoreType.DMA((2,))],
        ),
        out_shape=jax.ShapeDtypeStruct((n_rows, XS_COLS), BF16),
        compiler_params=_cparams(("arbitrary",)),
        name="compact",
    )(seg_src, seg_dst, nch, gap_start, gap_len, tot, hn2, slots, top_w)


def _per_fill(rows, is_first, compute):
    for n in range(MOE_SUB, MOE_TM + 1, MOE_SUB):
        fits = (rows > n - MOE_SUB) & (rows <= n)

        @pl.when(fits & is_first)
        def _():
            compute(n, True)

        @pl.when(fits & jnp.logical_not(is_first))
        def _():
            compute(n, False)


def _moe_up_kernel(te_ref, first_ref, next_ref, rows_ref, nu_ref, slot_ref, x_ref, wg_hbm, wu_hbm, bg_ref, bu_ref,
                   h_ref, wg_st, wu_st, wg_bf, wu_bf, sems):
    c = pl.program_id(0)
    i = pl.program_id(1)
    slot = slot_ref[c * pl.num_programs(1) + i]

    def fetch(e, cc, sl):
        col = pl.ds(pl.multiple_of(cc * MOE_FC, MOE_FC), MOE_FC)
        return (pltpu.make_async_copy(wg_hbm.at[e, :, col], wg_st.at[sl], sems.at[sl, 0]),
                pltpu.make_async_copy(wu_hbm.at[e, :, col], wu_st.at[sl], sems.at[sl, 1]))

    @pl.when((c == 0) & (i == 0))
    def _():
        for cp in fetch(te_ref[0], 0, slot):
            cp.start()

    @pl.when(i < nu_ref[0])
    def _():
        is_first = first_ref[i] == 1
        nxt = next_ref[i]
        wraps = nxt < 0
        last_pass = c + 1 == pl.num_programs(0)
        nxt_e = jnp.where(wraps, te_ref[0], nxt)
        nxt_c = jnp.where(wraps, jnp.where(last_pass, 0, c + 1), c)

        @pl.when(is_first)
        def _():
            for cp in fetch(te_ref[i], c, slot):
                cp.wait()

        def compute(n, casting):
            x = x_ref[0:n, :]

            def cast(s):
                cols = slice(s * MOE_NB, (s + 1) * MOE_NB)
                wg_bf[:, cols] = wg_st[slot, :, cols].astype(BF16)
                wu_bf[:, cols] = wu_st[slot, :, cols].astype(BF16)

            n_sub = MOE_FC // MOE_NB
            if casting:
                for cp in fetch(nxt_e, nxt_c, 1 - slot):
                    cp.start()
                cast(0)
            for s in range(n_sub):
                if casting and s + 1 < n_sub:
                    cast(s + 1)
                cols = slice(s * MOE_NB, (s + 1) * MOE_NB)
                gate = jnp.minimum(jnp.dot(x, wg_bf[:, cols], preferred_element_type=F32) + bg_ref[:, cols],
                                   SWIGLU_LIMIT)
                up = jnp.clip(jnp.dot(x, wu_bf[:, cols], preferred_element_type=F32) + bu_ref[:, cols],
                              -SWIGLU_LIMIT, SWIGLU_LIMIT)
                act = (up + 1.0) * gate * (1.0 / (1.0 + jnp.exp(-SWIGLU_ALPHA * gate)))
                h_ref[0:n, cols] = act.astype(h_ref.dtype)

        _per_fill(rows_ref[i], is_first, compute)

        @pl.when(is_first & wraps & last_pass)
        def _():
            for cp in fetch(te_ref[0], 0, 1 - slot):
                cp.wait()


def _moe_up(tiles, xs, wg, wu, bg, bu):
    n_rows = xs.shape[0]
    n_tiles = n_rows // MOE_TM
    row = lambda c, i, te, fi, nx, rw, nu, pr: (jnp.minimum(i, nu[0] - 1), 0)
    bsel = lambda c, i, te, fi, nx, rw, nu, pr: (te[i], 0, c)
    return pl.pallas_call(
        _moe_up_kernel,
        grid_spec=pltpu.PrefetchScalarGridSpec(
            num_scalar_prefetch=6,
            grid=(D_FF // MOE_FC, n_tiles),
            in_specs=[
                pl.BlockSpec((MOE_TM, D_MODEL), row),
                pl.BlockSpec(memory_space=pl.ANY),
                pl.BlockSpec(memory_space=pl.ANY),
                pl.BlockSpec((None, 1, MOE_FC), bsel),
                pl.BlockSpec((None, 1, MOE_FC), bsel),
            ],
            out_specs=pl.BlockSpec((MOE_TM, MOE_FC),
                                   lambda c, i, te, fi, nx, rw, nu, pr: (jnp.minimum(i, nu[0] - 1), c)),
            scratch_shapes=[pltpu.VMEM((2, D_MODEL, MOE_FC), F32), pltpu.VMEM((2, D_MODEL, MOE_FC), F32),
                            pltpu.VMEM((D_MODEL, MOE_FC), BF16), pltpu.VMEM((D_MODEL, MOE_FC), BF16),
                            pltpu.SemaphoreType.DMA((2, 2))],
        ),
        out_shape=jax.ShapeDtypeStruct((n_rows, D_FF), BF16),
        compiler_params=_cparams(("arbitrary", "arbitrary")),
        name="moe_up",
    )(*tiles, xs, wg, wu, bg, bu)


def _moe_down_kernel(te_ref, first_ref, next_ref, rows_ref, nu_ref, slot_ref, h_ref, cw_ref, wd_hbm, bd_ref, y_ref,
                     wd_st, wd_bf, sems):
    i = pl.program_id(0)
    slot = slot_ref[i]

    def fetch(e, sl):
        return pltpu.make_async_copy(wd_hbm.at[e], wd_st.at[sl], sems.at[sl])

    @pl.when(i == 0)
    def _():
        fetch(te_ref[0], slot).start()

    @pl.when(i < nu_ref[0])
    def _():
        is_first = first_ref[i] == 1
        nxt = next_ref[i]
        nxt_e = jnp.where(nxt < 0, te_ref[0], nxt)

        @pl.when(is_first)
        def _():
            fetch(te_ref[i], slot).wait()

        def compute(n, casting):
            cw = cw_ref[0:n, :].astype(F32)
            cw = cw[:, 0:1] + cw[:, 1:2] + cw[:, 2:3]
            hrow = h_ref[0:n, :]

            def cast(s):
                cols = slice(s * MOE_NB, (s + 1) * MOE_NB)
                wd_bf[:, cols] = wd_st[slot, :, cols].astype(BF16)

            n_sub = D_MODEL // MOE_NB
            if casting:
                fetch(nxt_e, 1 - slot).start()
                cast(0)
            for s in range(n_sub):
                if casting and s + 1 < n_sub:
                    cast(s + 1)
                cols = slice(s * MOE_NB, (s + 1) * MOE_NB)
                y = (jnp.dot(hrow, wd_bf[:, cols], preferred_element_type=F32) + bd_ref[:, cols]) * cw
                y_ref[0:n, cols] = y.astype(y_ref.dtype)

        _per_fill(rows_ref[i], is_first, compute)

        @pl.when(is_first & (nxt < 0))
        def _():
            fetch(te_ref[0], 1 - slot).wait()


def _moe_down(tiles, hs, xs, wd, bd):
    n_rows = hs.shape[0]
    n_tiles = n_rows // MOE_TM
    row = lambda i, te, fi, nx, rw, nu, pr: (jnp.minimum(i, nu[0] - 1), 0)
    return pl.pallas_call(
        _moe_down_kernel,
        grid_spec=pltpu.PrefetchScalarGridSpec(
            num_scalar_prefetch=6,
            grid=(n_tiles,),
            in_specs=[
                pl.BlockSpec((MOE_TM, D_FF), row),
                pl.BlockSpec((MOE_TM, LANE),
                             lambda i, te, fi, nx, rw, nu, pr: (jnp.minimum(i, nu[0] - 1), D_MODEL // LANE)),
                pl.BlockSpec(memory_space=pl.ANY),
                pl.BlockSpec((None, 1, D_MODEL), lambda i, te, fi, nx, rw, nu, pr: (te[i], 0, 0)),
            ],
            out_specs=pl.BlockSpec((MOE_TM, D_MODEL), row),
            scratch_shapes=[pltpu.VMEM((2, D_FF, D_MODEL), F32), pltpu.VMEM((D_FF, D_MODEL), BF16),
                            pltpu.SemaphoreType.DMA((2,))],
        ),
        out_shape=jax.ShapeDtypeStruct((n_rows, D_MODEL), BF16),
        compiler_params=_cparams(("arbitrary",)),
        name="moe_down",
    )(*tiles, hs, xs, wd, bd)


def _combine_kernel(src_ref, dst_ref, nch_ref, tot_ref, h_ref, slot_ref, ys_ref, o_ref, buf_ref, sems):
    tt = pl.program_id(0)
    n_tt = pl.num_programs(0)
    cur = tt & 1

    def copy(slot, sorted_chunk, local_chunk):
        return pltpu.make_async_copy(_chunk_rows(ys_ref, sorted_chunk), _chunk_rows(buf_ref.at[slot], local_chunk),
                                     sems.at[slot])

    def gather(tile, slot):
        def per_expert(e, _):
            n = nch_ref[tile * N_EXPERTS + e]
            loc0 = src_ref[tile * N_EXPERTS + e]
            srt0 = dst_ref[tile * N_EXPERTS + e]

            def issue(kk, _):
                copy(slot, srt0 + kk, loc0 + kk).start()
                return 0

            lax.fori_loop(0, n, issue, 0)
            return 0

        lax.fori_loop(0, N_EXPERTS, per_expert, 0)

    @pl.when(tt == 0)
    def _():
        buf_ref[...] = jnp.zeros_like(buf_ref)
        gather(0, 0)

    @pl.when(tt + 1 < n_tt)
    def _():
        gather(tt + 1, 1 - cur)

    def one(_, c):
        copy(cur, 0, 0).wait()
        return c

    lax.fori_loop(0, tot_ref[tt], one, 0)

    o_ref[...] = h_ref[...]
    for blk in range(SLOTS // SLOT_BLK):
        sid = lax.broadcasted_iota(I32, (TOK_TILE, SLOT_BLK), 1) + blk * SLOT_BLK
        hit = jnp.zeros((TOK_TILE, SLOT_BLK), F32)
        for kk in range(TOP_K):
            hit = jnp.where(sid == slot_ref[:, kk:kk + 1], 1.0, hit)
        yb = buf_ref[cur, blk * SLOT_BLK:(blk + 1) * SLOT_BLK, :]
        o_ref[...] += jnp.dot(hit.astype(BF16), yb, preferred_element_type=F32)


def _combine(tables, h, slots_t, ys):
    seg_src, seg_dst, nch, tot = tables
    t = h.shape[0]
    return pl.pallas_call(
        _combine_kernel,
        grid_spec=pltpu.PrefetchScalarGridSpec(
            num_scalar_prefetch=4,
            grid=(t // TOK_TILE,),
            in_specs=[
                pl.BlockSpec((TOK_TILE, D_MODEL), lambda i, *_: (i, 0)),
                pl.BlockSpec((TOK_TILE, TOP_K), lambda i, *_: (i, 0)),
                pl.BlockSpec(memory_space=pl.ANY),
            ],
            out_specs=pl.BlockSpec((TOK_TILE, D_MODEL), lambda i, *_: (i, 0)),
            scratch_shapes=[pltpu.VMEM((2, SLOTS, D_MODEL), BF16), pltpu.SemaphoreType.DMA((2,))],
        ),
        out_shape=jax.ShapeDtypeStruct((t, D_MODEL), F32),
        compiler_params=_cparams(("arbitrary",)),
        name="combine",
    )(seg_src, seg_dst, nch, tot, h, slots_t, ys)


def kernel(x, attn_norm_w, w_in, da_q_norm_w, da_k_norm_w, da_lambda_q1, da_lambda_k1, da_lambda_q2,
           da_lambda_k2, da_subln_w, gla_gate_up_w, gla_gate_up_b, gla_norm_w, w_out, ffn_norm_w,
           router_w, router_b, exp_w_gate, exp_b_gate, exp_w_up, exp_b_up, exp_w_down, exp_b_down):
    b, s, d = x.shape
    t = b * s
    assert d == D_MODEL and t % TOK_TILE == 0 and s % CHUNK == 0 and attn_norm_w.shape[0] == 1
    x2 = x.reshape(t, d)

    w_tail = jnp.pad(w_in[0, :, IN_COLS_MAIN:].astype(BF16), ((0, 0), (0, LANE - GLA_RANK)))
    hn, gd = _rms_gd(x2, attn_norm_w, w_tail)
    proj3 = _in_proj(hn, w_in.reshape(D_MODEL, IN_COLS).T).reshape(b, s, IN_COLS_MAIN)

    out_a = _diff_attn(proj3, jnp.tile(da_q_norm_w, (1, 2 * ATT_HP)), jnp.tile(da_k_norm_w, (1, 2 * ATT_HP)),
                       da_lambda_q1, da_lambda_k1, da_lambda_q2, da_lambda_k2, da_subln_w)
    wup_pad = jnp.pad(gla_gate_up_w[0].astype(BF16), ((0, LANE - GLA_RANK), (0, 0)))
    out_b = _gla(proj3, gd.reshape(b, s, LANE), wup_pad, gla_gate_up_b, gla_norm_w)

    h, hn2, slots, top_w, nch_b = _out_router(
        x2, out_a.reshape(t, -1), out_b.reshape(t, -1), w_out[0].astype(BF16), ffn_norm_w,
        router_w[0].T, router_b.reshape(N_EXPERTS, 1))

    n_tok_tiles = t // TOK_TILE
    nch = nch_b[:, 0].reshape(n_tok_tiles, N_EXPERTS)
    n_row_tiles = (TOP_K * t + n_tok_tiles * N_EXPERTS * (ROW_CHUNK - 1)) // MOE_TM + N_EXPERTS
    seg_src, seg_dst, nch_f, gap_start, gap_len, tot_scatter, tot_gather, tiles = _routing_tables(nch, n_row_tiles)

    xs = _compact((seg_src, seg_dst, nch_f, gap_start, gap_len, tot_scatter), hn2, slots, top_w,
                  n_row_tiles * MOE_TM)
    hs = _moe_up(tiles, xs,
                 exp_w_gate.reshape(N_EXPERTS, D_MODEL, D_FF), exp_w_up.reshape(N_EXPERTS, D_MODEL, D_FF),
                 exp_b_gate.reshape(N_EXPERTS, 1, D_FF), exp_b_up.reshape(N_EXPERTS, 1, D_FF))
    ys = _moe_down(tiles, hs, xs, exp_w_down.reshape(N_EXPERTS, D_FF, D_MODEL),
                   exp_b_down.reshape(N_EXPERTS, 1, D_MODEL))
    out = _combine((seg_src, seg_dst, nch_f, tot_gather), h, slots.T, ys)
    return out.reshape(b, s, d)
```

```python
import functools
import math

import jax
import jax.numpy as jnp
from jax import lax
from jax.experimental import pallas as pl
from jax.experimental.pallas import tpu as pltpu

F32 = jnp.float32
BF16 = jnp.bfloat16
I32 = jnp.int32

D_MODEL = 2048
CHUNK = 64
EPS = 1e-6
DA_HEADS = 8
DA_QK = 64
DA_V = 128
GLA_HEADS = 4
GLA_DK = 128
GLA_DV = 256
GLA_RANK = 16
GLA_TAU = 16.0
N_EXPERTS = 32
TOP_K = 4
D_FF = 2048
SWIGLU_LIMIT = 7.0
SWIGLU_ALPHA = 1.702
LAMBDA_INIT = 0.8 - 0.6 * math.exp(-0.3 * 0)

IN_COLS = 6160
IN_COLS_MAIN = IN_COLS - GLA_RANK
LANE = 128
ROW_CHUNK = 16

IN_TM = 1024
IN_TN = 1024
ATT_BQ = 256
ATT_HP = 8
ATT_SAFE_BOUND = 45.0
TOK_TILE = 512
SLOTS = TOP_K * TOK_TILE + N_EXPERTS * ROW_CHUNK
SLOT_BLK = 256
XS_COLS = D_MODEL + LANE
MOE_TM = 512
MOE_SUB = 128
MOE_FC = 1024
MOE_NB = 256
VMEM_LIMIT = 56 * 1024 * 1024


def _cparams(sem):
    return pltpu.CompilerParams(dimension_semantics=sem, vmem_limit_bytes=VMEM_LIMIT)


def _rms_gd_kernel(x_ref, nw_ref, wt_ref, hn_ref, gd_ref):
    x = x_ref[...]
    ms = jnp.mean(x * x, axis=-1, keepdims=True)
    hn = (x * lax.rsqrt(ms + EPS) * nw_ref[...]).astype(BF16)
    hn_ref[...] = hn
    gd_ref[...] = jnp.dot(hn, wt_ref[...], preferred_element_type=F32).astype(gd_ref.dtype)


def _rms_gd(x2, norm_w, w_tail):
    t = x2.shape[0]
    tm = min(IN_TM, t)
    return pl.pallas_call(
        _rms_gd_kernel,
        grid=(t // tm,),
        in_specs=[
            pl.BlockSpec((tm, D_MODEL), lambda i: (i, 0)),
            pl.BlockSpec((1, D_MODEL), lambda i: (0, 0)),
            pl.BlockSpec((D_MODEL, LANE), lambda i: (0, 0)),
        ],
        out_specs=[pl.BlockSpec((tm, D_MODEL), lambda i: (i, 0)), pl.BlockSpec((tm, LANE), lambda i: (i, 0))],
        out_shape=[jax.ShapeDtypeStruct((t, D_MODEL), BF16), jax.ShapeDtypeStruct((t, LANE), BF16)],
        compiler_params=_cparams(("parallel",)),
        name="rms_gd",
    )(x2, norm_w, w_tail)


def _in_proj_kernel(hn_ref, wt_ref, o_ref, w_bf):
    @pl.when(pl.program_id(1) == 0)
    def _():
        w_bf[...] = wt_ref[...].T.astype(BF16)

    o_ref[...] = jnp.dot(hn_ref[...], w_bf[...], preferred_element_type=F32).astype(o_ref.dtype)


def _in_proj(hn, w_in_t):
    t = hn.shape[0]
    tm = min(IN_TM, t)
    return pl.pallas_call(
        _in_proj_kernel,
        grid=(IN_COLS_MAIN // IN_TN, t // tm),
        in_specs=[
            pl.BlockSpec((tm, D_MODEL), lambda j, i: (i, 0)),
            pl.BlockSpec((IN_TN, D_MODEL), lambda j, i: (j, 0)),
        ],
        out_specs=pl.BlockSpec((tm, IN_TN), lambda j, i: (i, j)),
        out_shape=jax.ShapeDtypeStruct((t, IN_COLS_MAIN), BF16),
        scratch_shapes=[pltpu.VMEM((D_MODEL, IN_TN), BF16)],
        compiler_params=_cparams(("arbitrary", "arbitrary")),
        name="in_proj",
    )(hn, w_in_t)


def _attn_kernel(q_ref, k_ref, v_ref, qw_ref, kw_ref, lq1_ref, lk1_ref, lq2_ref, lk2_ref, sw_ref,
                 o_ref, kn_ref, vt_ref, *, bq):
    qi = pl.program_id(2)
    first_map = lax.broadcasted_iota(I32, (1, LANE), 1) < DA_QK
    n_grp = 2 * bq // LANE
    shift = CHUNK.bit_length() - 1

    def qk_norm(x, w):
        x2 = x * x
        scale = []
        for hh in range(ATT_HP):
            xh = x2[:, hh * LANE:(hh + 1) * LANE]
            s1 = jnp.sum(jnp.where(first_map, xh, 0.0), axis=-1, keepdims=True)
            s2 = jnp.sum(jnp.where(first_map, 0.0, xh), axis=-1, keepdims=True)
            scale.append(jnp.where(first_map, lax.rsqrt(s1 / DA_QK + EPS), lax.rsqrt(s2 / DA_QK + EPS)))
        return x * jnp.concatenate(scale, axis=1) * w

    @pl.when(qi == 0)
    def _():
        kn_ref[...] = qk_norm(k_ref[...].astype(F32), kw_ref[...]).astype(BF16)
        vt_ref[...] = v_ref[...].astype(F32).T.astype(BF16)

    q = qk_norm(q_ref[...].astype(F32), qw_ref[...]) * (DA_QK ** -0.5 * math.log2(math.e))
    q_t = []
    for hh in range(ATT_HP):
        qh = q[:, hh * LANE:(hh + 1) * LANE]
        q_t.append(jnp.concatenate([jnp.where(first_map, qh, 0.0), jnp.where(first_map, 0.0, qh)],
                                   axis=0).T.astype(BF16))

    bound = jnp.max(jnp.abs(qw_ref[...] * kw_ref[...]), axis=-1, keepdims=True) * (
        DA_QK ** 0.5 * math.log2(math.e) * 1.02)

    def step(j, carry, masked, fixed_shift):
        r0 = pl.multiple_of(j * bq, bq)
        scores = [jnp.dot(kn_ref[pl.ds(r0, bq), hh * LANE:(hh + 1) * LANE], q_t[hh], preferred_element_type=F32)
                  for hh in range(ATT_HP)]
        out = []
        for hh in range(ATT_HP):
            ms, ls, alphas, ps = [], [], [], []
            for g in range(n_grp):
                m, l, _ = carry[hh][g]
                s = scores[hh][:, g * LANE:(g + 1) * LANE]
                if masked:
                    key_chunk = lax.broadcasted_iota(I32, (bq, LANE), 0) >> shift
                    qry_chunk = (lax.broadcasted_iota(I32, (bq, LANE), 1) + (g * LANE) % bq) >> shift
                    s = jnp.where(key_chunk <= qry_chunk, s, -jnp.inf)
                if fixed_shift:
                    p = jnp.exp2(s - bound)
                    ms.append(m)
                    ls.append(l + jnp.sum(p, axis=0, keepdims=True))
                else:
                    m_new = jnp.maximum(m, jnp.max(s, axis=0, keepdims=True))
                    alpha = jnp.exp2(m - m_new)
                    p = jnp.exp2(s - m_new)
                    ms.append(m_new)
                    ls.append(alpha * l + jnp.sum(p, axis=0, keepdims=True))
                    alphas.append(alpha)
                ps.append(p.astype(BF16))
            pv = jnp.dot(vt_ref[hh * DA_V:(hh + 1) * DA_V, pl.ds(r0, bq)], jnp.concatenate(ps, axis=1),
                         preferred_element_type=F32)
            out.append(tuple((ms[g], ls[g],
                              (carry[hh][g][2] if fixed_shift else alphas[g] * carry[hh][g][2])
                              + pv[:, g * LANE:(g + 1) * LANE]) for g in range(n_grp)))
        return tuple(out)

    lam = (jnp.exp(jnp.sum(lq1_ref[...] * lk1_ref[...], axis=-1, keepdims=True))
           - jnp.exp(jnp.sum(lq2_ref[...] * lk2_ref[...], axis=-1, keepdims=True)) + LAMBDA_INIT)

    def attend(fixed_shift):
        init = tuple(tuple((jnp.full((1, LANE), -jnp.inf, F32), jnp.zeros((1, LANE), F32),
                            jnp.zeros((DA_V, LANE), F32)) for _ in range(n_grp)) for _ in range(ATT_HP))
        carry = lax.fori_loop(0, qi, lambda j, c: step(j, c, False, fixed_shift), init)
        carry = step(qi, carry, True, fixed_shift)
        half = n_grp // 2
        for hh in range(ATT_HP):
            for g in range(half):
                o1 = carry[hh][g][2] / carry[hh][g][1]
                o2 = carry[hh][g + half][2] / carry[hh][g + half][1]
                out = o1 - lam * o2
                ms = jnp.mean(out * out, axis=0, keepdims=True)
                y = out * lax.rsqrt(ms + EPS) * sw_ref[...] * (1.0 - LAMBDA_INIT)
                o_ref[g * LANE:(g + 1) * LANE, hh * DA_V:(hh + 1) * DA_V] = y.T.astype(o_ref.dtype)

    small = jnp.max(bound) <= ATT_SAFE_BOUND

    @pl.when(small)
    def _():
        attend(True)

    @pl.when(jnp.logical_not(small))
    def _():
        attend(False)


def _diff_attn(proj3, qw, kw, lq1, lk1, lq2, lk2, sw):
    b, s, _ = proj3.shape
    bq = min(ATT_BQ, s)
    hw = ATT_HP * LANE
    n_hg = DA_HEADS // ATT_HP
    vec = lambda n: pl.BlockSpec((1, n), lambda bi, h, qi: (0, 0))
    return pl.pallas_call(
        functools.partial(_attn_kernel, bq=bq),
        grid=(b, n_hg, s // bq),
        in_specs=[
            pl.BlockSpec((None, bq, hw), lambda bi, h, qi: (bi, qi, h)),
            pl.BlockSpec((None, s, hw), lambda bi, h, qi: (bi, 0, n_hg + h)),
            pl.BlockSpec((None, s, hw), lambda bi, h, qi: (bi, 0, 2 * n_hg + h)),
            vec(hw), vec(hw), vec(DA_QK), vec(DA_QK), vec(DA_QK), vec(DA_QK),
            pl.BlockSpec((DA_V, 1), lambda bi, h, qi: (0, 0)),
        ],
        out_specs=pl.BlockSpec((None, bq, hw), lambda bi, h, qi: (bi, qi, h)),
        out_shape=jax.ShapeDtypeStruct((b, s, DA_HEADS * DA_V), BF16),
        scratch_shapes=[pltpu.VMEM((s, hw), BF16), pltpu.VMEM((hw, s), BF16)],
        compiler_params=_cparams(("parallel", "parallel", "arbitrary")),
        name="diff_attn",
    )(proj3, proj3, proj3, qw, kw, lq1, lk1, lq2, lk2, sw.reshape(DA_V, 1))


GLA_SUB = 16


def _gla_kernel(q_ref, k_ref, v_ref, r_ref, gd_ref, wup_ref, bup_ref, nw_ref, o_ref, st_ref, *, n_chunks):
    st_ref[...] = jnp.zeros_like(st_ref)
    rr = lax.broadcasted_iota(I32, (CHUNK, CHUNK), 0)
    cc = lax.broadcasted_iota(I32, (CHUNK, CHUNK), 1)
    tril = jnp.where(cc <= rr, 1.0, 0.0).astype(BF16)
    sub_row = lax.broadcasted_iota(I32, (GLA_SUB, GLA_DK), 0)
    sub_lane = lax.broadcasted_iota(I32, (GLA_SUB, CHUNK), 1)
    chunk_row = lax.broadcasted_iota(I32, (CHUNK, GLA_DK), 0)
    nt = (((1,), (1,)), ((), ()))
    tn = (((0,), (0,)), ((), ()))

    def chunk(c, _):
        r0 = pl.multiple_of(c * CHUNK, CHUNK)
        z = jnp.dot(gd_ref[pl.ds(r0, CHUNK), :], wup_ref[...], preferred_element_type=F32) + bup_ref[...]
        lg = (jnp.minimum(z, 0.0) - jnp.log(1.0 + jnp.exp(-jnp.abs(z)))) / GLA_TAU
        lg_hi = lg.astype(BF16)
        lg_lo = (lg - lg_hi.astype(F32)).astype(BF16)
        bcum = (jnp.dot(tril, lg_hi, preferred_element_type=F32)
                + jnp.dot(tril, lg_lo, preferred_element_type=F32))
        bprev = bcum - lg
        for hh in range(GLA_HEADS):
            kcol = slice(hh * GLA_DK, (hh + 1) * GLA_DK)
            head_chunk(r0, hh, bcum[:, kcol], bprev[:, kcol])
        return 0

    def head_chunk(r0, hh, bcum, bprev):
        kcol = slice(hh * GLA_DK, (hh + 1) * GLA_DK)
        vcol = slice(hh * GLA_DV, (hh + 1) * GLA_DV)
        q = q_ref[pl.ds(r0, CHUNK), kcol].astype(F32) * (GLA_DK ** -0.5)
        k = k_ref[pl.ds(r0, CHUNK), kcol].astype(F32)
        v = v_ref[pl.ds(r0, CHUNK), vcol]
        st = st_ref[hh]
        o_inter = lax.dot_general((q * jnp.exp(bcum)).astype(BF16), st.astype(BF16), nt,
                                  preferred_element_type=F32)
        a_rows = []
        for blk in range(CHUNK // GLA_SUB):
            lo = blk * GLA_SUB
            hi = lo + GLA_SUB
            b_blk = bcum[lo:hi]
            q_blk = q[lo:hi]
            if blk > 0:
                ref = bprev[lo:lo + 1]
                qt = (q_blk * jnp.exp(b_blk - ref)).astype(BF16)
                kt = (k * jnp.exp(jnp.where(chunk_row < lo, ref - bcum, -jnp.inf))).astype(BF16)
                a_blk = lax.dot_general(qt, kt, nt, preferred_element_type=F32)
            else:
                a_blk = jnp.zeros((GLA_SUB, CHUNK), F32)
            for jj in range(GLA_SUB):
                dlt = jnp.where(sub_row >= jj, b_blk - bcum[lo + jj:lo + jj + 1], -jnp.inf)
                col = jnp.sum(q_blk * k[lo + jj:lo + jj + 1] * jnp.exp(dlt), axis=-1, keepdims=True)
                a_blk = a_blk + jnp.where(sub_lane == lo + jj, col, 0.0)
            a_rows.append(a_blk.astype(BF16))
        o = o_inter + jnp.dot(jnp.concatenate(a_rows, axis=0), v, preferred_element_type=F32)
        ms = jnp.mean(o * o, axis=-1, keepdims=True)
        rg = r_ref[pl.ds(r0, CHUNK), vcol].astype(F32)
        y = o * lax.rsqrt(ms + EPS) * nw_ref[...] * (rg / (1.0 + jnp.exp(-rg)))
        o_ref[pl.ds(r0, CHUNK), vcol] = y.astype(o_ref.dtype)
        b_last = bcum[CHUNK - 1:CHUNK]
        k_dec = (k * jnp.exp(b_last - bcum)).astype(BF16)
        st_ref[hh] = st * jnp.exp(b_last) + lax.dot_general(v, k_dec, tn, preferred_element_type=F32)

    lax.fori_loop(0, n_chunks, chunk, 0)


def _gla(proj3, gd3, wup_pad, bup, nw):
    b, s, _ = proj3.shape
    kw = GLA_HEADS * GLA_DK
    vw = GLA_HEADS * GLA_DV
    q0 = 3 * DA_HEADS * DA_V // kw
    k0 = q0 + 1
    v0 = (k0 + 1) * kw // vw
    r0 = v0 + 1
    return pl.pallas_call(
        functools.partial(_gla_kernel, n_chunks=s // CHUNK),
        grid=(b,),
        in_specs=[
            pl.BlockSpec((None, s, kw), lambda bi: (bi, 0, q0)),
            pl.BlockSpec((None, s, kw), lambda bi: (bi, 0, k0)),
            pl.BlockSpec((None, s, vw), lambda bi: (bi, 0, v0)),
            pl.BlockSpec((None, s, vw), lambda bi: (bi, 0, r0)),
            pl.BlockSpec((None, s, LANE), lambda bi: (bi, 0, 0)),
            pl.BlockSpec((LANE, kw), lambda bi: (0, 0)),
            pl.BlockSpec((1, kw), lambda bi: (0, 0)),
            pl.BlockSpec((1, GLA_DV), lambda bi: (0, 0)),
        ],
        out_specs=pl.BlockSpec((None, s, vw), lambda bi: (bi, 0, 0)),
        out_shape=jax.ShapeDtypeStruct((b, s, vw), BF16),
        scratch_shapes=[pltpu.VMEM((GLA_HEADS, GLA_DV, GLA_DK), F32)],
        compiler_params=_cparams(("parallel",)),
        name="gla",
    )(proj3, proj3, proj3, proj3, gd3, wup_pad, bup, nw)


def _out_router_kernel(x_ref, a_ref, b_ref, wo_ref, nw_ref, rwt_ref, rb_ref,
                       h_ref, hn_ref, slot_ref, w_ref, nch_ref):
    tm = x_ref.shape[0]
    half = a_ref.shape[1]
    h = (x_ref[...]
         + jnp.dot(a_ref[...], wo_ref[0:half, :], preferred_element_type=F32)
         + jnp.dot(b_ref[...], wo_ref[half:2 * half, :], preferred_element_type=F32))
    h_ref[...] = h
    ms = jnp.mean(h * h, axis=-1, keepdims=True)
    hn = h * lax.rsqrt(ms + EPS) * nw_ref[...]
    hn_hi = hn.astype(BF16)
    hn_ref[...] = hn_hi
    hn_lo = (hn - hn_hi.astype(F32)).astype(BF16)
    rwt = rwt_ref[...]
    rw_hi = rwt.astype(BF16)
    rw_lo = (rwt - rw_hi.astype(F32)).astype(BF16)
    nt = (((1,), (1,)), ((), ()))
    logits = (lax.dot_general(rw_hi, hn_hi, nt, preferred_element_type=F32)
              + lax.dot_general(rw_hi, hn_lo, nt, preferred_element_type=F32)
              + lax.dot_general(rw_lo, hn_hi, nt, preferred_element_type=F32)
              + rb_ref[...])
    e_iota = lax.broadcasted_iota(I32, (N_EXPERTS, tm), 0).astype(F32)
    vals, hots = [], []
    for _ in range(TOP_K):
        m = jnp.max(logits, axis=0, keepdims=True)
        idx = jnp.min(jnp.where(logits == m, e_iota, float(N_EXPERTS)), axis=0, keepdims=True)
        hot = e_iota == idx
        vals.append(m)
        hots.append(hot)
        logits = jnp.where(hot, -jnp.inf, logits)
    exps = [jnp.exp(v - vals[0]) for v in vals]
    denom = exps[0] + exps[1] + exps[2] + exps[3]
    for kk in range(TOP_K):
        w_ref[kk:kk + 1, :] = exps[kk] / denom
    sel = jnp.zeros((N_EXPERTS, tm), F32)
    for hot in hots:
        sel = sel + jnp.where(hot, 1.0, 0.0)
    tr = lax.broadcasted_iota(I32, (tm, tm), 0)
    tc = lax.broadcasted_iota(I32, (tm, tm), 1)
    upper = jnp.where(tr < tc, 1.0, 0.0).astype(BF16)
    rank = jnp.dot(sel.astype(BF16), upper, preferred_element_type=F32)
    cnt = jnp.sum(sel, axis=1, keepdims=True)
    nch = jnp.floor((cnt + (ROW_CHUNK - 1)) * (1.0 / ROW_CHUNK))
    nch_b = jnp.broadcast_to(nch, (N_EXPERTS, LANE))
    er = lax.broadcasted_iota(I32, (N_EXPERTS, N_EXPERTS), 0)
    ec = lax.broadcasted_iota(I32, (N_EXPERTS, N_EXPERTS), 1)
    lower = jnp.where(ec < er, 1.0, 0.0).astype(BF16)
    seg = jnp.dot(lower, nch_b.astype(BF16), preferred_element_type=F32)
    pos = seg[:, 0:1] * ROW_CHUNK + rank
    for kk in range(TOP_K):
        slot_ref[kk:kk + 1, :] = jnp.sum(jnp.where(hots[kk], pos, 0.0), axis=0, keepdims=True).astype(I32)
    nch_ref[...] = nch_b.astype(I32)


def _out_router(x2, out_a, out_b, wo_bf, nw, rwt, rb):
    t = x2.shape[0]
    tm = TOK_TILE
    half = out_a.shape[1]
    nt = t // tm
    return pl.pallas_call(
        _out_router_kernel,
        grid=(nt,),
        in_specs=[
            pl.BlockSpec((tm, D_MODEL), lambda i: (i, 0)),
            pl.BlockSpec((tm, half), lambda i: (i, 0)),
            pl.BlockSpec((tm, half), lambda i: (i, 0)),
            pl.BlockSpec((2 * half, D_MODEL), lambda i: (0, 0)),
            pl.BlockSpec((1, D_MODEL), lambda i: (0, 0)),
            pl.BlockSpec((N_EXPERTS, D_MODEL), lambda i: (0, 0)),
            pl.BlockSpec((N_EXPERTS, 1), lambda i: (0, 0)),
        ],
        out_specs=[
            pl.BlockSpec((tm, D_MODEL), lambda i: (i, 0)),
            pl.BlockSpec((tm, D_MODEL), lambda i: (i, 0)),
            pl.BlockSpec((TOP_K, tm), lambda i: (0, i)),
            pl.BlockSpec((TOP_K, tm), lambda i: (0, i)),
            pl.BlockSpec((N_EXPERTS, LANE), lambda i: (i, 0)),
        ],
        out_shape=[
            jax.ShapeDtypeStruct((t, D_MODEL), F32),
            jax.ShapeDtypeStruct((t, D_MODEL), BF16),
            jax.ShapeDtypeStruct((TOP_K, t), I32),
            jax.ShapeDtypeStruct((TOP_K, t), F32),
            jax.ShapeDtypeStruct((nt * N_EXPERTS, LANE), I32),
        ],
        compiler_params=_cparams(("parallel",)),
        name="out_router",
    )(x2, out_a, out_b, wo_bf, nw, rwt, rb)


def _routing_tables(nch, n_row_tiles):
    cpt = MOE_TM // ROW_CHUNK
    seg_src = jnp.cumsum(nch, axis=1) - nch
    rows_e = jnp.sum(nch, axis=0)
    tiles_e = (rows_e + cpt - 1) // cpt
    cum_tiles = jnp.cumsum(tiles_e)
    base_e = (cum_tiles - tiles_e) * cpt
    seg_dst = base_e[None, :] + jnp.cumsum(nch, axis=0) - nch
    gap_start = base_e + rows_e
    cps = MOE_SUB // ROW_CHUNK
    gap_len = (rows_e + cps - 1) // cps * cps - rows_e
    n_used = cum_tiles[-1]
    tile_ids = jnp.arange(n_row_tiles, dtype=I32)
    tile_e = jnp.sum((cum_tiles[None, :] <= jnp.minimum(tile_ids, n_used - 1)[:, None]).astype(I32), axis=1)
    tile_e = jnp.minimum(tile_e, N_EXPERTS - 1)
    first = jnp.concatenate([jnp.ones((1,), I32), (tile_e[1:] != tile_e[:-1]).astype(I32)])
    e_ids = jnp.arange(N_EXPERTS, dtype=I32)
    later = (e_ids[None, :] > e_ids[:, None]) & (tiles_e[None, :] > 0)
    next_e = jnp.min(jnp.where(later, e_ids[None, :], N_EXPERTS), axis=1)
    next_e = jnp.where(next_e == N_EXPERTS, -1, next_e)
    own = (tile_e[:, None] == e_ids[None, :]).astype(I32)
    pick = lambda per_expert: jnp.sum(own * per_expert[None, :], axis=1)
    tile_next = pick(next_e)
    tile_k = tile_ids - pick(cum_tiles - tiles_e)
    tile_rows = jnp.clip(pick(rows_e) * ROW_CHUNK - tile_k * MOE_TM, 0, MOE_TM)
    group_idx = jnp.cumsum(first) - 1
    n_groups = jnp.sum((tiles_e > 0).astype(I32))
    stage_slot = (jnp.arange(D_FF // MOE_FC, dtype=I32)[:, None] * n_groups + group_idx[None, :]) & 1
    tot_gather = jnp.sum(nch, axis=1)
    is_last = jnp.arange(nch.shape[0], dtype=I32) == nch.shape[0] - 1
    tot_scatter = tot_gather + jnp.where(is_last, jnp.sum(gap_len), 0)
    flat = lambda a: a.reshape(-1).astype(I32)
    return (flat(seg_src), flat(seg_dst), flat(nch), flat(gap_start), flat(gap_len),
            flat(tot_scatter), flat(tot_gather),
            (tile_e.astype(I32), first, tile_next.astype(I32), tile_rows.astype(I32),
             n_used.reshape(1).astype(I32), flat(stage_slot)))


def _chunk_rows(ref, chunk_idx):
    return ref.at[pl.ds(pl.multiple_of(chunk_idx * ROW_CHUNK, ROW_CHUNK), ROW_CHUNK), :]


def _compact_kernel(src_ref, dst_ref, nch_ref, gap_start_ref, gap_len_ref, tot_ref,
                    x_ref, slot_ref, w_ref, xs_ref, buf_ref, sems):
    tt = pl.program_id(0)
    n_tt = pl.num_programs(0)
    cur = tt & 1
    x = x_ref[...]
    s0, s1, s2, s3 = (slot_ref[kk:kk + 1, :] for kk in range(TOP_K))
    w0, w1, w2, w3 = (w_ref[kk:kk + 1, :] for kk in range(TOP_K))
    lane = lax.broadcasted_iota(I32, (SLOT_BLK, LANE), 1)
    for blk in range(SLOTS // SLOT_BLK):
        rows = slice(blk * SLOT_BLK, (blk + 1) * SLOT_BLK)
        sid = lax.broadcasted_iota(I32, (SLOT_BLK, TOK_TILE), 0) + blk * SLOT_BLK
        hit = jnp.where(sid == s0, 1.0, jnp.where(sid == s1, 1.0, jnp.where(sid == s2, 1.0,
                        jnp.where(sid == s3, 1.0, 0.0))))
        buf_ref[cur, rows, 0:D_MODEL] = jnp.dot(hit.astype(BF16), x, preferred_element_type=F32).astype(BF16)
        wsel = jnp.where(sid == s0, w0, jnp.where(sid == s1, w1, jnp.where(sid == s2, w2,
                         jnp.where(sid == s3, w3, 0.0))))
        wrow = jnp.sum(wsel, axis=1, keepdims=True)
        t0 = wrow.astype(BF16).astype(F32)
        t1 = (wrow - t0).astype(BF16).astype(F32)
        t2 = wrow - t0 - t1
        buf_ref[cur, rows, D_MODEL:XS_COLS] = jnp.where(
            lane == 0, t0, jnp.where(lane == 1, t1, jnp.where(lane == 2, t2, 0.0))).astype(BF16)

    def copy(slot, src_chunk, dst_chunk):
        return pltpu.make_async_copy(_chunk_rows(buf_ref.at[slot], src_chunk), _chunk_rows(xs_ref, dst_chunk),
                                     sems.at[slot])

    def per_expert(e, _):
        n = nch_ref[tt * N_EXPERTS + e]
        src0 = src_ref[tt * N_EXPERTS + e]
        dst0 = dst_ref[tt * N_EXPERTS + e]

        def issue(kk, _):
            copy(cur, src0 + kk, dst0 + kk).start()
            return 0

        lax.fori_loop(0, n, issue, 0)
        return 0

    lax.fori_loop(0, N_EXPERTS, per_expert, 0)

    zero_chunk = SLOTS // ROW_CHUNK - 1

    def per_gap(e, _):
        n = jnp.where(tt == n_tt - 1, gap_len_ref[e], 0)
        g0 = gap_start_ref[e]

        def issue(kk, _):
            copy(cur, zero_chunk, g0 + kk).start()
            return 0

        lax.fori_loop(0, n, issue, 0)
        return 0

    lax.fori_loop(0, N_EXPERTS, per_gap, 0)

    def drain(slot, count):
        def one(_, c):
            copy(slot, 0, 0).wait()
            return c

        lax.fori_loop(0, count, one, 0)

    @pl.when(tt > 0)
    def _():
        drain(1 - cur, tot_ref[tt - 1])

    @pl.when(tt == n_tt - 1)
    def _():
        drain(cur, tot_ref[tt])


def _compact(tables, hn2, slots, top_w, n_rows):
    seg_src, seg_dst, nch, gap_start, gap_len, tot = tables
    t = hn2.shape[0]
    return pl.pallas_call(
        _compact_kernel,
        grid_spec=pltpu.PrefetchScalarGridSpec(
            num_scalar_prefetch=6,
            grid=(t // TOK_TILE,),
            in_specs=[
                pl.BlockSpec((TOK_TILE, D_MODEL), lambda i, *_: (i, 0)),
                pl.BlockSpec((TOP_K, TOK_TILE), lambda i, *_: (0, i)),
                pl.BlockSpec((TOP_K, TOK_TILE), lambda i, *_: (0, i)),
            ],
            out_specs=pl.BlockSpec(memory_space=pl.ANY),
            scratch_shapes=[pltpu.VMEM((2, SLOTS, XS_COLS), BF16), pltpu.SemaphoreType.DMA((2,))],
        ),
        out_shape=jax.ShapeDtypeStruct((n_rows, XS_COLS), BF16),
        compiler_params=_cparams(("arbitrary",)),
        name="compact",
    )(seg_src, seg_dst, nch, gap_start, gap_len, tot, hn2, slots, top_w)


def _per_fill(rows, is_first, compute):
    for n in range(MOE_SUB, MOE_TM + 1, MOE_SUB):
        fits = (rows > n - MOE_SUB) & (rows <= n)

        @pl.when(fits & is_first)
        def _():
            compute(n, True)

        @pl.when(fits & jnp.logical_not(is_first))
        def _():
            compute(n, False)


def _moe_up_kernel(te_ref, first_ref, next_ref, rows_ref, nu_ref, slot_ref, x_ref, wg_hbm, wu_hbm, bg_ref, bu_ref,
                   h_ref, wg_st, wu_st, wg_bf, wu_bf, sems):
    c = pl.program_id(0)
    i = pl.program_id(1)
    slot = slot_ref[c * pl.num_programs(1) + i]

    def fetch(e, cc, sl):
        col = pl.ds(pl.multiple_of(cc * MOE_FC, MOE_FC), MOE_FC)
        return (pltpu.make_async_copy(wg_hbm.at[e, :, col], wg_st.at[sl], sems.at[sl, 0]),
                pltpu.make_async_copy(wu_hbm.at[e, :, col], wu_st.at[sl], sems.at[sl, 1]))

    @pl.when((c == 0) & (i == 0))
    def _():
        for cp in fetch(te_ref[0], 0, slot):
            cp.start()

    @pl.when(i < nu_ref[0])
    def _():
        is_first = first_ref[i] == 1
        nxt = next_ref[i]
        wraps = nxt < 0
        last_pass = c + 1 == pl.num_programs(0)
        nxt_e = jnp.where(wraps, te_ref[0], nxt)
        nxt_c = jnp.where(wraps, jnp.where(last_pass, 0, c + 1), c)

        @pl.when(is_first)
        def _():
            for cp in fetch(te_ref[i], c, slot):
                cp.wait()

        def compute(n, casting):
            x = x_ref[0:n, :]

            def cast(s):
                cols = slice(s * MOE_NB, (s + 1) * MOE_NB)
                wg_bf[:, cols] = wg_st[slot, :, cols].astype(BF16)
                wu_bf[:, cols] = wu_st[slot, :, cols].astype(BF16)

            n_sub = MOE_FC // MOE_NB
            if casting:
                for cp in fetch(nxt_e, nxt_c, 1 - slot):
                    cp.start()
                cast(0)
            for s in range(n_sub):
                if casting and s + 1 < n_sub:
                    cast(s + 1)
                cols = slice(s * MOE_NB, (s + 1) * MOE_NB)
                gate = jnp.minimum(jnp.dot(x, wg_bf[:, cols], preferred_element_type=F32) + bg_ref[:, cols],
                                   SWIGLU_LIMIT)
                up = jnp.clip(jnp.dot(x, wu_bf[:, cols], preferred_element_type=F32) + bu_ref[:, cols],
                              -SWIGLU_LIMIT, SWIGLU_LIMIT)
                act = (up + 1.0) * gate * (1.0 / (1.0 + jnp.exp(-SWIGLU_ALPHA * gate)))
                h_ref[0:n, cols] = act.astype(h_ref.dtype)

        _per_fill(rows_ref[i], is_first, compute)

        @pl.when(is_first & wraps & last_pass)
        def _():
            for cp in fetch(te_ref[0], 0, 1 - slot):
                cp.wait()


def _moe_up(tiles, xs, wg, wu, bg, bu):
    n_rows = xs.shape[0]
    n_tiles = n_rows // MOE_TM
    row = lambda c, i, te, fi, nx, rw, nu, pr: (jnp.minimum(i, nu[0] - 1), 0)
    bsel = lambda c, i, te, fi, nx, rw, nu, pr: (te[i], 0, c)
    return pl.pallas_call(
        _moe_up_kernel,
        grid_spec=pltpu.PrefetchScalarGridSpec(
            num_scalar_prefetch=6,
            grid=(D_FF // MOE_FC, n_tiles),
            in_specs=[
                pl.BlockSpec((MOE_TM, D_MODEL), row),
                pl.BlockSpec(memory_space=pl.ANY),
                pl.BlockSpec(memory_space=pl.ANY),
                pl.BlockSpec((None, 1, MOE_FC), bsel),
                pl.BlockSpec((None, 1, MOE_FC), bsel),
            ],
            out_specs=pl.BlockSpec((MOE_TM, MOE_FC),
                                   lambda c, i, te, fi, nx, rw, nu, pr: (jnp.minimum(i, nu[0] - 1), c)),
            scratch_shapes=[pltpu.VMEM((2, D_MODEL, MOE_FC), F32), pltpu.VMEM((2, D_MODEL, MOE_FC), F32),
                            pltpu.VMEM((D_MODEL, MOE_FC), BF16), pltpu.VMEM((D_MODEL, MOE_FC), BF16),
                            pltpu.SemaphoreType.DMA((2, 2))],
        ),
        out_shape=jax.ShapeDtypeStruct((n_rows, D_FF), BF16),
        compiler_params=_cparams(("arbitrary", "arbitrary")),
        name="moe_up",
    )(*tiles, xs, wg, wu, bg, bu)


def _moe_down_kernel(te_ref, first_ref, next_ref, rows_ref, nu_ref, slot_ref, h_ref, cw_ref, wd_hbm, bd_ref, y_ref,
                     wd_st, wd_bf, sems):
    i = pl.program_id(0)
    slot = slot_ref[i]

    def fetch(e, sl):
        return pltpu.make_async_copy(wd_hbm.at[e], wd_st.at[sl], sems.at[sl])

    @pl.when(i == 0)
    def _():
        fetch(te_ref[0], slot).start()

    @pl.when(i < nu_ref[0])
    def _():
        is_first = first_ref[i] == 1
        nxt = next_ref[i]
        nxt_e = jnp.where(nxt < 0, te_ref[0], nxt)

        @pl.when(is_first)
        def _():
            fetch(te_ref[i], slot).wait()

        def compute(n, casting):
            cw = cw_ref[0:n, :].astype(F32)
            cw = cw[:, 0:1] + cw[:, 1:2] + cw[:, 2:3]
            hrow = h_ref[0:n, :]

            def cast(s):
                cols = slice(s * MOE_NB, (s + 1) * MOE_NB)
                wd_bf[:, cols] = wd_st[slot, :, cols].astype(BF16)

            n_sub = D_MODEL // MOE_NB
            if casting:
                fetch(nxt_e, 1 - slot).start()
                cast(0)
            for s in range(n_sub):
                if casting and s + 1 < n_sub:
                    cast(s + 1)
                cols = slice(s * MOE_NB, (s + 1) * MOE_NB)
                y = (jnp.dot(hrow, wd_bf[:, cols], preferred_element_type=F32) + bd_ref[:, cols]) * cw
                y_ref[0:n, cols] = y.astype(y_ref.dtype)

        _per_fill(rows_ref[i], is_first, compute)

        @pl.when(is_first & (nxt < 0))
        def _():
            fetch(te_ref[0], 1 - slot).wait()


def _moe_down(tiles, hs, xs, wd, bd):
    n_rows = hs.shape[0]
    n_tiles = n_rows // MOE_TM
    row = lambda i, te, fi, nx, rw, nu, pr: (jnp.minimum(i, nu[0] - 1), 0)
    return pl.pallas_call(
        _moe_down_kernel,
        grid_spec=pltpu.PrefetchScalarGridSpec(
            num_scalar_prefetch=6,
            grid=(n_tiles,),
            in_specs=[
                pl.BlockSpec((MOE_TM, D_FF), row),
                pl.BlockSpec((MOE_TM, LANE),
                             lambda i, te, fi, nx, rw, nu, pr: (jnp.minimum(i, nu[0] - 1), D_MODEL // LANE)),
                pl.BlockSpec(memory_space=pl.ANY),
                pl.BlockSpec((None, 1, D_MODEL), lambda i, te, fi, nx, rw, nu, pr: (te[i], 0, 0)),
            ],
            out_specs=pl.BlockSpec((MOE_TM, D_MODEL), row),
            scratch_shapes=[pltpu.VMEM((2, D_FF, D_MODEL), F32), pltpu.VMEM((D_FF, D_MODEL), BF16),
                            pltpu.SemaphoreType.DMA((2,))],
        ),
        out_shape=jax.ShapeDtypeStruct((n_rows, D_MODEL), BF16),
        compiler_params=_cparams(("arbitrary",)),
        name="moe_down",
    )(*tiles, hs, xs, wd, bd)


def _combine_kernel(src_ref, dst_ref, nch_ref, tot_ref, h_ref, slot_ref, ys_ref, o_ref, buf_ref, sems):
    tt = pl.program_id(0)
    n_tt = pl.num_programs(0)
    cur = tt & 1

    def copy(slot, sorted_chunk, local_chunk):
        return pltpu.make_async_copy(_chunk_rows(ys_ref, sorted_chunk), _chunk_rows(buf_ref.at[slot], local_chunk),
                                     sems.at[slot])

    def gather(tile, slot):
        def per_expert(e, _):
            n = nch_ref[tile * N_EXPERTS + e]
            loc0 = src_ref[tile * N_EXPERTS + e]
            srt0 = dst_ref[tile * N_EXPERTS + e]

            def issue(kk, _):
                copy(slot, srt0 + kk, loc0 + kk).start()
                return 0

            lax.fori_loop(0, n, issue, 0)
            return 0

        lax.fori_loop(0, N_EXPERTS, per_expert, 0)

    @pl.when(tt == 0)
    def _():
        buf_ref[...] = jnp.zeros_like(buf_ref)
        gather(0, 0)

    @pl.when(tt + 1 < n_tt)
    def _():
        gather(tt + 1, 1 - cur)

    def one(_, c):
        copy(cur, 0, 0).wait()
        return c

    lax.fori_loop(0, tot_ref[tt], one, 0)

    o_ref[...] = h_ref[...]
    for blk in range(SLOTS // SLOT_BLK):
        sid = lax.broadcasted_iota(I32, (TOK_TILE, SLOT_BLK), 1) + blk * SLOT_BLK
        hit = jnp.zeros((TOK_TILE, SLOT_BLK), F32)
        for kk in range(TOP_K):
            hit = jnp.where(sid == slot_ref[:, kk:kk + 1], 1.0, hit)
        yb = buf_ref[cur, blk * SLOT_BLK:(blk + 1) * SLOT_BLK, :]
        o_ref[...] += jnp.dot(hit.astype(BF16), yb, preferred_element_type=F32)


def _combine(tables, h, slots_t, ys):
    seg_src, seg_dst, nch, tot = tables
    t = h.shape[0]
    return pl.pallas_call(
        _combine_kernel,
        grid_spec=pltpu.PrefetchScalarGridSpec(
            num_scalar_prefetch=4,
            grid=(t // TOK_TILE,),
            in_specs=[
                pl.BlockSpec((TOK_TILE, D_MODEL), lambda i, *_: (i, 0)),
                pl.BlockSpec((TOK_TILE, TOP_K), lambda i, *_: (i, 0)),
                pl.BlockSpec(memory_space=pl.ANY),
            ],
            out_specs=pl.BlockSpec((TOK_TILE, D_MODEL), lambda i, *_: (i, 0)),
            scratch_shapes=[pltpu.VMEM((2, SLOTS, D_MODEL), BF16), pltpu.SemaphoreType.DMA((2,))],
        ),
        out_shape=jax.ShapeDtypeStruct((t, D_MODEL), F32),
        compiler_params=_cparams(("arbitrary",)),
        name="combine",
    )(seg_src, seg_dst, nch, tot, h, slots_t, ys)


def kernel(x, attn_norm_w, w_in, da_q_norm_w, da_k_norm_w, da_lambda_q1, da_lambda_k1, da_lambda_q2,
           da_lambda_k2, da_subln_w, gla_gate_up_w, gla_gate_up_b, gla_norm_w, w_out, ffn_norm_w,
           router_w, router_b, exp_w_gate, exp_b_gate, exp_w_up, exp_b_up, exp_w_down, exp_b_down):
    b, s, d = x.shape
    t = b * s
    assert d == D_MODEL and t % TOK_TILE == 0 and s % CHUNK == 0 and attn_norm_w.shape[0] == 1
    x2 = x.reshape(t, d)

    w_tail = jnp.pad(w_in[0, :, IN_COLS_MAIN:].astype(BF16), ((0, 0), (0, LANE - GLA_RANK)))
    hn, gd = _rms_gd(x2, attn_norm_w, w_tail)
    proj3 = _in_proj(hn, w_in.reshape(D_MODEL, IN_COLS).T).reshape(b, s, IN_COLS_MAIN)

    out_a = _diff_attn(proj3, jnp.tile(da_q_norm_w, (1, 2 * ATT_HP)), jnp.tile(da_k_norm_w, (1, 2 * ATT_HP)),
                       da_lambda_q1, da_lambda_k1, da_lambda_q2, da_lambda_k2, da_subln_w)
    wup_pad = jnp.pad(gla_gate_up_w[0].astype(BF16), ((0, LANE - GLA_RANK), (0, 0)))
    out_b = _gla(proj3, gd.reshape(b, s, LANE), wup_pad, gla_gate_up_b, gla_norm_w)

    h, hn2, slots, top_w, nch_b = _out_router(
        x2, out_a.reshape(t, -1), out_b.reshape(t, -1), w_out[0].astype(BF16), ffn_norm_w,
        router_w[0].T, router_b.reshape(N_EXPERTS, 1))

    n_tok_tiles = t // TOK_TILE
    nch = nch_b[:, 0].reshape(n_tok_tiles, N_EXPERTS)
    n_row_tiles = (TOP_K * t + n_tok_tiles * N_EXPERTS * (ROW_CHUNK - 1)) // MOE_TM + N_EXPERTS
    seg_src, seg_dst, nch_f, gap_start, gap_len, tot_scatter, tot_gather, tiles = _routing_tables(nch, n_row_tiles)

    xs = _compact((seg_src, seg_dst, nch_f, gap_start, gap_len, tot_scatter), hn2, slots, top_w,
                  n_row_tiles * MOE_TM)
    hs = _moe_up(tiles, xs,
                 exp_w_gate.reshape(N_EXPERTS, D_MODEL, D_FF), exp_w_up.reshape(N_EXPERTS, D_MODEL, D_FF),
                 exp_b_gate.reshape(N_EXPERTS, 1, D_FF), exp_b_up.reshape(N_EXPERTS, 1, D_FF))
    ys = _moe_down(tiles, hs, xs, exp_w_down.reshape(N_EXPERTS, D_FF, D_MODEL),
                   exp_b_down.reshape(N_EXPERTS, 1, D_MODEL))
    out = _combine((seg_src, seg_dst, nch_f, tot_gather), h, slots.T, ys)
    return out.reshape(b, s, d)
```
